```python
import math
import jax
import jax.numpy as jnp
from jax import lax
import numpy as np

D_MODEL = 2048
BATCH = 1
SEQ = 8192
DEPTH = 1

GLA_HEADS = 4
GLA_DK = D_MODEL // 2
GLA_DV = D_MODEL
GLA_HEAD_K = GLA_DK // GLA_HEADS
GLA_HEAD_V = GLA_DV // GLA_HEADS
GLA_GATE_RANK = 16
GLA_GATE_NORMALIZER = 16.0
GLA_CHUNK = 64

SSM_INNER = 2 * D_MODEL
SSM_HEAD_DIM = 64
SSM_HEADS = SSM_INNER // SSM_HEAD_DIM
SSM_GROUPS = 8
SSM_HEADS_PER_GROUP = SSM_HEADS // SSM_GROUPS
SSM_STATE = 128
SSM_CONV = 4
SSM_CHUNK = 64
SSM_CONV_DIM = SSM_INNER + 2 * SSM_GROUPS * SSM_STATE

MOE_GROUPS = 8
MOE_EXPERTS_PER_GROUP = 8
N_EXPERTS = MOE_GROUPS * MOE_EXPERTS_PER_GROUP
MOE_TOPK = 2
MOE_FF = 1024
MOE_BLOCK = 128

N_MOD = 6
EPS = 1e-6

IN_SIZES = (GLA_DK, GLA_DK, GLA_DV, GLA_GATE_RANK, GLA_DV, SSM_INNER, SSM_CONV_DIM, SSM_HEADS, D_MODEL, D_MODEL)
IN_TOTAL = GLA_DK + GLA_DK + GLA_DV + GLA_GATE_RANK + GLA_DV + SSM_INNER + SSM_CONV_DIM + SSM_HEADS + D_MODEL + D_MODEL

kernel_name = 'hybrid_gla_ssd_hmoe_block'


def rms_norm(x, gain):
    xf = x.astype(jnp.float32)
    y = xf * lax.rsqrt(jnp.mean(xf * xf, axis=-1, keepdims=True) + EPS)
    return (y * gain.astype(jnp.float32)).astype(x.dtype)


def causal_depthwise_conv(u, w, b):
    out = lax.conv_general_dilated(
        u, w[:, None, :].astype(u.dtype), window_strides=(1,),
        padding=[(SSM_CONV - 1, 0)], dimension_numbers=('NWC', 'WIO', 'NWC'),
        feature_group_count=u.shape[-1])
    return out + b.astype(u.dtype)


def gla_chunked(q, k, v, log_a):
    bsz, seq, heads, dk = q.shape
    dv = v.shape[-1]
    nc = seq // GLA_CHUNK
    f32 = jnp.float32
    q = (q.astype(f32) * dk ** -0.5).reshape(bsz, nc, GLA_CHUNK, heads, dk)
    k = k.astype(f32).reshape(bsz, nc, GLA_CHUNK, heads, dk)
    vv = v.astype(f32).reshape(bsz, nc, GLA_CHUNK, heads, dv)
    b = jnp.cumsum(log_a.astype(f32).reshape(bsz, nc, GLA_CHUNK, heads, dk), axis=2)
    b_last = b[:, :, -1]
    q_dec = q * jnp.exp(b)
    k_inv = k * jnp.exp(-b)
    k_end = k * jnp.exp(b_last[:, :, None] - b)
    causal = jnp.asarray(np.tril(np.ones((GLA_CHUNK, GLA_CHUNK), dtype=bool)))
    scores = jnp.einsum('bclhk,bcshk->bchls', q_dec, k_inv)
    scores = jnp.where(causal, scores, 0.0)
    o_intra = jnp.einsum('bchls,bcshv->bclhv', scores, vv)
    chunk_upd = jnp.einsum('bcshk,bcshv->bchkv', k_end, vv)

    def step(state, inp):
        decay, upd = inp
        return jnp.exp(decay)[..., None] * state + upd, state

    init = jnp.zeros((bsz, heads, dk, dv), f32)
    _, prev = lax.scan(step, init, (jnp.moveaxis(b_last, 1, 0), jnp.moveaxis(chunk_upd, 1, 0)))
    prev = jnp.moveaxis(prev, 0, 1)
    o_inter = jnp.einsum('bclhk,bchkv->bclhv', q_dec, prev)
    return (o_intra + o_inter).reshape(bsz, seq, heads, dv).astype(v.dtype)


def ssd_chunked(xs, dt, a, bmat, cmat):
    bsz, seq, heads, hd = xs.shape
    nc = seq // SSM_CHUNK
    f32 = jnp.float32
    G, HG, L, N = SSM_GROUPS, SSM_HEADS_PER_GROUP, SSM_CHUNK, SSM_STATE
    dtf = dt.astype(f32)
    x = (xs.astype(f32) * dtf[..., None]).reshape(bsz, nc, L, G, HG, hd)
    a_dt = jnp.moveaxis((dtf * a.astype(f32)).reshape(bsz, nc, L, G, HG), 2, -1)
    bc = bmat.astype(f32).reshape(bsz, nc, L, G, N)
    cc = cmat.astype(f32).reshape(bsz, nc, L, G, N)
    acum = jnp.cumsum(a_dt, axis=-1)
    causal = jnp.asarray(np.tril(np.ones((L, L), dtype=bool)))
    seg = acum[..., :, None] - acum[..., None, :]
    decay = jnp.where(causal, jnp.exp(jnp.where(causal, seg, 0.0)), 0.0)
    cb = jnp.einsum('bclgn,bcsgn->bcgls', cc, bc)
    y_diag = jnp.einsum('bcgls,bcghls,bcsghp->bclghp', cb, decay, x)
    decay_states = jnp.exp(acum[..., -1:] - acum)
    states = jnp.einsum('bclgn,bcghl,bclghp->bcghpn', bc, decay_states, x)
    chunk_decay = jnp.exp(acum[..., -1])

    def step(h, inp):
        d, s = inp
        return d[..., None, None] * h + s, h

    init = jnp.zeros((bsz, G, HG, hd, N), f32)
    _, prev = lax.scan(step, init, (jnp.moveaxis(chunk_decay, 1, 0), jnp.moveaxis(states, 1, 0)))
    prev = jnp.moveaxis(prev, 0, 1)
    y_off = jnp.einsum('bclgn,bcghpn,bcghl->bclghp', cc, prev, jnp.exp(acum))
    return (y_diag + y_off).reshape(bsz, seq, heads, hd).astype(xs.dtype)


def hybrid_mixer(h, w_in, gla_w_gate_up, gla_b_gate, gla_norm, ssm_conv_w, ssm_conv_b,
                 ssm_dt_bias, ssm_a_log, ssm_d, ssm_norm, w_branch_gla, w_branch_ssm, w_out):
    bsz, seq, _ = h.shape
    offs = []
    acc = 0
    for n in IN_SIZES[:-1]:
        acc += n
        offs.append(acc)
    proj = h @ w_in
    q, k, v, gate_lr, og, z, xbc, dt_raw, gate_gla, gate_ssm = jnp.split(proj, offs, axis=-1)

    log_a = jax.nn.log_sigmoid((gate_lr @ gla_w_gate_up + gla_b_gate).astype(jnp.float32)) / GLA_GATE_NORMALIZER
    o = gla_chunked(q.reshape(bsz, seq, GLA_HEADS, GLA_HEAD_K),
                    k.reshape(bsz, seq, GLA_HEADS, GLA_HEAD_K),
                    v.reshape(bsz, seq, GLA_HEADS, GLA_HEAD_V),
                    log_a.reshape(bsz, seq, GLA_HEADS, GLA_HEAD_K))
    o = rms_norm(o, gla_norm.reshape(GLA_HEADS, GLA_HEAD_V)) * jax.nn.silu(og.reshape(bsz, seq, GLA_HEADS, GLA_HEAD_V))
    y_gla = o.reshape(bsz, seq, GLA_DV) @ w_branch_gla

    xbc = jax.nn.silu(causal_depthwise_conv(xbc, ssm_conv_w, ssm_conv_b))
    xs, bm, cm = jnp.split(xbc, [SSM_INNER, SSM_INNER + SSM_GROUPS * SSM_STATE], axis=-1)
    dt = jax.nn.softplus(dt_raw.astype(jnp.float32) + ssm_dt_bias.astype(jnp.float32))
    a = -jnp.exp(ssm_a_log.astype(jnp.float32))
    xs_h = xs.reshape(bsz, seq, SSM_HEADS, SSM_HEAD_DIM)
    y = ssd_chunked(xs_h, dt, a,
                    bm.reshape(bsz, seq, SSM_GROUPS, SSM_STATE),
                    cm.reshape(bsz, seq, SSM_GROUPS, SSM_STATE))
    y = y + ssm_d[:, None].astype(y.dtype) * xs_h
    y = y.reshape(bsz, seq, SSM_INNER) * jax.nn.silu(z)
    y = rms_norm(y.reshape(bsz, seq, SSM_GROUPS, SSM_INNER // SSM_GROUPS),
                 ssm_norm.reshape(SSM_GROUPS, SSM_INNER // SSM_GROUPS)).reshape(bsz, seq, SSM_INNER)
    y_ssm = y @ w_branch_ssm

    merged = jax.nn.sigmoid(gate_gla) * y_gla + jax.nn.sigmoid(gate_ssm) * y_ssm
    return merged @ w_out


def hierarchical_moe(h, router_group, router_expert, w_gate, w_up, w_down):
    bsz, seq, d = h.shape
    t = bsz * seq
    xt = h.reshape(t, d)
    g_prob = jax.nn.softmax((xt @ router_group).astype(jnp.float32), axis=-1)
    g_w, g_idx = lax.top_k(g_prob, 1)
    e_logits = (xt @ router_expert).astype(jnp.float32).reshape(t, MOE_GROUPS, MOE_EXPERTS_PER_GROUP)
    e_in = jnp.take_along_axis(e_logits, g_idx[:, :, None], axis=1)[:, 0]
    e_w, e_idx = lax.top_k(jax.nn.softmax(e_in, axis=-1), MOE_TOPK)
    e_w = e_w / jnp.sum(e_w, axis=-1, keepdims=True)
    weights = g_w * e_w
    expert_id = g_idx * MOE_EXPERTS_PER_GROUP + e_idx

    n_assign = t * MOE_TOPK
    flat_e = expert_id.reshape(n_assign)
    flat_tok = jnp.repeat(jnp.arange(t, dtype=jnp.int32), MOE_TOPK)
    flat_w = weights.reshape(n_assign)
    order = jnp.argsort(flat_e)
    e_sorted = flat_e[order]
    counts = jnp.bincount(flat_e, length=N_EXPERTS)
    padded = (counts + MOE_BLOCK - 1) // MOE_BLOCK * MOE_BLOCK
    pad_end = jnp.cumsum(padded)
    pad_start = pad_end - padded
    start = jnp.cumsum(counts) - counts
    dest = pad_start[e_sorted] + (jnp.arange(n_assign) - start[e_sorted])
    n_blocks = -(-n_assign // MOE_BLOCK) + N_EXPERTS
    n_rows = n_blocks * MOE_BLOCK
    row_tok = jnp.full((n_rows,), t, dtype=jnp.int32).at[dest].set(flat_tok[order])
    row_w = jnp.zeros((n_rows,), jnp.float32).at[dest].set(flat_w[order])
    block_expert = jnp.minimum(
        jnp.searchsorted(pad_end, jnp.arange(n_blocks) * MOE_BLOCK, side='right'), N_EXPERTS - 1)
    x_pad = jnp.concatenate([xt, jnp.zeros((1, d), xt.dtype)], axis=0)
    xb = x_pad[row_tok].reshape(n_blocks, MOE_BLOCK, d)

    def expert_block(args):
        xblk, e = args
        hid = jax.nn.silu(xblk @ w_gate[e]) * (xblk @ w_up[e])
        return hid @ w_down[e]

    yb = lax.map(expert_block, (xb, block_expert))
    y_rows = yb.reshape(n_rows, d) * row_w[:, None].astype(yb.dtype)
    out = jax.ops.segment_sum(y_rows, row_tok, num_segments=t + 1)[:t]
    return out.reshape(bsz, seq, d)


def setup_inputs(seed: int = 0) -> dict:
    key = jax.random.key(seed)
    ks = jax.random.split(key, 32)
    f32 = jnp.float32
    L = DEPTH

    def nrm(k, shape, fan_in):
        return jax.random.normal(k, shape, f32) * fan_in ** -0.5

    def gain(k, shape):
        return 1.0 + 0.02 * jax.random.normal(k, shape, f32)

    dt0 = jnp.exp(jax.random.uniform(ks[15], (L, SSM_HEADS), f32, math.log(1e-3), math.log(1e-1)))
    return {
        'x': jax.random.normal(ks[0], (BATCH, SEQ, D_MODEL), f32),
        'c': jax.random.normal(ks[1], (BATCH, D_MODEL), f32),
        'w_ada': nrm(ks[2], (L, D_MODEL, N_MOD * D_MODEL), D_MODEL) * 0.5,
        'b_ada': 0.02 * jax.random.normal(ks[3], (L, N_MOD * D_MODEL), f32),
        'norm_pre_mix': gain(ks[4], (L, D_MODEL)),
        'norm_post_mix': gain(ks[5], (L, D_MODEL)),
        'norm_pre_ffn': gain(ks[6], (L, D_MODEL)),
        'norm_post_ffn': gain(ks[7], (L, D_MODEL)),
        'w_in': nrm(ks[8], (L, D_MODEL, IN_TOTAL), D_MODEL),
        'gla_w_gate_up': nrm(ks[9], (L, GLA_GATE_RANK, GLA_DK), GLA_GATE_RANK),
        'gla_b_gate': 0.1 * jax.random.normal(ks[10], (L, GLA_DK), f32),
        'gla_norm': gain(ks[11], (L, GLA_DV)),
        'ssm_conv_w': nrm(ks[12], (L, SSM_CONV, SSM_CONV_DIM), SSM_CONV),
        'ssm_conv_b': 0.02 * jax.random.normal(ks[13], (L, SSM_CONV_DIM), f32),
        'ssm_dt_bias': dt0 + jnp.log(-jnp.expm1(-dt0)),
        'ssm_a_log': jnp.log(jax.random.uniform(ks[16], (L, SSM_HEADS), f32, 1.0, 16.0)),
        'ssm_d': 1.0 + 0.1 * jax.random.normal(ks[17], (L, SSM_HEADS), f32),
        'ssm_norm': gain(ks[18], (L, SSM_INNER)),
        'w_branch_gla': nrm(ks[19], (L, GLA_DV, D_MODEL), GLA_DV),
        'w_branch_ssm': nrm(ks[20], (L, SSM_INNER, D_MODEL), SSM_INNER),
        'w_out': nrm(ks[21], (L, D_MODEL, D_MODEL), D_MODEL),
        'router_group': nrm(ks[22], (L, D_MODEL, MOE_GROUPS), D_MODEL),
        'router_expert': nrm(ks[23], (L, D_MODEL, N_EXPERTS), D_MODEL),
        'moe_w_gate': nrm(ks[24], (L, N_EXPERTS, D_MODEL, MOE_FF), D_MODEL),
        'moe_w_up': nrm(ks[25], (L, N_EXPERTS, D_MODEL, MOE_FF), D_MODEL),
        'moe_w_down': nrm(ks[26], (L, N_EXPERTS, MOE_FF, D_MODEL), MOE_FF),
    }


def reference(x, c, w_ada, b_ada, norm_pre_mix, norm_post_mix, norm_pre_ffn, norm_post_ffn,
              w_in, gla_w_gate_up, gla_b_gate, gla_norm, ssm_conv_w, ssm_conv_b, ssm_dt_bias,
              ssm_a_log, ssm_d, ssm_norm, w_branch_gla, w_branch_ssm, w_out,
              router_group, router_expert, moe_w_gate, moe_w_up, moe_w_down):
    for layer in range(DEPTH):
        mod = jax.nn.silu(c) @ w_ada[layer] + b_ada[layer]
        sh1, sc1, g1, sh2, sc2, g2 = jnp.split(mod[:, None, :], N_MOD, axis=-1)
        h = rms_norm(x, norm_pre_mix[layer]) * (1.0 + sc1) + sh1
        mix = hybrid_mixer(h, w_in[layer], gla_w_gate_up[layer], gla_b_gate[layer], gla_norm[layer],
                           ssm_conv_w[layer], ssm_conv_b[layer], ssm_dt_bias[layer], ssm_a_log[layer],
                           ssm_d[layer], ssm_norm[layer], w_branch_gla[layer], w_branch_ssm[layer],
                           w_out[layer])
        x = x + g1 * rms_norm(mix, norm_post_mix[layer])
        h = rms_norm(x, norm_pre_ffn[layer]) * (1.0 + sc2) + sh2
        ffn = hierarchical_moe(h, router_group[layer], router_expert[layer],
                               moe_w_gate[layer], moe_w_up[layer], moe_w_down[layer])
        x = x + g2 * rms_norm(ffn, norm_post_ffn[layer])
    return x
```

```python
import functools

import jax
import jax.numpy as jnp
from jax import lax
from jax.experimental import pallas as pl
from jax.experimental.pallas import tpu as pltpu

F32 = jnp.float32
BF16 = jnp.bfloat16
I32 = jnp.int32

D_MODEL = 2048
EPS = 1e-6

GLA_HEADS = 4
GLA_HEAD_K = 256
GLA_HEAD_V = 512
GLA_RANK = 16
GLA_NORMALIZER = 16.0
CHUNK = 64

SSM_GROUPS = 8
SSM_HEADS = 64
SSM_HEAD_DIM = 64
SSM_STATE = 128
SSM_CONV = 4
SSM_GROUP_W = 512
SSM_INNER = 4096

N_EXPERTS = 64
EXPERTS_PER_GROUP = 8
MOE_GROUPS = 8
MOE_FF = 1024
MOE_BLOCK = 128
ITEM_BLOCKS = 4

COL_Q, COL_K, COL_V, COL_OG, COL_Z, COL_XS, COL_B, COL_C, COL_GG, COL_GS = (
    0, 1024, 2048, 4096, 6144, 10240, 14336, 15360, 16384, 18432)
P_COLS = 20480
SMALL_COLS = 128
SMALL_DT0 = GLA_RANK

VMEM_LIMIT = 56 * 1024 * 1024


def _cparams(sem, vmem=VMEM_LIMIT):
    return pltpu.CompilerParams(dimension_semantics=sem, vmem_limit_bytes=vmem)


def _dot(a, b):
    return jnp.dot(a, b, preferred_element_type=F32)


def _dot_nt(a, b):
    return lax.dot_general(a, b, (((1,), (1,)), ((), ())), preferred_element_type=F32)


def _dot_tn(a, b):
    return lax.dot_general(a, b, (((0,), (0,)), ((), ())), preferred_element_type=F32)


def _split3(a):
    hi = a.astype(BF16)
    r1 = a - hi.astype(F32)
    mid = r1.astype(BF16)
    lo = (r1 - mid.astype(F32)).astype(BF16)
    return hi, mid, lo


def _dot_sel_r(a, sel):
    hi, mid, lo = _split3(a)
    return _dot(hi, sel) + _dot(mid, sel) + _dot(lo, sel)


def _dot_sel_l(sel, a):
    hi, mid, lo = _split3(a)
    return _dot(sel, hi) + _dot(sel, mid) + _dot(sel, lo)


def _sigmoid(x):
    return 1.0 / (1.0 + jnp.exp(-x))


def _silu(x):
    return x * _sigmoid(x)


def _softplus(x):
    return jnp.maximum(x, 0.0) + jnp.log1p(jnp.exp(-jnp.abs(x)))


def _log_sigmoid(x):
    return jnp.minimum(x, 0.0) - jnp.log1p(jnp.exp(-jnp.abs(x)))


def _rms(x):
    return x * lax.rsqrt(jnp.mean(x * x, axis=-1, keepdims=True) + EPS)


def _ada_body(c_ref, w_ref, b_ref, o_ref):
    c = c_ref[...]
    s = jnp.broadcast_to(_silu(c), (8, c.shape[1])).astype(BF16)
    o_ref[...] = _dot(s, w_ref[...].astype(BF16))[0:1] + b_ref[...]


def _ada(c, w_ada, b_ada):
    d, n = w_ada.shape
    tn = 1024
    return pl.pallas_call(
        _ada_body,
        grid=(n // tn,),
        in_specs=[pl.BlockSpec((1, d), lambda j: (0, 0)),
                  pl.BlockSpec((d, tn), lambda j: (0, j)),
                  pl.BlockSpec((1, tn), lambda j: (0, j))],
        out_specs=pl.BlockSpec((1, tn), lambda j: (0, j)),
        out_shape=jax.ShapeDtypeStruct((1, n), F32),
        compiler_params=_cparams(("parallel",)),
        name="ada",
    )(c, w_ada, b_ada.reshape(1, n))


def _inproj_body(x_ref, mod_ref, g_ref, w_ref, ws_ref, p_ref, s_ref, h_ref):
    @pl.when(pl.program_id(1) == 0)
    def _():
        h = _rms(x_ref[...]) * g_ref[...] * (1.0 + mod_ref[1:2, :]) + mod_ref[0:1, :]
        hb = h.astype(BF16)
        h_ref[...] = hb
        s_ref[...] = _dot(hb, ws_ref[...])

    p_ref[...] = _dot(h_ref[...], w_ref[...]).astype(BF16)


def _inproj(x2, mod8, gain, w_big, w_small):
    t, d = x2.shape
    n = w_big.shape[1]
    tm, tn = 1024, 1024
    return pl.pallas_call(
        _inproj_body,
        grid=(t // tm, n // tn),
        in_specs=[pl.BlockSpec((tm, d), lambda m, j: (m, 0)),
                  pl.BlockSpec((8, d), lambda m, j: (0, 0)),
                  pl.BlockSpec((1, d), lambda m, j: (0, 0)),
                  pl.BlockSpec((d, tn), lambda m, j: (0, j)),
                  pl.BlockSpec((d, SMALL_COLS), lambda m, j: (0, 0))],
        out_specs=[pl.BlockSpec((tm, tn), lambda m, j: (m, j)),
                   pl.BlockSpec((tm, SMALL_COLS), lambda m, j: (m, 0))],
        out_shape=[jax.ShapeDtypeStruct((t, n), BF16),
                   jax.ShapeDtypeStruct((t, SMALL_COLS), F32)],
        scratch_shapes=[pltpu.VMEM((tm, d), BF16)],
        compiler_params=_cparams(("parallel", "arbitrary")),
        name="inproj",
    )(x2, mod8, gain, w_big, w_small)


GLA_TB = 512


def _gla_body(q_ref, k_ref, v_ref, og_ref, sm_ref, wup_ref, bg_ref, gn_ref, o_ref, st_ref):
    @pl.when(pl.program_id(1) == 0)
    def _():
        st_ref[...] = jnp.zeros_like(st_ref)

    row = lax.broadcasted_iota(I32, (CHUNK, CHUNK), 0)
    col = lax.broadcasted_iota(I32, (CHUNK, CHUNK), 1)
    causal = row >= col
    tril = causal.astype(BF16)
    wup = wup_ref[...].astype(BF16)
    scale = GLA_HEAD_K ** -0.5

    def chunk(c, carry):
        r0 = pl.multiple_of(c * CHUNK, CHUNK)
        rows = pl.ds(r0, CHUNK)
        glr = sm_ref[rows, 0:GLA_RANK].astype(BF16)
        pre = _dot(glr, wup) + bg_ref[...]
        log_a = _log_sigmoid(pre) / GLA_NORMALIZER
        b = _dot_sel_l(tril, log_a)
        b_last = b[CHUNK - 1:CHUNK, :]
        q = q_ref[rows, :].astype(F32) * scale
        k = k_ref[rows, :].astype(F32)
        v = v_ref[rows, :]
        q_dec = (q * jnp.exp(b)).astype(BF16)
        k_inv = (k * jnp.exp(-b)).astype(BF16)
        k_end = (k * jnp.exp(b_last - b)).astype(BF16)
        scores = jnp.where(causal, _dot_nt(q_dec, k_inv), 0.0)
        st = st_ref[...]
        o = _dot(scores.astype(BF16), v) + _dot_nt(q_dec, st.astype(BF16))
        st_ref[...] = st * jnp.exp(b_last) + _dot_tn(v, k_end)
        og = og_ref[rows, :].astype(F32)
        o_ref[rows, :] = (_rms(o) * gn_ref[...] * _silu(og)).astype(BF16)
        return carry

    lax.fori_loop(0, GLA_TB // CHUNK, chunk, 0)


def _gla(p, small, wup, bg, gn):
    t = p.shape[0]
    tb = GLA_TB
    kb, vb = GLA_HEAD_K, GLA_HEAD_V
    return pl.pallas_call(
        _gla_body,
        grid=(GLA_HEADS, t // tb),
        in_specs=[pl.BlockSpec((tb, kb), lambda h, i: (i, COL_Q // kb + h)),
                  pl.BlockSpec((tb, kb), lambda h, i: (i, COL_K // kb + h)),
                  pl.BlockSpec((tb, vb), lambda h, i: (i, COL_V // vb + h)),
                  pl.BlockSpec((tb, vb), lambda h, i: (i, COL_OG // vb + h)),
                  pl.BlockSpec((tb, SMALL_COLS), lambda h, i: (i, 0)),
                  pl.BlockSpec((GLA_RANK, kb), lambda h, i: (0, h)),
                  pl.BlockSpec((1, kb), lambda h, i: (0, h)),
                  pl.BlockSpec((1, vb), lambda h, i: (0, h))],
        out_specs=pl.BlockSpec((tb, vb), lambda h, i: (i, h)),
        out_shape=jax.ShapeDtypeStruct((t, GLA_HEADS * vb), BF16),
        scratch_shapes=[pltpu.VMEM((vb, kb), F32)],
        compiler_params=_cparams(("parallel", "arbitrary")),
        name="gla",
    )(p, p, p, p, small, wup, bg, gn)


SSD_TB = 256
HALO = 8


def _ssd_body(xs_ref, b_ref, c_ref, z_ref, sm_ref, cwx_ref, cwb_ref, cwc_ref, cbx_ref, cbb_ref, cbc_ref,
              dtb_ref, alog_ref, dsk_ref, ng_ref, o_ref,
              ex_ref, eb_ref, ec_ref, xa_ref, ba_ref, ca_ref, dt_ref, adt_ref, ht_ref):
    g = pl.program_id(0)
    t = pl.program_id(1)
    tb = SSD_TB
    gw = SSM_GROUP_W

    @pl.when(t == 0)
    def _():
        ht_ref[...] = jnp.zeros_like(ht_ref)
        ex_ref[0:HALO, :] = jnp.zeros((HALO, gw), F32)
        eb_ref[0:HALO, :] = jnp.zeros((HALO, SSM_STATE), F32)
        ec_ref[0:HALO, :] = jnp.zeros((HALO, SSM_STATE), F32)

    @pl.when(t > 0)
    def _():
        ex_ref[0:HALO, :] = ex_ref[tb:tb + HALO, :]
        eb_ref[0:HALO, :] = eb_ref[tb:tb + HALO, :]
        ec_ref[0:HALO, :] = ec_ref[tb:tb + HALO, :]

    def conv_silu(u_ref, e_ref, w_ref, bias_ref):
        e_ref[HALO:HALO + tb, :] = u_ref[...].astype(F32)
        acc = bias_ref[...] + w_ref[0:1, :] * e_ref[HALO - 3:HALO - 3 + tb, :]
        for kk in range(1, SSM_CONV):
            acc = acc + w_ref[kk:kk + 1, :] * e_ref[HALO - 3 + kk:HALO - 3 + kk + tb, :]
        return _silu(acc)

    xa_ref[...] = conv_silu(xs_ref, ex_ref, cwx_ref, cbx_ref)
    ba_ref[...] = conv_silu(b_ref, eb_ref, cwb_ref, cbb_ref).astype(BF16)
    ca_ref[...] = conv_silu(c_ref, ec_ref, cwc_ref, cbc_ref).astype(BF16)

    e_row = lax.broadcasted_iota(I32, (SMALL_COLS, gw), 0)
    e_col = lax.broadcasted_iota(I32, (SMALL_COLS, gw), 1)
    expand = (e_row == SMALL_DT0 + g * (gw // SSM_HEAD_DIM) + e_col // SSM_HEAD_DIM).astype(BF16)

    dt_small = _softplus(sm_ref[...] + dtb_ref[...])
    dt_exp = _dot_sel_r(dt_small, expand)
    a_exp = _dot_sel_r(jnp.broadcast_to(-jnp.exp(alog_ref[...]), (8, SMALL_COLS)), expand)[0:1]
    d_exp = _dot_sel_r(jnp.broadcast_to(dsk_ref[...], (8, SMALL_COLS)), expand)[0:1]
    dt_ref[...] = dt_exp
    adt_ref[...] = dt_exp * a_exp

    L = CHUNK
    li = lax.broadcasted_iota(I32, (L, gw), 0)
    lj = lax.broadcasted_iota(I32, (L, gw), 1) % L
    causal_t = li >= lj
    eye_t = (li == lj).astype(F32)
    r2 = lax.broadcasted_iota(I32, (L, L), 0)
    c2 = lax.broadcasted_iota(I32, (L, L), 1)
    tril = (r2 >= c2).astype(BF16)
    ones = jnp.ones((L, L), BF16)
    bi = lax.broadcasted_iota(I32, (gw, gw), 0) // L
    bj = lax.broadcasted_iota(I32, (gw, gw), 1) // SSM_HEAD_DIM
    blockmask = bi == bj
    rep = gw // L

    def chunk(c, carry):
        r0 = pl.multiple_of(c * L, L)
        rows = pl.ds(r0, L)
        acum = _dot_sel_l(tril, adt_ref[rows, :])
        rterm = _dot_sel_l(ones, acum * eye_t)
        seg = acum - rterm
        decay = jnp.where(causal_t, jnp.exp(jnp.where(causal_t, seg, 0.0)), 0.0)
        cc = ca_ref[rows, :]
        bc = ba_ref[rows, :]
        cb_t = _dot_nt(cc, jnp.concatenate([bc] * rep, axis=0))
        m = (cb_t * decay).astype(BF16)
        xa = xa_ref[rows, :]
        xdt = xa * dt_ref[rows, :]
        xdt_b = xdt.astype(BF16)
        bd = jnp.where(blockmask, jnp.concatenate([xdt_b] * rep, axis=0), jnp.zeros((), BF16))
        y_diag = _dot(m, bd)
        ht = ht_ref[...]
        y_off = _dot(cc, ht.astype(BF16)) * jnp.exp(acum)
        a_last = acum[L - 1:L, :]
        xd = (xdt * jnp.exp(a_last - acum)).astype(BF16)
        ht_ref[...] = ht * jnp.exp(a_last) + _dot_tn(bc, xd)
        y = y_diag + y_off + d_exp * xa
        y = y * _silu(z_ref[rows, :].astype(F32))
        o_ref[rows, :] = (_rms(y) * ng_ref[...]).astype(BF16)
        return carry

    lax.fori_loop(0, tb // L, chunk, 0)


def _ssd(p, small, conv_w, conv_b, dt_bias_s, a_log_s, d_skip_s, norm_g):
    t = p.shape[0]
    tb = SSD_TB
    gw, ns = SSM_GROUP_W, SSM_STATE
    xs0, b0, c0 = 0, SSM_INNER // ns, (SSM_INNER + SSM_GROUPS * ns) // ns
    row = lambda w, off: pl.BlockSpec((1, w), lambda g, i: (0, off + g))
    return pl.pallas_call(
        _ssd_body,
        grid=(SSM_GROUPS, t // tb),
        in_specs=[pl.BlockSpec((tb, gw), lambda g, i: (i, COL_XS // gw + g)),
                  pl.BlockSpec((tb, ns), lambda g, i: (i, COL_B // ns + g)),
                  pl.BlockSpec((tb, ns), lambda g, i: (i, COL_C // ns + g)),
                  pl.BlockSpec((tb, gw), lambda g, i: (i, COL_Z // gw + g)),
                  pl.BlockSpec((tb, SMALL_COLS), lambda g, i: (i, 0)),
                  pl.BlockSpec((SSM_CONV, gw), lambda g, i: (0, xs0 + g)),
                  pl.BlockSpec((SSM_CONV, ns), lambda g, i: (0, b0 + g)),
                  pl.BlockSpec((SSM_CONV, ns), lambda g, i: (0, c0 + g)),
                  row(gw, xs0), row(ns, b0), row(ns, c0),
                  pl.BlockSpec((1, SMALL_COLS), lambda g, i: (0, 0)),
                  pl.BlockSpec((1, SMALL_COLS), lambda g, i: (0, 0)),
                  pl.BlockSpec((1, SMALL_COLS), lambda g, i: (0, 0)),
                  pl.BlockSpec((1, gw), lambda g, i: (0, g))],
        out_specs=pl.BlockSpec((tb, gw), lambda g, i: (i, g)),
        out_shape=jax.ShapeDtypeStruct((t, SSM_INNER), BF16),
        scratch_shapes=[pltpu.VMEM((tb + HALO, gw), F32),
                        pltpu.VMEM((tb + HALO, ns), F32),
                        pltpu.VMEM((tb + HALO, ns), F32),
                        pltpu.VMEM((tb, gw), F32),
                        pltpu.VMEM((tb, ns), BF16),
                        pltpu.VMEM((tb, ns), BF16),
                        pltpu.VMEM((tb, gw), F32),
                        pltpu.VMEM((tb, gw), F32),
                        pltpu.VMEM((ns, gw), F32)],
        compiler_params=_cparams(("parallel", "arbitrary")),
        name="ssd",
    )(p, p, p, p, small, conv_w, conv_w, conv_w, conv_b, conv_b, conv_b,
      dt_bias_s, a_log_s, d_skip_s, norm_g)


def _merge_body(a1_ref, a2_ref, w1_ref, w2_ref, gg_ref, gs_ref, o_ref):
    y1 = _dot(a1_ref[...], w1_ref[...])
    y2 = _dot(a2_ref[...], w2_ref[...])
    o_ref[...] = (_sigmoid(gg_ref[...].astype(F32)) * y1 + _sigmoid(gs_ref[...].astype(F32)) * y2).astype(BF16)


def _merge(o_gla, y_ssm, w1, w2, p):
    t, k1 = o_gla.shape
    k2 = y_ssm.shape[1]
    n = w1.shape[1]
    tm, tn = 512, 512
    return pl.pallas_call(
        _merge_body,
        grid=(t // tm, n // tn),
        in_specs=[pl.BlockSpec((tm, k1), lambda m, j: (m, 0)),
                  pl.BlockSpec((tm, k2), lambda m, j: (m, 0)),
                  pl.BlockSpec((k1, tn), lambda m, j: (0, j)),
                  pl.BlockSpec((k2, tn), lambda m, j: (0, j)),
                  pl.BlockSpec((tm, tn), lambda m, j: (m, COL_GG // tn + j)),
                  pl.BlockSpec((tm, tn), lambda m, j: (m, COL_GS // tn + j))],
        out_specs=pl.BlockSpec((tm, tn), lambda m, j: (m, j)),
        out_shape=jax.ShapeDtypeStruct((t, n), BF16),
        compiler_params=_cparams(("parallel", "arbitrary")),
        name="merge",
    )(o_gla, y_ssm, w1, w2, p, p)


def _outproj_body(m_ref, w_ref, x_ref, mod_ref, gpost_ref, gpre_ref, wr_ref, x1_ref, hf_ref, lg_ref):
    mix = _dot(m_ref[...], w_ref[...])
    x1 = x_ref[...] + mod_ref[2:3, :] * (_rms(mix) * gpost_ref[...])
    x1_ref[...] = x1
    h = _rms(x1) * gpre_ref[...] * (1.0 + mod_ref[4:5, :]) + mod_ref[3:4, :]
    hf_ref[...] = h
    h_hi = h.astype(BF16)
    h_lo = (h - h_hi.astype(F32)).astype(BF16)
    wr = wr_ref[...]
    w_hi = wr.astype(BF16)
    w_lo = (wr - w_hi.astype(F32)).astype(BF16)
    lg_ref[...] = _dot(h_hi, w_hi) + _dot(h_hi, w_lo) + _dot(h_lo, w_hi)


def _outproj(merged, w_out, x2, mod8, g_post, g_pre, w_router):
    t, d = x2.shape
    tm = 256
    full = lambda r, c: pl.BlockSpec((r, c), lambda m: (0, 0))
    tile = lambda c: pl.BlockSpec((tm, c), lambda m: (m, 0))
    return pl.pallas_call(
        _outproj_body,
        grid=(t // tm,),
        in_specs=[tile(d), full(d, d), tile(d), full(8, d), full(1, d), full(1, d), full(d, 128)],
        out_specs=[tile(d), tile(d), tile(128)],
        out_shape=[jax.ShapeDtypeStruct((t, d), F32),
                   jax.ShapeDtypeStruct((t, d), F32),
                   jax.ShapeDtypeStruct((t, 128), F32)],
        compiler_params=_cparams(("parallel",)),
        name="outproj",
    )(merged, w_out, x2, mod8, g_post, g_pre, w_router)


def _route_body(lg_ref, id_ref, w_ref):
    lg = lg_ref[...]
    lane = lax.broadcasted_iota(I32, lg.shape, 1)
    lane_f = lane.astype(F32)
    neg = jnp.float32(-jnp.inf)

    def first_argmax(vals, mx):
        return jnp.min(jnp.where(vals == mx, lane_f, 1e9), axis=-1, keepdims=True).astype(I32)

    gmask = lane < MOE_GROUPS
    gl = jnp.where(gmask, lg, neg)
    gmax = jnp.max(gl, axis=-1, keepdims=True)
    gsum = jnp.sum(jnp.where(gmask, jnp.exp(gl - gmax), 0.0), axis=-1, keepdims=True)
    g_w = 1.0 / gsum
    g_idx = first_argmax(gl, gmax)
    lo = MOE_GROUPS + g_idx * EXPERTS_PER_GROUP
    emask = (lane >= lo) & (lane < lo + EXPERTS_PER_GROUP)
    el = jnp.where(emask, lg, neg)
    m1 = jnp.max(el, axis=-1, keepdims=True)
    i1 = first_argmax(el, m1)
    el2 = jnp.where(lane == i1, neg, el)
    m2 = jnp.max(el2, axis=-1, keepdims=True)
    i2 = first_argmax(el2, m2)
    r = jnp.exp(m2 - m1)
    w1 = g_w / (1.0 + r)
    w2 = g_w * r / (1.0 + r)
    id_ref[...] = jnp.where(lane == 0, i1 - MOE_GROUPS, jnp.where(lane == 1, i2 - MOE_GROUPS, 0))
    w_ref[...] = jnp.where(lane == 0, w1, jnp.where(lane == 1, w2, 0.0))


def _route(logits):
    t = logits.shape[0]
    tm = 1024
    spec = pl.BlockSpec((tm, 128), lambda m: (m, 0))
    return pl.pallas_call(
        _route_body,
        grid=(t // tm,),
        in_specs=[spec],
        out_specs=[spec, spec],
        out_shape=[jax.ShapeDtypeStruct((t, 128), I32), jax.ShapeDtypeStruct((t, 128), F32)],
        compiler_params=_cparams(("parallel",)),
        name="route",
    )(logits)


GATHER_ROWS = 512


def _gather_body(idx_ref, src_ref, dst_ref, sem):
    base = pl.program_id(0) * GATHER_ROWS

    def issue(r, carry):
        tok = idx_ref[0, 0, r]
        pltpu.make_async_copy(src_ref.at[pl.ds(tok, 1)], dst_ref.at[pl.ds(base + r, 1)], sem).start()
        return carry

    lax.fori_loop(0, GATHER_ROWS, issue, 0, unroll=8)

    def drain(r, carry):
        pltpu.make_async_copy(src_ref.at[pl.ds(0, 1)], dst_ref.at[pl.ds(base + r, 1)], sem).wait()
        return carry

    lax.fori_loop(0, GATHER_ROWS, drain, 0, unroll=8)


def _gather_rows(src, idx):
    n = idx.shape[0]
    d = src.shape[1]
    nchunks = n // GATHER_ROWS
    return pl.pallas_call(
        _gather_body,
        grid=(nchunks,),
        in_specs=[pl.BlockSpec((1, 1, GATHER_ROWS), lambda i: (i, 0, 0), memory_space=pltpu.SMEM),
                  pl.BlockSpec(memory_space=pl.ANY)],
        out_specs=pl.BlockSpec(memory_space=pl.ANY),
        out_shape=jax.ShapeDtypeStruct((n, d), src.dtype),
        scratch_shapes=[pltpu.SemaphoreType.DMA(())],
        compiler_params=_cparams(("arbitrary",)),
        name="gather",
    )(idx.reshape(nchunks, 1, GATHER_ROWS), src)


MOE_FC = 512
MOE_J = MOE_FF // MOE_FC
N_ITEMS = (16384 // MOE_BLOCK + N_EXPERTS) // ITEM_BLOCKS + (N_EXPERTS * (ITEM_BLOCKS - 1)) // ITEM_BLOCKS


def _experts_body(ie_ref, io_ref, ins_ref, ifl_ref, nr_ref, wg_ref, wu_ref, wd_ref, xs_ref, ys_ref,
                  wgb_ref, wub_ref, wdb_ref, xst_ref, xb_ref, acc_ref, sem_in, sem_out):
    i = pl.program_id(0)
    j = pl.program_id(1)
    nsub = ins_ref[i]
    nfill = ifl_ref[i]
    row0 = pl.multiple_of(io_ref[i] * MOE_BLOCK, MOE_BLOCK)
    blk = MOE_BLOCK

    def x_copy(s):
        return pltpu.make_async_copy(xs_ref.at[pl.ds(row0 + s * blk, blk)], xst_ref.at[s], sem_in.at[s])

    def y_copy(s):
        return pltpu.make_async_copy(acc_ref.at[s], ys_ref.at[pl.ds(row0 + s * blk, blk)], sem_out.at[s])

    @pl.when((nsub > 0) & (j == 0))
    def _():
        for s in range(ITEM_BLOCKS):
            @pl.when(s < nsub)
            def _():
                x_copy(s).start()
        for s in range(ITEM_BLOCKS):
            @pl.when(s < nsub)
            def _():
                x_copy(s).wait()
                xb_ref[s] = xst_ref[s].astype(BF16)

    @pl.when(nsub > 0)
    def _():
        wgb_ref[...] = wg_ref[...].astype(BF16)
        wub_ref[...] = wu_ref[...].astype(BF16)
        wdb_ref[...] = wd_ref[...].astype(BF16)

    for n in range(1, ITEM_BLOCKS + 1):
        @pl.when(nsub == n)
        def _():
            x = xb_ref[0:n].reshape(n * blk, D_MODEL)
            gate = _dot(x, wgb_ref[...])
            up = _dot(x, wub_ref[...])
            hid = (_silu(gate) * up).astype(BF16)
            y = _dot(hid, wdb_ref[...]).reshape(n, blk, D_MODEL)

            @pl.when(j == 0)
            def _():
                acc_ref[0:n] = y

            @pl.when(j > 0)
            def _():
                acc_ref[0:n] = acc_ref[0:n] + y

    @pl.when((nsub > 0) & (j == MOE_J - 1))
    def _():
        for s in range(ITEM_BLOCKS):
            @pl.when(s < nsub)
            def _():
                y_copy(s).start()
        for s in range(ITEM_BLOCKS):
            @pl.when(s < nsub)
            def _():
                y_copy(s).wait()

    @pl.when((nfill > 0) & (j == 0))
    def _():
        acc_ref[...] = jnp.zeros_like(acc_ref)
        for s in range(ITEM_BLOCKS):
            @pl.when(s < nfill)
            def _():
                y_copy(s).start()
        for s in range(ITEM_BLOCKS):
            @pl.when(s < nfill)
            def _():
                y_copy(s).wait()


def _experts(x_sorted, w_gate, w_up, w_down, item_e, item_off, item_nsub, item_fill, n_real):
    n_rows, d = x_sorted.shape

    def w_in_map(i, j, ie, io, ins, ifl, nr):
        return (ie[i], 0, jnp.where(i < nr[0], j, MOE_J - 1))

    def w_dn_map(i, j, ie, io, ins, ifl, nr):
        return (ie[i], jnp.where(i < nr[0], j, MOE_J - 1), 0)

    grid_spec = pltpu.PrefetchScalarGridSpec(
        num_scalar_prefetch=5,
        grid=(N_ITEMS, MOE_J),
        in_specs=[pl.BlockSpec((None, d, MOE_FC), w_in_map),
                  pl.BlockSpec((None, d, MOE_FC), w_in_map),
                  pl.BlockSpec((None, MOE_FC, d), w_dn_map),
                  pl.BlockSpec(memory_space=pl.ANY)],
        out_specs=pl.BlockSpec(memory_space=pl.ANY),
        scratch_shapes=[pltpu.VMEM((d, MOE_FC), BF16),
                        pltpu.VMEM((d, MOE_FC), BF16),
                        pltpu.VMEM((MOE_FC, d), BF16),
                        pltpu.VMEM((ITEM_BLOCKS, MOE_BLOCK, d), F32),
                        pltpu.VMEM((ITEM_BLOCKS, MOE_BLOCK, d), BF16),
                        pltpu.VMEM((ITEM_BLOCKS, MOE_BLOCK, d), F32),
                        pltpu.SemaphoreType.DMA((ITEM_BLOCKS,)),
                        pltpu.SemaphoreType.DMA((ITEM_BLOCKS,))],
    )
    return pl.pallas_call(
        _experts_body,
        grid_spec=grid_spec,
        out_shape=jax.ShapeDtypeStruct((n_rows, d), F32),
        compiler_params=_cparams(("arbitrary", "arbitrary")),
        name="experts",
    )(item_e, item_off, item_nsub, item_fill, n_real, w_gate, w_up, w_down, x_sorted)


def _final_body(y0_ref, y1_ref, w_ref, x1_ref, mod_ref, g_ref, o_ref):
    w = w_ref[...]
    ffn = w[:, 0:1] * y0_ref[...] + w[:, 1:2] * y1_ref[...]
    o_ref[...] = x1_ref[...] + mod_ref[5:6, :] * (_rms(ffn) * g_ref[...])


def _final(yk, wts, x1, mod8, g_post):
    t, d = x1.shape
    tm = 512
    nt = t // tm
    return pl.pallas_call(
        _final_body,
        grid=(nt,),
        in_specs=[pl.BlockSpec((tm, d), lambda m: (m, 0)),
                  pl.BlockSpec((tm, d), lambda m: (m + nt, 0)),
                  pl.BlockSpec((tm, 128), lambda m: (m, 0)),
                  pl.BlockSpec((tm, d), lambda m: (m, 0)),
                  pl.BlockSpec((8, d), lambda m: (0, 0)),
                  pl.BlockSpec((1, d), lambda m: (0, 0))],
        out_specs=pl.BlockSpec((tm, d), lambda m: (m, 0)),
        out_shape=jax.ShapeDtypeStruct((t, d), F32),
        compiler_params=_cparams(("parallel",)),
        name="final",
    )(yk, yk, wts, x1, mod8, g_post)


def _routing_tables(ids, n_tok):
    n_assign = n_tok * 2
    n_blocks = n_assign // MOE_BLOCK + N_EXPERTS
    flat_e = ids.reshape(n_assign)
    onehot = (flat_e[:, None] == jnp.arange(N_EXPERTS, dtype=I32)[None, :]).astype(I32)
    csum = jnp.cumsum(onehot, axis=0)
    rank = jnp.sum(csum * onehot, axis=1) - 1
    counts = csum[-1]
    nb = (counts + MOE_BLOCK - 1) // MOE_BLOCK
    blk_start = jnp.cumsum(nb) - nb
    dest = blk_start[flat_e] * MOE_BLOCK + rank
    row_tok = jnp.zeros((n_blocks * MOE_BLOCK,), I32).at[dest].set(jnp.arange(n_assign, dtype=I32) // 2)
    pos = dest.reshape(n_tok, 2)

    n_it = (nb + ITEM_BLOCKS - 1) // ITEM_BLOCKS
    it_end = jnp.cumsum(n_it)
    it_start = it_end - n_it
    n_real = it_end[-1]
    i = jnp.arange(N_ITEMS, dtype=I32)
    e_i = jnp.minimum(jnp.searchsorted(it_end, i, side='right').astype(I32), N_EXPERTS - 1)
    k_i = i - it_start[e_i]
    valid = i < n_real
    last_e = e_i[jnp.maximum(n_real - 1, 0)]
    item_e = jnp.where(valid, e_i, last_e).astype(I32)
    fill_off = jnp.sum(nb) + ITEM_BLOCKS * (i - n_real)
    item_fill = jnp.where(valid, 0, jnp.clip(n_blocks - fill_off, 0, ITEM_BLOCKS)).astype(I32)
    item_off = jnp.where(valid, blk_start[e_i] + ITEM_BLOCKS * k_i, jnp.minimum(fill_off, n_blocks - 1)).astype(I32)
    item_nsub = jnp.where(valid, jnp.clip(nb[e_i] - ITEM_BLOCKS * k_i, 0, ITEM_BLOCKS), 0).astype(I32)
    return row_tok, pos, item_e, item_off, item_nsub, item_fill, n_real.reshape(1).astype(I32)


def _pad_lanes(v, start, total=SMALL_COLS):
    return jnp.zeros((1, total), F32).at[0, start:start + v.shape[0]].set(v)


def _layer(x2, c, w_ada, b_ada, norm_pre_mix, norm_post_mix, norm_pre_ffn, norm_post_ffn,
           w_in, gla_w_gate_up, gla_b_gate, gla_norm, ssm_conv_w, ssm_conv_b, ssm_dt_bias,
           ssm_a_log, ssm_d, ssm_norm, w_branch_gla, w_branch_ssm, w_out,
           router_group, router_expert, moe_w_gate, moe_w_up, moe_w_down):
    t, d = x2.shape
    row = lambda v: v.reshape(1, -1)

    mod = _ada(c, w_ada, b_ada)
    mod8 = jnp.concatenate([mod.reshape(6, d), jnp.zeros((2, d), F32)], axis=0)

    o_glr, o_og, o_dt, o_gg = 4096, 4112, 16400, 16464
    w_big = jnp.concatenate([w_in[:, :o_glr], w_in[:, o_og:o_dt], w_in[:, o_gg:]], axis=1).astype(BF16)
    w_small = jnp.concatenate([w_in[:, o_glr:o_og], w_in[:, o_dt:o_gg],
                               jnp.zeros((d, SMALL_COLS - GLA_RANK - SSM_HEADS), F32)], axis=1).astype(BF16)

    p, small = _inproj(x2, mod8, row(norm_pre_mix), w_big, w_small)

    o_gla = _gla(p, small, gla_w_gate_up, row(gla_b_gate), row(gla_norm))
    y_ssm = _ssd(p, small, ssm_conv_w, row(ssm_conv_b),
                 _pad_lanes(ssm_dt_bias, SMALL_DT0), _pad_lanes(ssm_a_log, SMALL_DT0),
                 _pad_lanes(ssm_d, SMALL_DT0), row(ssm_norm))
    merged = _merge(o_gla, y_ssm, w_branch_gla.astype(BF16), w_branch_ssm.astype(BF16), p)

    w_router = jnp.concatenate([router_group, router_expert,
                                jnp.zeros((d, 128 - MOE_GROUPS - N_EXPERTS), F32)], axis=1)
    x1, h2f, logits = _outproj(merged, w_out.astype(BF16), x2, mod8,
                               row(norm_post_mix), row(norm_pre_ffn), w_router)
    ids, wts = _route(logits)

    row_tok, pos, item_e, item_off, item_nsub, item_fill, n_real = _routing_tables(ids[:, :2], t)
    x_sorted = _gather_rows(h2f, row_tok)
    y_sorted = _experts(x_sorted, moe_w_gate, moe_w_up, moe_w_down,
                        item_e, item_off, item_nsub, item_fill, n_real)
    yk = _gather_rows(y_sorted, jnp.concatenate([pos[:, 0], pos[:, 1]]))
    return _final(yk, wts, x1, mod8, row(norm_post_ffn))


def kernel(x, c, w_ada, b_ada, norm_pre_mix, norm_post_mix, norm_pre_ffn, norm_post_ffn, w_in, gla_w_gate_up, gla_b_gate, gla_norm, ssm_conv_w, ssm_conv_b, ssm_dt_bias, ssm_a_log, ssm_d, ssm_norm, w_branch_gla, w_branch_ssm, w_out, router_group, router_expert, moe_w_gate, moe_w_up, moe_w_down):
    bsz, seq, d = x.shape
    assert bsz == 1 and d == D_MODEL
    x2 = x.reshape(seq, d)
    params = (w_ada, b_ada, norm_pre_mix, norm_post_mix, norm_pre_ffn, norm_post_ffn, w_in, gla_w_gate_up,
              gla_b_gate, gla_norm, ssm_conv_w, ssm_conv_b, ssm_dt_bias, ssm_a_log, ssm_d, ssm_norm,
              w_branch_gla, w_branch_ssm, w_out, router_group, router_expert, moe_w_gate, moe_w_up, moe_w_down)
    for layer in range(w_ada.shape[0]):
        x2 = _layer(x2, c, *(prm[layer] for prm in params))
    return x2.reshape(bsz, seq, d)
```

```python
import functools

import jax
import jax.numpy as jnp
from jax import lax
from jax.experimental import pallas as pl
from jax.experimental.pallas import tpu as pltpu

F32 = jnp.float32
BF16 = jnp.bfloat16
I32 = jnp.int32

D_MODEL = 2048
EPS = 1e-6
LANES = 128
ROW_TILES = D_MODEL // LANES

GLA_HEADS = 4
GLA_HEAD_K = 256
GLA_HEAD_V = 512
GLA_RANK = 16
GLA_NORMALIZER = 16.0
CHUNK = 64

SSM_GROUPS = 8
SSM_HEADS = 64
SSM_HEAD_DIM = 64
SSM_STATE = 128
SSM_CONV = 4
SSM_GROUP_W = 512
SSM_INNER = 4096

N_EXPERTS = 64
EXPERTS_PER_GROUP = 8
MOE_GROUPS = 8
MOE_FF = 1024
MOE_BLOCK = 128
ITEM_BLOCKS = 4

COL_Q, COL_K, COL_V, COL_OG, COL_Z, COL_XS, COL_B, COL_C, COL_GG, COL_GS = (
    0, 1024, 2048, 4096, 6144, 10240, 14336, 15360, 16384, 18432)
P_COLS = 20480
SMALL_COLS = 128
SMALL_DT0 = GLA_RANK

VMEM_LIMIT = 56 * 1024 * 1024


def _cparams(sem, vmem=VMEM_LIMIT):
    return pltpu.CompilerParams(dimension_semantics=sem, vmem_limit_bytes=vmem)


def _dot(a, b):
    return jnp.dot(a, b, preferred_element_type=F32)


def _dot_nt(a, b):
    return lax.dot_general(a, b, (((1,), (1,)), ((), ())), preferred_element_type=F32)


def _dot_tn(a, b):
    return lax.dot_general(a, b, (((0,), (0,)), ((), ())), preferred_element_type=F32)


def _split3(a):
    hi = a.astype(BF16)
    r1 = a - hi.astype(F32)
    mid = r1.astype(BF16)
    lo = (r1 - mid.astype(F32)).astype(BF16)
    return hi, mid, lo


def _dot_sel_r(a, sel):
    hi, mid, lo = _split3(a)
    return _dot(hi, sel) + _dot(mid, sel) + _dot(lo, sel)


def _dot_sel_l(sel, a):
    hi, mid, lo = _split3(a)
    return _dot(sel, hi) + _dot(sel, mid) + _dot(sel, lo)


def _sigmoid(x):
    return 1.0 / (1.0 + jnp.exp(-x))


def _silu(x):
    return x * _sigmoid(x)


def _softplus(x):
    return jnp.maximum(x, 0.0) + jnp.log1p(jnp.exp(-jnp.abs(x)))


def _log_sigmoid(x):
    return jnp.minimum(x, 0.0) - jnp.log1p(jnp.exp(-jnp.abs(x)))


def _rms(x):
    return x * lax.rsqrt(jnp.mean(x * x, axis=-1, keepdims=True) + EPS)


def _ada_body(c_ref, w_ref, b_ref, o_ref):
    c = c_ref[...]
    s = jnp.broadcast_to(_silu(c), (8, c.shape[1])).astype(BF16)
    o_ref[...] = _dot(s, w_ref[...].astype(BF16))[0:1] + b_ref[...]


def _ada(c, w_ada, b_ada):
    d, n = w_ada.shape
    tn = 1024
    return pl.pallas_call(
        _ada_body,
        grid=(n // tn,),
        in_specs=[pl.BlockSpec((1, d), lambda j: (0, 0)),
                  pl.BlockSpec((d, tn), lambda j: (0, j)),
                  pl.BlockSpec((1, tn), lambda j: (0, j))],
        out_specs=pl.BlockSpec((1, tn), lambda j: (0, j)),
        out_shape=jax.ShapeDtypeStruct((1, n), F32),
        compiler_params=_cparams(("parallel",)),
        name="ada",
    )(c, w_ada, b_ada.reshape(1, n))


def _inproj_body(x_ref, mod_ref, g_ref, w_ref, ws_ref, p_ref, s_ref, h_ref):
    @pl.when(pl.program_id(1) == 0)
    def _():
        h = _rms(x_ref[...]) * g_ref[...] * (1.0 + mod_ref[1:2, :]) + mod_ref[0:1, :]
        hb = h.astype(BF16)
        h_ref[...] = hb
        s_ref[...] = _dot(hb, ws_ref[...])

    p_ref[...] = _dot(h_ref[...], w_ref[...]).astype(BF16)


def _inproj(x2, mod8, gain, w_big, w_small):
    t, d = x2.shape
    n = w_big.shape[1]
    tm, tn = 1024, 1024
    return pl.pallas_call(
        _inproj_body,
        grid=(t // tm, n // tn),
        in_specs=[pl.BlockSpec((tm, d), lambda m, j: (m, 0)),
                  pl.BlockSpec((8, d), lambda m, j: (0, 0)),
                  pl.BlockSpec((1, d), lambda m, j: (0, 0)),
                  pl.BlockSpec((d, tn), lambda m, j: (0, j)),
                  pl.BlockSpec((d, SMALL_COLS), lambda m, j: (0, 0))],
        out_specs=[pl.BlockSpec((tm, tn), lambda m, j: (m, j)),
                   pl.BlockSpec((tm, SMALL_COLS), lambda m, j: (m, 0))],
        out_shape=[jax.ShapeDtypeStruct((t, n), BF16),
                   jax.ShapeDtypeStruct((t, SMALL_COLS), F32)],
        scratch_shapes=[pltpu.VMEM((tm, d), BF16)],
        compiler_params=_cparams(("parallel", "arbitrary")),
        name="inproj",
    )(x2, mod8, gain, w_big, w_small)


GLA_TB = 512


def _gla_body(q_ref, k_ref, v_ref, og_ref, sm_ref, wup_ref, bg_ref, gn_ref, o_ref, st_ref):
    @pl.when(pl.program_id(1) == 0)
    def _():
        st_ref[...] = jnp.zeros_like(st_ref)

    row = lax.broadcasted_iota(I32, (CHUNK, CHUNK), 0)
    col = lax.broadcasted_iota(I32, (CHUNK, CHUNK), 1)
    causal = row >= col
    tril = causal.astype(BF16)
    wup = wup_ref[...].astype(BF16)
    scale = GLA_HEAD_K ** -0.5

    def chunk(c, carry):
        r0 = pl.multiple_of(c * CHUNK, CHUNK)
        rows = pl.ds(r0, CHUNK)
        glr = sm_ref[rows, 0:GLA_RANK].astype(BF16)
        pre = _dot(glr, wup) + bg_ref[...]
        log_a = _log_sigmoid(pre) / GLA_NORMALIZER
        b = _dot_sel_l(tril, log_a)
        b_last = b[CHUNK - 1:CHUNK, :]
        q = q_ref[rows, :].astype(F32) * scale
        k = k_ref[rows, :].astype(F32)
        v = v_ref[rows, :]
        q_dec = (q * jnp.exp(b)).astype(BF16)
        k_inv = (k * jnp.exp(-b)).astype(BF16)
        k_end = (k * jnp.exp(b_last - b)).astype(BF16)
        scores = jnp.where(causal, _dot_nt(q_dec, k_inv), 0.0)
        st = st_ref[...]
        o = _dot(scores.astype(BF16), v) + _dot_nt(q_dec, st.astype(BF16))
        st_ref[...] = st * jnp.exp(b_last) + _dot_tn(v, k_end)
        og = og_ref[rows, :].astype(F32)
        o_ref[rows, :] = (_rms(o) * gn_ref[...] * _silu(og)).astype(BF16)
        return carry

    lax.fori_loop(0, GLA_TB // CHUNK, chunk, 0)


def _gla(p, small, wup, bg, gn):
    t = p.shape[0]
    tb = GLA_TB
    kb, vb = GLA_HEAD_K, GLA_HEAD_V
    return pl.pallas_call(
        _gla_body,
        grid=(GLA_HEADS, t // tb),
        in_specs=[pl.BlockSpec((tb, kb), lambda h, i: (i, COL_Q // kb + h)),
                  pl.BlockSpec((tb, kb), lambda h, i: (i, COL_K // kb + h)),
                  pl.BlockSpec((tb, vb), lambda h, i: (i, COL_V // vb + h)),
                  pl.BlockSpec((tb, vb), lambda h, i: (i, COL_OG // vb + h)),
                  pl.BlockSpec((tb, SMALL_COLS), lambda h, i: (i, 0)),
                  pl.BlockSpec((GLA_RANK, kb), lambda h, i: (0, h)),
                  pl.BlockSpec((1, kb), lambda h, i: (0, h)),
                  pl.BlockSpec((1, vb), lambda h, i: (0, h))],
        out_specs=pl.BlockSpec((tb, vb), lambda h, i: (i, h)),
        out_shape=jax.ShapeDtypeStruct((t, GLA_HEADS * vb), BF16),
        scratch_shapes=[pltpu.VMEM((vb, kb), F32)],
        compiler_params=_cparams(("parallel", "arbitrary")),
        name="gla",
    )(p, p, p, p, small, wup, bg, gn)


SSD_TB = 256
HALO = 8


def _ssd_body(xs_ref, b_ref, c_ref, z_ref, sm_ref, cwx_ref, cwb_ref, cwc_ref, cbx_ref, cbb_ref, cbc_ref,
              dtb_ref, alog_ref, dsk_ref, ng_ref, o_ref,
              ex_ref, eb_ref, ec_ref, xa_ref, ba_ref, ca_ref, dt_ref, adt_ref, ht_ref):
    g = pl.program_id(0)
    t = pl.program_id(1)
    tb = SSD_TB
    gw = SSM_GROUP_W

    @pl.when(t == 0)
    def _():
        ht_ref[...] = jnp.zeros_like(ht_ref)
        ex_ref[0:HALO, :] = jnp.zeros((HALO, gw), F32)
        eb_ref[0:HALO, :] = jnp.zeros((HALO, SSM_STATE), F32)
        ec_ref[0:HALO, :] = jnp.zeros((HALO, SSM_STATE), F32)

    @pl.when(t > 0)
    def _():
        ex_ref[0:HALO, :] = ex_ref[tb:tb + HALO, :]
        eb_ref[0:HALO, :] = eb_ref[tb:tb + HALO, :]
        ec_ref[0:HALO, :] = ec_ref[tb:tb + HALO, :]

    def conv_silu(u_ref, e_ref, w_ref, bias_ref):
        e_ref[HALO:HALO + tb, :] = u_ref[...].astype(F32)
        acc = bias_ref[...] + w_ref[0:1, :] * e_ref[HALO - 3:HALO - 3 + tb, :]
        for kk in range(1, SSM_CONV):
            acc = acc + w_ref[kk:kk + 1, :] * e_ref[HALO - 3 + kk:HALO - 3 + kk + tb, :]
        return _silu(acc)

    xa_ref[...] = conv_silu(xs_ref, ex_ref, cwx_ref, cbx_ref)
    ba_ref[...] = conv_silu(b_ref, eb_ref, cwb_ref, cbb_ref).astype(BF16)
    ca_ref[...] = conv_silu(c_ref, ec_ref, cwc_ref, cbc_ref).astype(BF16)

    e_row = lax.broadcasted_iota(I32, (SMALL_COLS, gw), 0)
    e_col = lax.broadcasted_iota(I32, (SMALL_COLS, gw), 1)
    expand = (e_row == SMALL_DT0 + g * (gw // SSM_HEAD_DIM) + e_col // SSM_HEAD_DIM).astype(BF16)

    dt_small = _softplus(sm_ref[...] + dtb_ref[...])
    dt_exp = _dot_sel_r(dt_small, expand)
    a_exp = _dot_sel_r(jnp.broadcast_to(-jnp.exp(alog_ref[...]), (8, SMALL_COLS)), expand)[0:1]
    d_exp = _dot_sel_r(jnp.broadcast_to(dsk_ref[...], (8, SMALL_COLS)), expand)[0:1]
    dt_ref[...] = dt_exp
    adt_ref[...] = dt_exp * a_exp

    L = CHUNK
    li = lax.broadcasted_iota(I32, (L, gw), 0)
    lj = lax.broadcasted_iota(I32, (L, gw), 1) % L
    causal_t = li >= lj
    eye_t = (li == lj).astype(F32)
    r2 = lax.broadcasted_iota(I32, (L, L), 0)
    c2 = lax.broadcasted_iota(I32, (L, L), 1)
    tril = (r2 >= c2).astype(BF16)
    ones = jnp.ones((L, L), BF16)
    bi = lax.broadcasted_iota(I32, (gw, gw), 0) // L
    bj = lax.broadcasted_iota(I32, (gw, gw), 1) // SSM_HEAD_DIM
    blockmask = bi == bj
    rep = gw // L

    def chunk(c, carry):
        r0 = pl.multiple_of(c * L, L)
        rows = pl.ds(r0, L)
        acum = _dot_sel_l(tril, adt_ref[rows, :])
        rterm = _dot_sel_l(ones, acum * eye_t)
        seg = acum - rterm
        decay = jnp.where(causal_t, jnp.exp(jnp.where(causal_t, seg, 0.0)), 0.0)
        cc = ca_ref[rows, :]
        bc = ba_ref[rows, :]
        cb_t = _dot_nt(cc, jnp.concatenate([bc] * rep, axis=0))
        m = (cb_t * decay).astype(BF16)
        xa = xa_ref[rows, :]
        xdt = xa * dt_ref[rows, :]
        xdt_b = xdt.astype(BF16)
        bd = jnp.where(blockmask, jnp.concatenate([xdt_b] * rep, axis=0), jnp.zeros((), BF16))
        y_diag = _dot(m, bd)
        ht = ht_ref[...]
        y_off = _dot(cc, ht.astype(BF16)) * jnp.exp(acum)
        a_last = acum[L - 1:L, :]
        xd = (xdt * jnp.exp(a_last - acum)).astype(BF16)
        ht_ref[...] = ht * jnp.exp(a_last) + _dot_tn(bc, xd)
        y = y_diag + y_off + d_exp * xa
        y = y * _silu(z_ref[rows, :].astype(F32))
        o_ref[rows, :] = (_rms(y) * ng_ref[...]).astype(BF16)
        return carry

    lax.fori_loop(0, tb // L, chunk, 0)


def _ssd(p, small, conv_w, conv_b, dt_bias_s, a_log_s, d_skip_s, norm_g):
    t = p.shape[0]
    tb = SSD_TB
    gw, ns = SSM_GROUP_W, SSM_STATE
    xs0, b0, c0 = 0, SSM_INNER // ns, (SSM_INNER + SSM_GROUPS * ns) // ns
    row = lambda w, off: pl.BlockSpec((1, w), lambda g, i: (0, off + g))
    return pl.pallas_call(
        _ssd_body,
        grid=(SSM_GROUPS, t // tb),
        in_specs=[pl.BlockSpec((tb, gw), lambda g, i: (i, COL_XS // gw + g)),
                  pl.BlockSpec((tb, ns), lambda g, i: (i, COL_B // ns + g)),
                  pl.BlockSpec((tb, ns), lambda g, i: (i, COL_C // ns + g)),
                  pl.BlockSpec((tb, gw), lambda g, i: (i, COL_Z // gw + g)),
                  pl.BlockSpec((tb, SMALL_COLS), lambda g, i: (i, 0)),
                  pl.BlockSpec((SSM_CONV, gw), lambda g, i: (0, xs0 + g)),
                  pl.BlockSpec((SSM_CONV, ns), lambda g, i: (0, b0 + g)),
                  pl.BlockSpec((SSM_CONV, ns), lambda g, i: (0, c0 + g)),
                  row(gw, xs0), row(ns, b0), row(ns, c0),
                  pl.BlockSpec((1, SMALL_COLS), lambda g, i: (0, 0)),
                  pl.BlockSpec((1, SMALL_COLS), lambda g, i: (0, 0)),
                  pl.BlockSpec((1, SMALL_COLS), lambda g, i: (0, 0)),
                  pl.BlockSpec((1, gw), lambda g, i: (0, g))],
        out_specs=pl.BlockSpec((tb, gw), lambda g, i: (i, g)),
        out_shape=jax.ShapeDtypeStruct((t, SSM_INNER), BF16),
        scratch_shapes=[pltpu.VMEM((tb + HALO, gw), F32),
                        pltpu.VMEM((tb + HALO, ns), F32),
                        pltpu.VMEM((tb + HALO, ns), F32),
                        pltpu.VMEM((tb, gw), F32),
                        pltpu.VMEM((tb, ns), BF16),
                        pltpu.VMEM((tb, ns), BF16),
                        pltpu.VMEM((tb, gw), F32),
                        pltpu.VMEM((tb, gw), F32),
                        pltpu.VMEM((ns, gw), F32)],
        compiler_params=_cparams(("parallel", "arbitrary")),
        name="ssd",
    )(p, p, p, p, small, conv_w, conv_w, conv_w, conv_b, conv_b, conv_b,
      dt_bias_s, a_log_s, d_skip_s, norm_g)


def _merge_body(a1_ref, a2_ref, w1_ref, w2_ref, gg_ref, gs_ref, o_ref):
    y1 = _dot(a1_ref[...], w1_ref[...])
    y2 = _dot(a2_ref[...], w2_ref[...])
    o_ref[...] = (_sigmoid(gg_ref[...].astype(F32)) * y1 + _sigmoid(gs_ref[...].astype(F32)) * y2).astype(BF16)


def _merge(o_gla, y_ssm, w1, w2, p):
    t, k1 = o_gla.shape
    k2 = y_ssm.shape[1]
    n = w1.shape[1]
    tm, tn = 512, 512
    return pl.pallas_call(
        _merge_body,
        grid=(t // tm, n // tn),
        in_specs=[pl.BlockSpec((tm, k1), lambda m, j: (m, 0)),
                  pl.BlockSpec((tm, k2), lambda m, j: (m, 0)),
                  pl.BlockSpec((k1, tn), lambda m, j: (0, j)),
                  pl.BlockSpec((k2, tn), lambda m, j: (0, j)),
                  pl.BlockSpec((tm, tn), lambda m, j: (m, COL_GG // tn + j)),
                  pl.BlockSpec((tm, tn), lambda m, j: (m, COL_GS // tn + j))],
        out_specs=pl.BlockSpec((tm, tn), lambda m, j: (m, j)),
        out_shape=jax.ShapeDtypeStruct((t, n), BF16),
        compiler_params=_cparams(("parallel", "arbitrary")),
        name="merge",
    )(o_gla, y_ssm, w1, w2, p, p)


def _outproj_body(m_ref, w_ref, x_ref, mod_ref, gpost_ref, gpre_ref, wr_ref, x1_ref, hf_ref, lg_ref):
    mix = _dot(m_ref[...], w_ref[...])
    x1 = x_ref[...] + mod_ref[2:3, :] * (_rms(mix) * gpost_ref[...])
    x1_ref[...] = x1
    h = _rms(x1) * gpre_ref[...] * (1.0 + mod_ref[4:5, :]) + mod_ref[3:4, :]
    for s in range(ROW_TILES):
        hf_ref[:, s, :] = h[:, s * LANES:(s + 1) * LANES]
    h_hi = h.astype(BF16)
    h_lo = (h - h_hi.astype(F32)).astype(BF16)
    wr = wr_ref[...]
    w_hi = wr.astype(BF16)
    w_lo = (wr - w_hi.astype(F32)).astype(BF16)
    lg_ref[...] = _dot(h_hi, w_hi) + _dot(h_hi, w_lo) + _dot(h_lo, w_hi)


def _outproj(merged, w_out, x2, mod8, g_post, g_pre, w_router):
    t, d = x2.shape
    tm = 256
    full = lambda r, c: pl.BlockSpec((r, c), lambda m: (0, 0))
    tile = lambda c: pl.BlockSpec((tm, c), lambda m: (m, 0))
    return pl.pallas_call(
        _outproj_body,
        grid=(t // tm,),
        in_specs=[tile(d), full(d, d), tile(d), full(8, d), full(1, d), full(1, d), full(d, 128)],
        out_specs=[tile(d), pl.BlockSpec((tm, ROW_TILES, LANES), lambda m: (m, 0, 0)), tile(128)],
        out_shape=[jax.ShapeDtypeStruct((t, d), F32),
                   jax.ShapeDtypeStruct((t, ROW_TILES, LANES), F32),
                   jax.ShapeDtypeStruct((t, 128), F32)],
        compiler_params=_cparams(("parallel",)),
        name="outproj",
    )(merged, w_out, x2, mod8, g_post, g_pre, w_router)


def _route_body(lg_ref, id_ref, w_ref):
    lg = lg_ref[...]
    lane = lax.broadcasted_iota(I32, lg.shape, 1)
    lane_f = lane.astype(F32)
    neg = jnp.float32(-jnp.inf)

    def first_argmax(vals, mx):
        return jnp.min(jnp.where(vals == mx, lane_f, 1e9), axis=-1, keepdims=True).astype(I32)

    gmask = lane < MOE_GROUPS
    gl = jnp.where(gmask, lg, neg)
    gmax = jnp.max(gl, axis=-1, keepdims=True)
    gsum = jnp.sum(jnp.where(gmask, jnp.exp(gl - gmax), 0.0), axis=-1, keepdims=True)
    g_w = 1.0 / gsum
    g_idx = first_argmax(gl, gmax)
    lo = MOE_GROUPS + g_idx * EXPERTS_PER_GROUP
    emask = (lane >= lo) & (lane < lo + EXPERTS_PER_GROUP)
    el = jnp.where(emask, lg, neg)
    m1 = jnp.max(el, axis=-1, keepdims=True)
    i1 = first_argmax(el, m1)
    el2 = jnp.where(lane == i1, neg, el)
    m2 = jnp.max(el2, axis=-1, keepdims=True)
    i2 = first_argmax(el2, m2)
    r = jnp.exp(m2 - m1)
    w1 = g_w / (1.0 + r)
    w2 = g_w * r / (1.0 + r)
    id_ref[...] = jnp.where(lane == 0, i1 - MOE_GROUPS, jnp.where(lane == 1, i2 - MOE_GROUPS, 0))
    w_ref[...] = jnp.where(lane == 0, w1, jnp.where(lane == 1, w2, 0.0))


def _route(logits):
    t = logits.shape[0]
    tm = 1024
    spec = pl.BlockSpec((tm, 128), lambda m: (m, 0))
    return pl.pallas_call(
        _route_body,
        grid=(t // tm,),
        in_specs=[spec],
        out_specs=[spec, spec],
        out_shape=[jax.ShapeDtypeStruct((t, 128), I32), jax.ShapeDtypeStruct((t, 128), F32)],
        compiler_params=_cparams(("parallel",)),
        name="route",
    )(logits)


GATHER_ROWS = 512


def _gather_body(idx_ref, src_ref, dst_ref, sem):
    base = pl.program_id(0) * GATHER_ROWS

    def issue(r, carry):
        tok = idx_ref[0, 0, r]
        pltpu.make_async_copy(src_ref.at[tok], dst_ref.at[base + r], sem).start()
        return carry

    lax.fori_loop(0, GATHER_ROWS, issue, 0, unroll=8)

    def drain(r, carry):
        pltpu.make_async_copy(src_ref.at[0], dst_ref.at[base + r], sem).wait()
        return carry

    lax.fori_loop(0, GATHER_ROWS, drain, 0, unroll=8)


def _gather_rows(src, idx):
    n = idx.shape[0]
    nchunks = n // GATHER_ROWS
    return pl.pallas_call(
        _gather_body,
        grid=(nchunks,),
        in_specs=[pl.BlockSpec((1, 1, GATHER_ROWS), lambda i: (i, 0, 0), memory_space=pltpu.SMEM),
                  pl.BlockSpec(memory_space=pl.ANY)],
        out_specs=pl.BlockSpec(memory_space=pl.ANY),
        out_shape=jax.ShapeDtypeStruct((n,) + src.shape[1:], src.dtype),
        scratch_shapes=[pltpu.SemaphoreType.DMA(())],
        compiler_params=_cparams(("arbitrary",)),
        name="gather",
    )(idx.reshape(nchunks, 1, GATHER_ROWS), src)


MOE_FC = 512
MOE_J = MOE_FF // MOE_FC
N_ITEMS = (16384 // MOE_BLOCK + N_EXPERTS) // ITEM_BLOCKS + (N_EXPERTS * (ITEM_BLOCKS - 1)) // ITEM_BLOCKS


def _experts_body(ie_ref, io_ref, ins_ref, ifl_ref, nr_ref, wg_ref, wu_ref, wd_ref, xs_ref, ys_ref,
                  wgb_ref, wub_ref, wdb_ref, xst_ref, xb_ref, acc_ref, sem_in, sem_out):
    i = pl.program_id(0)
    j = pl.program_id(1)
    nsub = ins_ref[i]
    nfill = ifl_ref[i]
    row0 = pl.multiple_of(io_ref[i] * MOE_BLOCK, MOE_BLOCK)
    blk = MOE_BLOCK

    def x_copy(s):
        return pltpu.make_async_copy(xs_ref.at[pl.ds(row0 + s * blk, blk)], xst_ref.at[s], sem_in.at[s])

    def y_copy(s):
        return pltpu.make_async_copy(xst_ref.at[s], ys_ref.at[pl.ds(row0 + s * blk, blk)], sem_out.at[s])

    @pl.when((nsub > 0) & (j == 0))
    def _():
        for s in range(ITEM_BLOCKS):
            @pl.when(s < nsub)
            def _():
                x_copy(s).start()
        for s in range(ITEM_BLOCKS):
            @pl.when(s < nsub)
            def _():
                x_copy(s).wait()
                for c in range(ROW_TILES):
                    xb_ref[s, :, c * LANES:(c + 1) * LANES] = xst_ref[s, :, c, :].astype(BF16)

    @pl.when(nsub > 0)
    def _():
        wgb_ref[...] = wg_ref[...].astype(BF16)
        wub_ref[...] = wu_ref[...].astype(BF16)
        wdb_ref[...] = wd_ref[...].astype(BF16)

    for n in range(1, ITEM_BLOCKS + 1):
        @pl.when(nsub == n)
        def _():
            x = xb_ref[0:n].reshape(n * blk, D_MODEL)
            gate = _dot(x, wgb_ref[...])
            up = _dot(x, wub_ref[...])
            hid = (_silu(gate) * up).astype(BF16)
            y = _dot(hid, wdb_ref[...]).reshape(n, blk, D_MODEL)

            @pl.when(j == 0)
            def _():
                acc_ref[0:n] = y

            @pl.when(j > 0)
            def _():
                acc_ref[0:n] = acc_ref[0:n] + y

    @pl.when((nsub > 0) & (j == MOE_J - 1))
    def _():
        for s in range(ITEM_BLOCKS):
            @pl.when(s < nsub)
            def _():
                for c in range(ROW_TILES):
                    xst_ref[s, :, c, :] = acc_ref[s, :, c * LANES:(c + 1) * LANES]
                y_copy(s).start()
        for s in range(ITEM_BLOCKS):
            @pl.when(s < nsub)
            def _():
                y_copy(s).wait()

    @pl.when((nfill > 0) & (j == 0))
    def _():
        xst_ref[...] = jnp.zeros_like(xst_ref)
        for s in range(ITEM_BLOCKS):
            @pl.when(s < nfill)
            def _():
                y_copy(s).start()
        for s in range(ITEM_BLOCKS):
            @pl.when(s < nfill)
            def _():
                y_copy(s).wait()


def _experts(x_sorted, w_gate, w_up, w_down, item_e, item_off, item_nsub, item_fill, n_real):
    n_rows = x_sorted.shape[0]
    d = D_MODEL

    def w_in_map(i, j, ie, io, ins, ifl, nr):
        return (ie[i], 0, jnp.where(i < nr[0], j, MOE_J - 1))

    def w_dn_map(i, j, ie, io, ins, ifl, nr):
        return (ie[i], jnp.where(i < nr[0], j, MOE_J - 1), 0)

    grid_spec = pltpu.PrefetchScalarGridSpec(
        num_scalar_prefetch=5,
        grid=(N_ITEMS, MOE_J),
        in_specs=[pl.BlockSpec((None, d, MOE_FC), w_in_map),
                  pl.BlockSpec((None, d, MOE_FC), w_in_map),
                  pl.BlockSpec((None, MOE_FC, d), w_dn_map),
                  pl.BlockSpec(memory_space=pl.ANY)],
        out_specs=pl.BlockSpec(memory_space=pl.ANY),
        scratch_shapes=[pltpu.VMEM((d, MOE_FC), BF16),
                        pltpu.VMEM((d, MOE_FC), BF16),
                        pltpu.VMEM((MOE_FC, d), BF16),
                        pltpu.VMEM((ITEM_BLOCKS, MOE_BLOCK, ROW_TILES, LANES), F32),
                        pltpu.VMEM((ITEM_BLOCKS, MOE_BLOCK, d), BF16),
                        pltpu.VMEM((ITEM_BLOCKS, MOE_BLOCK, d), F32),
                        pltpu.SemaphoreType.DMA((ITEM_BLOCKS,)),
                        pltpu.SemaphoreType.DMA((ITEM_BLOCKS,))],
    )
    return pl.pallas_call(
        _experts_body,
        grid_spec=grid_spec,
        out_shape=jax.ShapeDtypeStruct((n_rows, ROW_TILES, LANES), F32),
        compiler_params=_cparams(("arbitrary", "arbitrary")),
        name="experts",
    )(item_e, item_off, item_nsub, item_fill, n_real, w_gate, w_up, w_down, x_sorted)


def _final_body(y0_ref, y1_ref, w_ref, x1_ref, mod_ref, g_ref, o_ref):
    w = w_ref[...]
    w0, w1 = w[:, 0:1], w[:, 1:2]
    ffn = jnp.concatenate([w0 * y0_ref[:, c, :] + w1 * y1_ref[:, c, :] for c in range(ROW_TILES)], axis=1)
    o_ref[...] = x1_ref[...] + mod_ref[5:6, :] * (_rms(ffn) * g_ref[...])


def _final(yk, wts, x1, mod8, g_post):
    t, d = x1.shape
    tm = 512
    nt = t // tm
    return pl.pallas_call(
        _final_body,
        grid=(nt,),
        in_specs=[pl.BlockSpec((tm, ROW_TILES, LANES), lambda m: (m, 0, 0)),
                  pl.BlockSpec((tm, ROW_TILES, LANES), lambda m: (m + nt, 0, 0)),
                  pl.BlockSpec((tm, 128), lambda m: (m, 0)),
                  pl.BlockSpec((tm, d), lambda m: (m, 0)),
                  pl.BlockSpec((8, d), lambda m: (0, 0)),
                  pl.BlockSpec((1, d), lambda m: (0, 0))],
        out_specs=pl.BlockSpec((tm, d), lambda m: (m, 0)),
        out_shape=jax.ShapeDtypeStruct((t, d), F32),
        compiler_params=_cparams(("parallel",)),
        name="final",
    )(yk, yk, wts, x1, mod8, g_post)


def _routing_tables(ids, n_tok):
    n_assign = n_tok * 2
    n_blocks = n_assign // MOE_BLOCK + N_EXPERTS
    flat_e = ids.reshape(n_assign)
    onehot = (flat_e[:, None] == jnp.arange(N_EXPERTS, dtype=I32)[None, :]).astype(I32)
    csum = jnp.cumsum(onehot, axis=0)
    rank = jnp.sum(csum * onehot, axis=1) - 1
    counts = csum[-1]
    nb = (counts + MOE_BLOCK - 1) // MOE_BLOCK
    blk_start = jnp.cumsum(nb) - nb
    dest = blk_start[flat_e] * MOE_BLOCK + rank
    row_tok = jnp.zeros((n_blocks * MOE_BLOCK,), I32).at[dest].set(jnp.arange(n_assign, dtype=I32) // 2)
    pos = dest.reshape(n_tok, 2)

    n_it = (nb + ITEM_BLOCKS - 1) // ITEM_BLOCKS
    it_end = jnp.cumsum(n_it)
    it_start = it_end - n_it
    n_real = it_end[-1]
    i = jnp.arange(N_ITEMS, dtype=I32)
    e_i = jnp.minimum(jnp.searchsorted(it_end, i, side='right').astype(I32), N_EXPERTS - 1)
    k_i = i - it_start[e_i]
    valid = i < n_real
    last_e = e_i[jnp.maximum(n_real - 1, 0)]
    item_e = jnp.where(valid, e_i, last_e).astype(I32)
    fill_off = jnp.sum(nb) + ITEM_BLOCKS * (i - n_real)
    item_fill = jnp.where(valid, 0, jnp.clip(n_blocks - fill_off, 0, ITEM_BLOCKS)).astype(I32)
    item_off = jnp.where(valid, blk_start[e_i] + ITEM_BLOCKS * k_i, jnp.minimum(fill_off, n_blocks - 1)).astype(I32)
    item_nsub = jnp.where(valid, jnp.clip(nb[e_i] - ITEM_BLOCKS * k_i, 0, ITEM_BLOCKS), 0).astype(I32)
    return row_tok, pos, item_e, item_off, item_nsub, item_fill, n_real.reshape(1).astype(I32)


def _pad_lanes(v, start, total=SMALL_COLS):
    return jnp.zeros((1, total), F32).at[0, start:start + v.shape[0]].set(v)


def _layer(x2, c, w_ada, b_ada, norm_pre_mix, norm_post_mix, norm_pre_ffn, norm_post_ffn,
           w_in, gla_w_gate_up, gla_b_gate, gla_norm, ssm_conv_w, ssm_conv_b, ssm_dt_bias,
           ssm_a_log, ssm_d, ssm_norm, w_branch_gla, w_branch_ssm, w_out,
           router_group, router_expert, moe_w_gate, moe_w_up, moe_w_down):
    t, d = x2.shape
    row = lambda v: v.reshape(1, -1)

    mod = _ada(c, w_ada, b_ada)
    mod8 = jnp.concatenate([mod.reshape(6, d), jnp.zeros((2, d), F32)], axis=0)

    o_glr, o_og, o_dt, o_gg = 4096, 4112, 16400, 16464
    w_big = jnp.concatenate([w_in[:, :o_glr], w_in[:, o_og:o_dt], w_in[:, o_gg:]], axis=1).astype(BF16)
    w_small = jnp.concatenate([w_in[:, o_glr:o_og], w_in[:, o_dt:o_gg],
                               jnp.zeros((d, SMALL_COLS - GLA_RANK - SSM_HEADS), F32)], axis=1).astype(BF16)

    p, small = _inproj(x2, mod8, row(norm_pre_mix), w_big, w_small)

    o_gla = _gla(p, small, gla_w_gate_up, row(gla_b_gate), row(gla_norm))
    y_ssm = _ssd(p, small, ssm_conv_w, row(ssm_conv_b),
                 _pad_lanes(ssm_dt_bias, SMALL_DT0), _pad_lanes(ssm_a_log, SMALL_DT0),
                 _pad_lanes(ssm_d, SMALL_DT0), row(ssm_norm))
    merged = _merge(o_gla, y_ssm, w_branch_gla.astype(BF16), w_branch_ssm.astype(BF16), p)

    w_router = jnp.concatenate([router_group, router_expert,
                                jnp.zeros((d, 128 - MOE_GROUPS - N_EXPERTS), F32)], axis=1)
    x1, h2f, logits = _outproj(merged, w_out.astype(BF16), x2, mod8,
                               row(norm_post_mix), row(norm_pre_ffn), w_router)
    ids, wts = _route(logits)

    row_tok, pos, item_e, item_off, item_nsub, item_fill, n_real = _routing_tables(ids[:, :2], t)
    x_sorted = _gather_rows(h2f, row_tok)
    y_sorted = _experts(x_sorted, moe_w_gate, moe_w_up, moe_w_down,
                        item_e, item_off, item_nsub, item_fill, n_real)
    yk = _gather_rows(y_sorted, jnp.concatenate([pos[:, 0], pos[:, 1]]))
    return _final(yk, wts, x1, mod8, row(norm_post_ffn))


def kernel(x, c, w_ada, b_ada, norm_pre_mix, norm_post_mix, norm_pre_ffn, norm_post_ffn, w_in, gla_w_gate_up, gla_b_gate, gla_norm, ssm_conv_w, ssm_conv_b, ssm_dt_bias, ssm_a_log, ssm_d, ssm_norm, w_branch_gla, w_branch_ssm, w_out, router_group, router_expert, moe_w_gate, moe_w_up, moe_w_down):
    bsz, seq, d = x.shape
    assert bsz == 1 and d == D_MODEL
    x2 = x.reshape(seq, d)
    params = (w_ada, b_ada, norm_pre_mix, norm_post_mix, norm_pre_ffn, norm_post_ffn, w_in, gla_w_gate_up,
              gla_b_gate, gla_norm, ssm_conv_w, ssm_conv_b, ssm_dt_bias, ssm_a_log, ssm_d, ssm_norm,
              w_branch_gla, w_branch_ssm, w_out, router_group, router_expert, moe_w_gate, moe_w_up, moe_w_down)
    for layer in range(w_ada.shape[0]):
        x2 = _layer(x2, c, *(prm[layer] for prm in params))
    return x2.reshape(bsz, seq, d)
```

```python
import functools

import jax
import jax.numpy as jnp
from jax import lax
from jax.experimental import pallas as pl
from jax.experimental.pallas import tpu as pltpu

F32 = jnp.float32
BF16 = jnp.bfloat16
I32 = jnp.int32

D_MODEL = 2048
EPS = 1e-6
LANES = 128
ROW_TILES = D_MODEL // LANES

GLA_HEADS = 4
GLA_HEAD_K = 256
GLA_HEAD_V = 512
GLA_RANK = 16
GLA_NORMALIZER = 16.0
CHUNK = 64

SSM_GROUPS = 8
SSM_HEADS = 64
SSM_HEAD_DIM = 64
SSM_STATE = 128
SSM_CONV = 4
SSM_GROUP_W = 512
SSM_INNER = 4096

N_EXPERTS = 64
EXPERTS_PER_GROUP = 8
MOE_GROUPS = 8
MOE_FF = 1024
MOE_BLOCK = 128
ITEM_BLOCKS = 4

COL_Q, COL_K, COL_V, COL_OG, COL_Z, COL_XS, COL_B, COL_C, COL_GG, COL_GS = (
    0, 1024, 2048, 4096, 6144, 10240, 14336, 15360, 16384, 18432)
P_COLS = 20480
SMALL_COLS = 128
SMALL_DT0 = GLA_RANK

VMEM_LIMIT = 56 * 1024 * 1024


def _cparams(sem, vmem=VMEM_LIMIT):
    return pltpu.CompilerParams(dimension_semantics=sem, vmem_limit_bytes=vmem)


def _dot(a, b):
    return jnp.dot(a, b, preferred_element_type=F32)


def _dot_nt(a, b):
    return lax.dot_general(a, b, (((1,), (1,)), ((), ())), preferred_element_type=F32)


def _dot_tn(a, b):
    return lax.dot_general(a, b, (((0,), (0,)), ((), ())), preferred_element_type=F32)


def _split3(a):
    hi = a.astype(BF16)
    r1 = a - hi.astype(F32)
    mid = r1.astype(BF16)
    lo = (r1 - mid.astype(F32)).astype(BF16)
    return hi, mid, lo


def _dot_sel_r(a, sel):
    hi, mid, lo = _split3(a)
    return _dot(hi, sel) + _dot(mid, sel) + _dot(lo, sel)


def _dot_sel_l(sel, a):
    hi, mid, lo = _split3(a)
    return _dot(sel, hi) + _dot(sel, mid) + _dot(sel, lo)


def _sigmoid(x):
    return 1.0 / (1.0 + jnp.exp(-x))


def _silu(x):
    return x * _sigmoid(x)


def _softplus(x):
    return jnp.maximum(x, 0.0) + jnp.log1p(jnp.exp(-jnp.abs(x)))


def _log_sigmoid(x):
    return jnp.minimum(x, 0.0) - jnp.log1p(jnp.exp(-jnp.abs(x)))


def _rms(x):
    return x * lax.rsqrt(jnp.mean(x * x, axis=-1, keepdims=True) + EPS)


def _ada_body(c_ref, w_ref, b_ref, o_ref):
    c = c_ref[...]
    s = jnp.broadcast_to(_silu(c), (8, c.shape[1])).astype(BF16)
    o_ref[...] = _dot(s, w_ref[...].astype(BF16))[0:1] + b_ref[...]


def _ada(c, w_ada, b_ada):
    d, n = w_ada.shape
    tn = 1024
    return pl.pallas_call(
        _ada_body,
        grid=(n // tn,),
        in_specs=[pl.BlockSpec((1, d), lambda j: (0, 0)),
                  pl.BlockSpec((d, tn), lambda j: (0, j)),
                  pl.BlockSpec((1, tn), lambda j: (0, j))],
        out_specs=pl.BlockSpec((1, tn), lambda j: (0, j)),
        out_shape=jax.ShapeDtypeStruct((1, n), F32),
        compiler_params=_cparams(("parallel",)),
        name="ada",
    )(c, w_ada, b_ada.reshape(1, n))


def _inproj_body(x_ref, mod_ref, g_ref, w_ref, ws_ref, p_ref, s_ref, h_ref):
    @pl.when(pl.program_id(1) == 0)
    def _():
        h = _rms(x_ref[...]) * g_ref[...] * (1.0 + mod_ref[1:2, :]) + mod_ref[0:1, :]
        hb = h.astype(BF16)
        h_ref[...] = hb
        s_ref[...] = _dot(hb, ws_ref[...])

    p_ref[...] = _dot(h_ref[...], w_ref[...]).astype(BF16)


def _inproj(x2, mod8, gain, w_big, w_small):
    t, d = x2.shape
    n = w_big.shape[1]
    tm, tn = 1024, 1024
    return pl.pallas_call(
        _inproj_body,
        grid=(t // tm, n // tn),
        in_specs=[pl.BlockSpec((tm, d), lambda m, j: (m, 0)),
                  pl.BlockSpec((8, d), lambda m, j: (0, 0)),
                  pl.BlockSpec((1, d), lambda m, j: (0, 0)),
                  pl.BlockSpec((d, tn), lambda m, j: (0, j)),
                  pl.BlockSpec((d, SMALL_COLS), lambda m, j: (0, 0))],
        out_specs=[pl.BlockSpec((tm, tn), lambda m, j: (m, j)),
                   pl.BlockSpec((tm, SMALL_COLS), lambda m, j: (m, 0))],
        out_shape=[jax.ShapeDtypeStruct((t, n), BF16),
                   jax.ShapeDtypeStruct((t, SMALL_COLS), F32)],
        scratch_shapes=[pltpu.VMEM((tm, d), BF16)],
        compiler_params=_cparams(("parallel", "arbitrary")),
        name="inproj",
    )(x2, mod8, gain, w_big, w_small)


GLA_TB = 512


def _gla_body(q_ref, k_ref, v_ref, og_ref, sm_ref, wup_ref, bg_ref, gn_ref, o_ref, st_ref):
    @pl.when(pl.program_id(1) == 0)
    def _():
        st_ref[...] = jnp.zeros_like(st_ref)

    row = lax.broadcasted_iota(I32, (CHUNK, CHUNK), 0)
    col = lax.broadcasted_iota(I32, (CHUNK, CHUNK), 1)
    causal = row >= col
    tril = causal.astype(BF16)
    wup = wup_ref[...].astype(BF16)
    scale = GLA_HEAD_K ** -0.5

    def chunk(c, carry):
        r0 = pl.multiple_of(c * CHUNK, CHUNK)
        rows = pl.ds(r0, CHUNK)
        glr = sm_ref[rows, 0:GLA_RANK].astype(BF16)
        pre = _dot(glr, wup) + bg_ref[...]
        log_a = _log_sigmoid(pre) / GLA_NORMALIZER
        b = _dot_sel_l(tril, log_a)
        b_last = b[CHUNK - 1:CHUNK, :]
        q = q_ref[rows, :].astype(F32) * scale
        k = k_ref[rows, :].astype(F32)
        v = v_ref[rows, :]
        q_dec = (q * jnp.exp(b)).astype(BF16)
        k_inv = (k * jnp.exp(-b)).astype(BF16)
        k_end = (k * jnp.exp(b_last - b)).astype(BF16)
        scores = jnp.where(causal, _dot_nt(q_dec, k_inv), 0.0)
        st = st_ref[...]
        o = _dot(scores.astype(BF16), v) + _dot_nt(q_dec, st.astype(BF16))
        st_ref[...] = st * jnp.exp(b_last) + _dot_tn(v, k_end)
        og = og_ref[rows, :].astype(F32)
        o_ref[rows, :] = (_rms(o) * gn_ref[...] * _silu(og)).astype(BF16)
        return carry

    lax.fori_loop(0, GLA_TB // CHUNK, chunk, 0, unroll=4)


def _gla(p, small, wup, bg, gn):
    t = p.shape[0]
    tb = GLA_TB
    kb, vb = GLA_HEAD_K, GLA_HEAD_V
    return pl.pallas_call(
        _gla_body,
        grid=(GLA_HEADS, t // tb),
        in_specs=[pl.BlockSpec((tb, kb), lambda h, i: (i, COL_Q // kb + h)),
                  pl.BlockSpec((tb, kb), lambda h, i: (i, COL_K // kb + h)),
                  pl.BlockSpec((tb, vb), lambda h, i: (i, COL_V // vb + h)),
                  pl.BlockSpec((tb, vb), lambda h, i: (i, COL_OG // vb + h)),
                  pl.BlockSpec((tb, SMALL_COLS), lambda h, i: (i, 0)),
                  pl.BlockSpec((GLA_RANK, kb), lambda h, i: (0, h)),
                  pl.BlockSpec((1, kb), lambda h, i: (0, h)),
                  pl.BlockSpec((1, vb), lambda h, i: (0, h))],
        out_specs=pl.BlockSpec((tb, vb), lambda h, i: (i, h)),
        out_shape=jax.ShapeDtypeStruct((t, GLA_HEADS * vb), BF16),
        scratch_shapes=[pltpu.VMEM((vb, kb), F32)],
        compiler_params=_cparams(("parallel", "arbitrary")),
        name="gla",
    )(p, p, p, p, small, wup, bg, gn)


SSD_TB = 256
HALO = 8


def _ssd_body(xs_ref, b_ref, c_ref, z_ref, sm_ref, cwx_ref, cwb_ref, cwc_ref, cbx_ref, cbb_ref, cbc_ref,
              dtb_ref, alog_ref, dsk_ref, ng_ref, o_ref,
              ex_ref, eb_ref, ec_ref, xa_ref, ba_ref, ca_ref, dt_ref, adt_ref, ht_ref):
    g = pl.program_id(0)
    t = pl.program_id(1)
    tb = SSD_TB
    gw = SSM_GROUP_W

    @pl.when(t == 0)
    def _():
        ht_ref[...] = jnp.zeros_like(ht_ref)
        ex_ref[0:HALO, :] = jnp.zeros((HALO, gw), F32)
        eb_ref[0:HALO, :] = jnp.zeros((HALO, SSM_STATE), F32)
        ec_ref[0:HALO, :] = jnp.zeros((HALO, SSM_STATE), F32)

    @pl.when(t > 0)
    def _():
        ex_ref[0:HALO, :] = ex_ref[tb:tb + HALO, :]
        eb_ref[0:HALO, :] = eb_ref[tb:tb + HALO, :]
        ec_ref[0:HALO, :] = ec_ref[tb:tb + HALO, :]

    def conv_silu(u_ref, e_ref, w_ref, bias_ref):
        e_ref[HALO:HALO + tb, :] = u_ref[...].astype(F32)
        acc = bias_ref[...] + w_ref[0:1, :] * e_ref[HALO - 3:HALO - 3 + tb, :]
        for kk in range(1, SSM_CONV):
            acc = acc + w_ref[kk:kk + 1, :] * e_ref[HALO - 3 + kk:HALO - 3 + kk + tb, :]
        return _silu(acc)

    xa_ref[...] = conv_silu(xs_ref, ex_ref, cwx_ref, cbx_ref)
    ba_ref[...] = conv_silu(b_ref, eb_ref, cwb_ref, cbb_ref).astype(BF16)
    ca_ref[...] = conv_silu(c_ref, ec_ref, cwc_ref, cbc_ref).astype(BF16)

    e_row = lax.broadcasted_iota(I32, (SMALL_COLS, gw), 0)
    e_col = lax.broadcasted_iota(I32, (SMALL_COLS, gw), 1)
    expand = (e_row == SMALL_DT0 + g * (gw // SSM_HEAD_DIM) + e_col // SSM_HEAD_DIM).astype(BF16)

    dt_small = _softplus(sm_ref[...] + dtb_ref[...])
    dt_exp = _dot_sel_r(dt_small, expand)
    a_exp = _dot_sel_r(jnp.broadcast_to(-jnp.exp(alog_ref[...]), (8, SMALL_COLS)), expand)[0:1]
    d_exp = _dot_sel_r(jnp.broadcast_to(dsk_ref[...], (8, SMALL_COLS)), expand)[0:1]
    dt_ref[...] = dt_exp
    adt_ref[...] = dt_exp * a_exp

    L = CHUNK
    li = lax.broadcasted_iota(I32, (L, gw), 0)
    lj = lax.broadcasted_iota(I32, (L, gw), 1) % L
    causal_t = li >= lj
    eye_t = (li == lj).astype(F32)
    r2 = lax.broadcasted_iota(I32, (L, L), 0)
    c2 = lax.broadcasted_iota(I32, (L, L), 1)
    tril = (r2 >= c2).astype(BF16)
    ones = jnp.ones((L, L), BF16)
    bi = lax.broadcasted_iota(I32, (gw, gw), 0) // L
    bj = lax.broadcasted_iota(I32, (gw, gw), 1) // SSM_HEAD_DIM
    blockmask = bi == bj
    rep = gw // L

    def chunk(c, carry):
        r0 = pl.multiple_of(c * L, L)
        rows = pl.ds(r0, L)
        acum = _dot_sel_l(tril, adt_ref[rows, :])
        rterm = _dot_sel_l(ones, acum * eye_t)
        seg = acum - rterm
        decay = jnp.where(causal_t, jnp.exp(jnp.where(causal_t, seg, 0.0)), 0.0)
        cc = ca_ref[rows, :]
        bc = ba_ref[rows, :]
        cb_t = _dot_nt(cc, jnp.concatenate([bc] * rep, axis=0))
        m = (cb_t * decay).astype(BF16)
        xa = xa_ref[rows, :]
        xdt = xa * dt_ref[rows, :]
        xdt_b = xdt.astype(BF16)
        bd = jnp.where(blockmask, jnp.concatenate([xdt_b] * rep, axis=0), jnp.zeros((), BF16))
        y_diag = _dot(m, bd)
        ht = ht_ref[...]
        y_off = _dot(cc, ht.astype(BF16)) * jnp.exp(acum)
        a_last = acum[L - 1:L, :]
        xd = (xdt * jnp.exp(a_last - acum)).astype(BF16)
        ht_ref[...] = ht * jnp.exp(a_last) + _dot_tn(bc, xd)
        y = y_diag + y_off + d_exp * xa
        y = y * _silu(z_ref[rows, :].astype(F32))
        o_ref[rows, :] = (_rms(y) * ng_ref[...]).astype(BF16)
        return carry

    lax.fori_loop(0, tb // L, chunk, 0, unroll=True)


def _ssd(p, small, conv_w, conv_b, dt_bias_s, a_log_s, d_skip_s, norm_g):
    t = p.shape[0]
    tb = SSD_TB
    gw, ns = SSM_GROUP_W, SSM_STATE
    xs0, b0, c0 = 0, SSM_INNER // ns, (SSM_INNER + SSM_GROUPS * ns) // ns
    row = lambda w, off: pl.BlockSpec((1, w), lambda g, i: (0, off + g))
    return pl.pallas_call(
        _ssd_body,
        grid=(SSM_GROUPS, t // tb),
        in_specs=[pl.BlockSpec((tb, gw), lambda g, i: (i, COL_XS // gw + g)),
                  pl.BlockSpec((tb, ns), lambda g, i: (i, COL_B // ns + g)),
                  pl.BlockSpec((tb, ns), lambda g, i: (i, COL_C // ns + g)),
                  pl.BlockSpec((tb, gw), lambda g, i: (i, COL_Z // gw + g)),
                  pl.BlockSpec((tb, SMALL_COLS), lambda g, i: (i, 0)),
                  pl.BlockSpec((SSM_CONV, gw), lambda g, i: (0, xs0 + g)),
                  pl.BlockSpec((SSM_CONV, ns), lambda g, i: (0, b0 + g)),
                  pl.BlockSpec((SSM_CONV, ns), lambda g, i: (0, c0 + g)),
                  row(gw, xs0), row(ns, b0), row(ns, c0),
                  pl.BlockSpec((1, SMALL_COLS), lambda g, i: (0, 0)),
                  pl.BlockSpec((1, SMALL_COLS), lambda g, i: (0, 0)),
                  pl.BlockSpec((1, SMALL_COLS), lambda g, i: (0, 0)),
                  pl.BlockSpec((1, gw), lambda g, i: (0, g))],
        out_specs=pl.BlockSpec((tb, gw), lambda g, i: (i, g)),
        out_shape=jax.ShapeDtypeStruct((t, SSM_INNER), BF16),
        scratch_shapes=[pltpu.VMEM((tb + HALO, gw), F32),
                        pltpu.VMEM((tb + HALO, ns), F32),
                        pltpu.VMEM((tb + HALO, ns), F32),
                        pltpu.VMEM((tb, gw), F32),
                        pltpu.VMEM((tb, ns), BF16),
                        pltpu.VMEM((tb, ns), BF16),
                        pltpu.VMEM((tb, gw), F32),
                        pltpu.VMEM((tb, gw), F32),
                        pltpu.VMEM((ns, gw), F32)],
        compiler_params=_cparams(("parallel", "arbitrary")),
        name="ssd",
    )(p, p, p, p, small, conv_w, conv_w, conv_w, conv_b, conv_b, conv_b,
      dt_bias_s, a_log_s, d_skip_s, norm_g)


def _merge_body(a1_ref, a2_ref, w1_ref, w2_ref, gg_ref, gs_ref, o_ref):
    y1 = _dot(a1_ref[...], w1_ref[...])
    y2 = _dot(a2_ref[...], w2_ref[...])
    o_ref[...] = (_sigmoid(gg_ref[...].astype(F32)) * y1 + _sigmoid(gs_ref[...].astype(F32)) * y2).astype(BF16)


def _merge(o_gla, y_ssm, w1, w2, p):
    t, k1 = o_gla.shape
    k2 = y_ssm.shape[1]
    n = w1.shape[1]
    tm, tn = 512, 512
    return pl.pallas_call(
        _merge_body,
        grid=(t // tm, n // tn),
        in_specs=[pl.BlockSpec((tm, k1), lambda m, j: (m, 0)),
                  pl.BlockSpec((tm, k2), lambda m, j: (m, 0)),
                  pl.BlockSpec((k1, tn), lambda m, j: (0, j)),
                  pl.BlockSpec((k2, tn), lambda m, j: (0, j)),
                  pl.BlockSpec((tm, tn), lambda m, j: (m, COL_GG // tn + j)),
                  pl.BlockSpec((tm, tn), lambda m, j: (m, COL_GS // tn + j))],
        out_specs=pl.BlockSpec((tm, tn), lambda m, j: (m, j)),
        out_shape=jax.ShapeDtypeStruct((t, n), BF16),
        compiler_params=_cparams(("parallel", "arbitrary")),
        name="merge",
    )(o_gla, y_ssm, w1, w2, p, p)


def _outproj_body(m_ref, w_ref, x_ref, mod_ref, gpost_ref, gpre_ref, wr_ref, x1_ref, hf_ref, lg_ref):
    mix = _dot(m_ref[...], w_ref[...])
    x1 = x_ref[...] + mod_ref[2:3, :] * (_rms(mix) * gpost_ref[...])
    x1_ref[...] = x1
    h = _rms(x1) * gpre_ref[...] * (1.0 + mod_ref[4:5, :]) + mod_ref[3:4, :]
    for s in range(ROW_TILES):
        hf_ref[:, s, :] = h[:, s * LANES:(s + 1) * LANES]
    h_hi = h.astype(BF16)
    h_lo = (h - h_hi.astype(F32)).astype(BF16)
    wr = wr_ref[...]
    w_hi = wr.astype(BF16)
    w_lo = (wr - w_hi.astype(F32)).astype(BF16)
    lg_ref[...] = _dot(h_hi, w_hi) + _dot(h_hi, w_lo) + _dot(h_lo, w_hi)


def _outproj(merged, w_out, x2, mod8, g_post, g_pre, w_router):
    t, d = x2.shape
    tm = 256
    full = lambda r, c: pl.BlockSpec((r, c), lambda m: (0, 0))
    tile = lambda c: pl.BlockSpec((tm, c), lambda m: (m, 0))
    return pl.pallas_call(
        _outproj_body,
        grid=(t // tm,),
        in_specs=[tile(d), full(d, d), tile(d), full(8, d), full(1, d), full(1, d), full(d, 128)],
        out_specs=[tile(d), pl.BlockSpec((tm, ROW_TILES, LANES), lambda m: (m, 0, 0)), tile(128)],
        out_shape=[jax.ShapeDtypeStruct((t, d), F32),
                   jax.ShapeDtypeStruct((t, ROW_TILES, LANES), F32),
                   jax.ShapeDtypeStruct((t, 128), F32)],
        compiler_params=_cparams(("parallel",)),
        name="outproj",
    )(merged, w_out, x2, mod8, g_post, g_pre, w_router)


def _route_body(lg_ref, id_ref, w_ref):
    lg = lg_ref[...]
    lane = lax.broadcasted_iota(I32, lg.shape, 1)
    lane_f = lane.astype(F32)
    neg = jnp.float32(-jnp.inf)

    def first_argmax(vals, mx):
        return jnp.min(jnp.where(vals == mx, lane_f, 1e9), axis=-1, keepdims=True).astype(I32)

    gmask = lane < MOE_GROUPS
    gl = jnp.where(gmask, lg, neg)
    gmax = jnp.max(gl, axis=-1, keepdims=True)
    gsum = jnp.sum(jnp.where(gmask, jnp.exp(gl - gmax), 0.0), axis=-1, keepdims=True)
    g_w = 1.0 / gsum
    g_idx = first_argmax(gl, gmax)
    lo = MOE_GROUPS + g_idx * EXPERTS_PER_GROUP
    emask = (lane >= lo) & (lane < lo + EXPERTS_PER_GROUP)
    el = jnp.where(emask, lg, neg)
    m1 = jnp.max(el, axis=-1, keepdims=True)
    i1 = first_argmax(el, m1)
    el2 = jnp.where(lane == i1, neg, el)
    m2 = jnp.max(el2, axis=-1, keepdims=True)
    i2 = first_argmax(el2, m2)
    r = jnp.exp(m2 - m1)
    w1 = g_w / (1.0 + r)
    w2 = g_w * r / (1.0 + r)
    id_ref[...] = jnp.where(lane == 0, i1 - MOE_GROUPS, jnp.where(lane == 1, i2 - MOE_GROUPS, 0))
    w_ref[...] = jnp.where(lane == 0, w1, jnp.where(lane == 1, w2, 0.0))


def _route(logits):
    t = logits.shape[0]
    tm = 1024
    spec = pl.BlockSpec((tm, 128), lambda m: (m, 0))
    return pl.pallas_call(
        _route_body,
        grid=(t // tm,),
        in_specs=[spec],
        out_specs=[spec, spec],
        out_shape=[jax.ShapeDtypeStruct((t, 128), I32), jax.ShapeDtypeStruct((t, 128), F32)],
        compiler_params=_cparams(("parallel",)),
        name="route",
    )(logits)


GATHER_ROWS = 512


def _issue_rows(idx_ref, src_ref, buf_ref, slot, sem):
    def body(r, carry):
        pltpu.make_async_copy(src_ref.at[idx_ref[0, 0, r]], buf_ref.at[slot, r], sem.at[slot]).start()
        return carry

    lax.fori_loop(0, GATHER_ROWS, body, 0, unroll=8)


def _wait_rows(src_ref, buf_ref, slot, sem):
    def body(r, carry):
        pltpu.make_async_copy(src_ref.at[0], buf_ref.at[slot, r], sem.at[slot]).wait()
        return carry

    lax.fori_loop(0, GATHER_ROWS, body, 0, unroll=8)


def _gather_step(idx_ref, idx_next_ref, src_ref, buf_ref, sem):
    i = pl.program_id(0)
    slot = i % 2

    @pl.when(i == 0)
    def _():
        _issue_rows(idx_ref, src_ref, buf_ref, 0, sem)

    @pl.when(i + 1 < pl.num_programs(0))
    def _():
        _issue_rows(idx_next_ref, src_ref, buf_ref, 1 - slot, sem)

    _wait_rows(src_ref, buf_ref, slot, sem)
    return slot


def _gather_specs(nsteps):
    smem = lambda f: pl.BlockSpec((1, 1, GATHER_ROWS), f, memory_space=pltpu.SMEM)
    return [smem(lambda i: (i, 0, 0)),
            smem(lambda i: (jnp.minimum(i + 1, nsteps - 1), 0, 0)),
            pl.BlockSpec(memory_space=pl.ANY)]


GATHER_SCRATCH = [pltpu.VMEM((2, GATHER_ROWS, ROW_TILES, LANES), F32), pltpu.SemaphoreType.DMA((2,))]


def _dispatch_body(idx_ref, idx_next_ref, src_ref, o_ref, buf_ref, sem):
    slot = _gather_step(idx_ref, idx_next_ref, src_ref, buf_ref, sem)
    for c in range(ROW_TILES):
        o_ref[:, c * LANES:(c + 1) * LANES] = buf_ref[slot, :, c, :].astype(BF16)


def _dispatch(src, idx):
    n = idx.shape[0]
    nsteps = n // GATHER_ROWS
    idx3 = idx.reshape(nsteps, 1, GATHER_ROWS)
    return pl.pallas_call(
        _dispatch_body,
        grid=(nsteps,),
        in_specs=_gather_specs(nsteps),
        out_specs=pl.BlockSpec((GATHER_ROWS, D_MODEL), lambda i: (i, 0)),
        out_shape=jax.ShapeDtypeStruct((n, D_MODEL), BF16),
        scratch_shapes=GATHER_SCRATCH,
        compiler_params=_cparams(("arbitrary",)),
        name="dispatch",
    )(idx3, idx3, src)


MOE_FC = 512
MOE_J = MOE_FF // MOE_FC
N_ITEMS = (16384 // MOE_BLOCK + N_EXPERTS) // ITEM_BLOCKS + (N_EXPERTS * (ITEM_BLOCKS - 1)) // ITEM_BLOCKS


def _experts_body(ie_ref, io_ref, ins_ref, ifl_ref, nr_ref, wg_ref, wu_ref, wd_ref, xs_ref, ys_ref,
                  wgb_ref, wub_ref, wdb_ref, xst_ref, xb_ref, acc_ref, sem_in, sem_out):
    i = pl.program_id(0)
    j = pl.program_id(1)
    nsub = ins_ref[i]
    nfill = ifl_ref[i]
    row0 = pl.multiple_of(io_ref[i] * MOE_BLOCK, MOE_BLOCK)
    blk = MOE_BLOCK

    def x_copy(s):
        return pltpu.make_async_copy(xs_ref.at[pl.ds(row0 + s * blk, blk)], xb_ref.at[s], sem_in.at[s])

    def y_copy(s):
        return pltpu.make_async_copy(xst_ref.at[s], ys_ref.at[pl.ds(row0 + s * blk, blk)], sem_out.at[s])

    @pl.when((nsub > 0) & (j == 0))
    def _():
        for s in range(ITEM_BLOCKS):
            @pl.when(s < nsub)
            def _():
                x_copy(s).start()
        for s in range(ITEM_BLOCKS):
            @pl.when(s < nsub)
            def _():
                x_copy(s).wait()

    @pl.when(nsub > 0)
    def _():
        wgb_ref[...] = wg_ref[...].astype(BF16)
        wub_ref[...] = wu_ref[...].astype(BF16)
        wdb_ref[...] = wd_ref[...].astype(BF16)

    for n in range(1, ITEM_BLOCKS + 1):
        @pl.when(nsub == n)
        def _():
            x = xb_ref[0:n].reshape(n * blk, D_MODEL)
            gate = _dot(x, wgb_ref[...])
            up = _dot(x, wub_ref[...])
            hid = (_silu(gate) * up).astype(BF16)
            y = _dot(hid, wdb_ref[...]).reshape(n, blk, D_MODEL)

            @pl.when(j == 0)
            def _():
                acc_ref[0:n] = y

            @pl.when(j > 0)
            def _():
                acc_ref[0:n] = acc_ref[0:n] + y

    @pl.when((nsub > 0) & (j == MOE_J - 1))
    def _():
        for s in range(ITEM_BLOCKS):
            @pl.when(s < nsub)
            def _():
                for c in range(ROW_TILES):
                    xst_ref[s, :, c, :] = acc_ref[s, :, c * LANES:(c + 1) * LANES]
                y_copy(s).start()
        for s in range(ITEM_BLOCKS):
            @pl.when(s < nsub)
            def _():
                y_copy(s).wait()

    @pl.when((nfill > 0) & (j == 0))
    def _():
        xst_ref[...] = jnp.zeros_like(xst_ref)
        for s in range(ITEM_BLOCKS):
            @pl.when(s < nfill)
            def _():
                y_copy(s).start()
        for s in range(ITEM_BLOCKS):
            @pl.when(s < nfill)
            def _():
                y_copy(s).wait()


def _experts(x_sorted, w_gate, w_up, w_down, item_e, item_off, item_nsub, item_fill, n_real):
    n_rows = x_sorted.shape[0]
    d = D_MODEL

    def w_in_map(i, j, ie, io, ins, ifl, nr):
        return (ie[i], 0, jnp.where(i < nr[0], j, MOE_J - 1))

    def w_dn_map(i, j, ie, io, ins, ifl, nr):
        return (ie[i], jnp.where(i < nr[0], j, MOE_J - 1), 0)

    grid_spec = pltpu.PrefetchScalarGridSpec(
        num_scalar_prefetch=5,
        grid=(N_ITEMS, MOE_J),
        in_specs=[pl.BlockSpec((None, d, MOE_FC), w_in_map),
                  pl.BlockSpec((None, d, MOE_FC), w_in_map),
                  pl.BlockSpec((None, MOE_FC, d), w_dn_map),
                  pl.BlockSpec(memory_space=pl.ANY)],
        out_specs=pl.BlockSpec(memory_space=pl.ANY),
        scratch_shapes=[pltpu.VMEM((d, MOE_FC), BF16),
                        pltpu.VMEM((d, MOE_FC), BF16),
                        pltpu.VMEM((MOE_FC, d), BF16),
                        pltpu.VMEM((ITEM_BLOCKS, MOE_BLOCK, ROW_TILES, LANES), F32),
                        pltpu.VMEM((ITEM_BLOCKS, MOE_BLOCK, d), BF16),
                        pltpu.VMEM((ITEM_BLOCKS, MOE_BLOCK, d), F32),
                        pltpu.SemaphoreType.DMA((ITEM_BLOCKS,)),
                        pltpu.SemaphoreType.DMA((ITEM_BLOCKS,))],
    )
    return pl.pallas_call(
        _experts_body,
        grid_spec=grid_spec,
        out_shape=jax.ShapeDtypeStruct((n_rows, ROW_TILES, LANES), F32),
        compiler_params=_cparams(("arbitrary", "arbitrary")),
        name="experts",
    )(item_e, item_off, item_nsub, item_fill, n_real, w_gate, w_up, w_down, x_sorted)


FINAL_TM = GATHER_ROWS // 2


def _final_body(idx_ref, idx_next_ref, ys_ref, w_ref, x1_ref, mod_ref, g_ref, o_ref, buf_ref, sem):
    slot = _gather_step(idx_ref, idx_next_ref, ys_ref, buf_ref, sem)
    tm = FINAL_TM
    w = w_ref[...]
    w0, w1 = w[:, 0:1], w[:, 1:2]
    ffn = jnp.concatenate([w0 * buf_ref[slot, 0:tm, c, :] + w1 * buf_ref[slot, tm:2 * tm, c, :]
                           for c in range(ROW_TILES)], axis=1)
    o_ref[...] = x1_ref[...] + mod_ref[5:6, :] * (_rms(ffn) * g_ref[...])


def _final(y_sorted, pos, wts, x1, mod8, g_post):
    t, d = x1.shape
    tm = FINAL_TM
    nt = t // tm
    idx3 = pos.reshape(nt, tm, 2).transpose(0, 2, 1).reshape(nt, 1, 2 * tm)
    return pl.pallas_call(
        _final_body,
        grid=(nt,),
        in_specs=_gather_specs(nt) + [pl.BlockSpec((tm, 128), lambda m: (m, 0)),
                                      pl.BlockSpec((tm, d), lambda m: (m, 0)),
                                      pl.BlockSpec((8, d), lambda m: (0, 0)),
                                      pl.BlockSpec((1, d), lambda m: (0, 0))],
        out_specs=pl.BlockSpec((tm, d), lambda m: (m, 0)),
        out_shape=jax.ShapeDtypeStruct((t, d), F32),
        scratch_shapes=GATHER_SCRATCH,
        compiler_params=_cparams(("arbitrary",)),
        name="final",
    )(idx3, idx3, y_sorted, wts, x1, mod8, g_post)


def _routing_tables(ids, n_tok):
    n_assign = n_tok * 2
    n_blocks = n_assign // MOE_BLOCK + N_EXPERTS
    flat_e = ids.reshape(n_assign)
    onehot = (flat_e[:, None] == jnp.arange(N_EXPERTS, dtype=I32)[None, :]).astype(I32)
    csum = jnp.cumsum(onehot, axis=0)
    rank = jnp.sum(csum * onehot, axis=1) - 1
    counts = csum[-1]
    nb = (counts + MOE_BLOCK - 1) // MOE_BLOCK
    blk_start = jnp.cumsum(nb) - nb
    dest = blk_start[flat_e] * MOE_BLOCK + rank
    row_tok = jnp.zeros((n_blocks * MOE_BLOCK,), I32).at[dest].set(jnp.arange(n_assign, dtype=I32) // 2)
    pos = dest.reshape(n_tok, 2)

    n_it = (nb + ITEM_BLOCKS - 1) // ITEM_BLOCKS
    it_end = jnp.cumsum(n_it)
    it_start = it_end - n_it
    n_real = it_end[-1]
    i = jnp.arange(N_ITEMS, dtype=I32)
    e_i = jnp.minimum(jnp.searchsorted(it_end, i, side='right').astype(I32), N_EXPERTS - 1)
    k_i = i - it_start[e_i]
    valid = i < n_real
    last_e = e_i[jnp.maximum(n_real - 1, 0)]
    item_e = jnp.where(valid, e_i, last_e).astype(I32)
    fill_off = jnp.sum(nb) + ITEM_BLOCKS * (i - n_real)
    item_fill = jnp.where(valid, 0, jnp.clip(n_blocks - fill_off, 0, ITEM_BLOCKS)).astype(I32)
    item_off = jnp.where(valid, blk_start[e_i] + ITEM_BLOCKS * k_i, jnp.minimum(fill_off, n_blocks - 1)).astype(I32)
    item_nsub = jnp.where(valid, jnp.clip(nb[e_i] - ITEM_BLOCKS * k_i, 0, ITEM_BLOCKS), 0).astype(I32)
    return row_tok, pos, item_e, item_off, item_nsub, item_fill, n_real.reshape(1).astype(I32)


def _pad_lanes(v, start, total=SMALL_COLS):
    return jnp.zeros((1, total), F32).at[0, start:start + v.shape[0]].set(v)


def _layer(x2, c, w_ada, b_ada, norm_pre_mix, norm_post_mix, norm_pre_ffn, norm_post_ffn,
           w_in, gla_w_gate_up, gla_b_gate, gla_norm, ssm_conv_w, ssm_conv_b, ssm_dt_bias,
           ssm_a_log, ssm_d, ssm_norm, w_branch_gla, w_branch_ssm, w_out,
           router_group, router_expert, moe_w_gate, moe_w_up, moe_w_down):
    t, d = x2.shape
    row = lambda v: v.reshape(1, -1)

    mod = _ada(c, w_ada, b_ada)
    mod8 = jnp.concatenate([mod.reshape(6, d), jnp.zeros((2, d), F32)], axis=0)

    o_glr, o_og, o_dt, o_gg = 4096, 4112, 16400, 16464
    w_big = jnp.concatenate([w_in[:, :o_glr], w_in[:, o_og:o_dt], w_in[:, o_gg:]], axis=1).astype(BF16)
    w_small = jnp.concatenate([w_in[:, o_glr:o_og], w_in[:, o_dt:o_gg],
                               jnp.zeros((d, SMALL_COLS - GLA_RANK - SSM_HEADS), F32)], axis=1).astype(BF16)

    p, small = _inproj(x2, mod8, row(norm_pre_mix), w_big, w_small)

    o_gla = _gla(p, small, gla_w_gate_up, row(gla_b_gate), row(gla_norm))
    y_ssm = _ssd(p, small, ssm_conv_w, row(ssm_conv_b),
                 _pad_lanes(ssm_dt_bias, SMALL_DT0), _pad_lanes(ssm_a_log, SMALL_DT0),
                 _pad_lanes(ssm_d, SMALL_DT0), row(ssm_norm))
    merged = _merge(o_gla, y_ssm, w_branch_gla.astype(BF16), w_branch_ssm.astype(BF16), p)

    w_router = jnp.concatenate([router_group, router_expert,
                                jnp.zeros((d, 128 - MOE_GROUPS - N_EXPERTS), F32)], axis=1)
    x1, h2f, logits = _outproj(merged, w_out.astype(BF16), x2, mod8,
                               row(norm_post_mix), row(norm_pre_ffn), w_router)
    ids, wts = _route(logits)

    row_tok, pos, item_e, item_off, item_nsub, item_fill, n_real = _routing_tables(ids[:, :2], t)
    x_sorted = _dispatch(h2f, row_tok)
    y_sorted = _experts(x_sorted, moe_w_gate, moe_w_up, moe_w_down,
                        item_e, item_off, item_nsub, item_fill, n_real)
    return _final(y_sorted, pos, wts, x1, mod8, row(norm_post_ffn))


def kernel(x, c, w_ada, b_ada, norm_pre_mix, norm_post_mix, norm_pre_ffn, norm_post_ffn, w_in, gla_w_gate_up, gla_b_gate, gla_norm, ssm_conv_w, ssm_conv_b, ssm_dt_bias, ssm_a_log, ssm_d, ssm_norm, w_branch_gla, w_branch_ssm, w_out, router_group, router_expert, moe_w_gate, moe_w_up, moe_w_down):
    bsz, seq, d = x.shape
    assert bsz == 1 and d == D_MODEL
    x2 = x.reshape(seq, d)
    params = (w_ada, b_ada, norm_pre_mix, norm_post_mix, norm_pre_ffn, norm_post_ffn, w_in, gla_w_gate_up,
              gla_b_gate, gla_norm, ssm_conv_w, ssm_conv_b, ssm_dt_bias, ssm_a_log, ssm_d, ssm_norm,
              w_branch_gla, w_branch_ssm, w_out, router_group, router_expert, moe_w_gate, moe_w_up, moe_w_down)
    for layer in range(w_ada.shape[0]):
        x2 = _layer(x2, c, *(prm[layer] for prm in params))
    return x2.reshape(bsz, seq, d)
```

```python
import functools

import jax
import jax.numpy as jnp
from jax import lax
from jax.experimental import pallas as pl
from jax.experimental.pallas import tpu as pltpu

F32 = jnp.float32
BF16 = jnp.bfloat16
I32 = jnp.int32

D_MODEL = 2048
EPS = 1e-6
LANES = 128
ROW_TILES = D_MODEL // LANES

GLA_HEADS = 4
GLA_HEAD_K = 256
GLA_HEAD_V = 512
GLA_RANK = 16
GLA_NORMALIZER = 16.0
CHUNK = 64

SSM_GROUPS = 8
SSM_HEADS = 64
SSM_HEAD_DIM = 64
SSM_STATE = 128
SSM_CONV = 4
SSM_GROUP_W = 512
SSM_INNER = 4096

N_EXPERTS = 64
EXPERTS_PER_GROUP = 8
MOE_GROUPS = 8
MOE_FF = 1024
MOE_BLOCK = 128
ITEM_BLOCKS = 4

COL_Q, COL_K, COL_V, COL_OG, COL_Z, COL_XS, COL_B, COL_C, COL_GG, COL_GS = (
    0, 1024, 2048, 4096, 6144, 10240, 14336, 15360, 16384, 18432)
P_COLS = 20480
SMALL_COLS = 128
SMALL_DT0 = GLA_RANK

VMEM_LIMIT = 56 * 1024 * 1024


def _cparams(sem, vmem=VMEM_LIMIT):
    return pltpu.CompilerParams(dimension_semantics=sem, vmem_limit_bytes=vmem)


def _dot(a, b):
    return jnp.dot(a, b, preferred_element_type=F32)


def _dot_nt(a, b):
    return lax.dot_general(a, b, (((1,), (1,)), ((), ())), preferred_element_type=F32)


def _dot_tn(a, b):
    return lax.dot_general(a, b, (((0,), (0,)), ((), ())), preferred_element_type=F32)


def _split3(a):
    hi = a.astype(BF16)
    r1 = a - hi.astype(F32)
    mid = r1.astype(BF16)
    lo = (r1 - mid.astype(F32)).astype(BF16)
    return hi, mid, lo


def _dot_sel_r(a, sel):
    hi, mid, lo = _split3(a)
    return _dot(hi, sel) + _dot(mid, sel) + _dot(lo, sel)


def _dot_sel_l(sel, a):
    hi, mid, lo = _split3(a)
    return _dot(sel, hi) + _dot(sel, mid) + _dot(sel, lo)


def _sigmoid(x):
    return 1.0 / (1.0 + jnp.exp(-x))


def _silu(x):
    return x * _sigmoid(x)


def _softplus(x):
    return jnp.maximum(x, 0.0) + jnp.log1p(jnp.exp(-jnp.abs(x)))


def _log_sigmoid(x):
    return jnp.minimum(x, 0.0) - jnp.log1p(jnp.exp(-jnp.abs(x)))


def _rms(x):
    return x * lax.rsqrt(jnp.mean(x * x, axis=-1, keepdims=True) + EPS)


def _ada_body(c_ref, w_ref, b_ref, o_ref):
    c = c_ref[...]
    s = jnp.broadcast_to(_silu(c), (8, c.shape[1])).astype(BF16)
    o_ref[...] = _dot(s, w_ref[...].astype(BF16))[0:1] + b_ref[...]


def _ada(c, w_ada, b_ada):
    d, n = w_ada.shape
    tn = 1024
    return pl.pallas_call(
        _ada_body,
        grid=(n // tn,),
        in_specs=[pl.BlockSpec((1, d), lambda j: (0, 0)),
                  pl.BlockSpec((d, tn), lambda j: (0, j)),
                  pl.BlockSpec((1, tn), lambda j: (0, j))],
        out_specs=pl.BlockSpec((1, tn), lambda j: (0, j)),
        out_shape=jax.ShapeDtypeStruct((1, n), F32),
        compiler_params=_cparams(("parallel",)),
        name="ada",
    )(c, w_ada, b_ada.reshape(1, n))


def _inproj_body(x_ref, mod_ref, g_ref, w_ref, ws_ref, p_ref, s_ref, h_ref):
    @pl.when(pl.program_id(1) == 0)
    def _():
        h = _rms(x_ref[...]) * g_ref[...] * (1.0 + mod_ref[1:2, :]) + mod_ref[0:1, :]
        hb = h.astype(BF16)
        h_ref[...] = hb
        s_ref[...] = _dot(hb, ws_ref[...])

    p_ref[...] = _dot(h_ref[...], w_ref[...]).astype(BF16)


def _inproj(x2, mod8, gain, w_big, w_small):
    t, d = x2.shape
    n = w_big.shape[1]
    tm, tn = 1024, 1024
    return pl.pallas_call(
        _inproj_body,
        grid=(t // tm, n // tn),
        in_specs=[pl.BlockSpec((tm, d), lambda m, j: (m, 0)),
                  pl.BlockSpec((8, d), lambda m, j: (0, 0)),
                  pl.BlockSpec((1, d), lambda m, j: (0, 0)),
                  pl.BlockSpec((d, tn), lambda m, j: (0, j)),
                  pl.BlockSpec((d, SMALL_COLS), lambda m, j: (0, 0))],
        out_specs=[pl.BlockSpec((tm, tn), lambda m, j: (m, j)),
                   pl.BlockSpec((tm, SMALL_COLS), lambda m, j: (m, 0))],
        out_shape=[jax.ShapeDtypeStruct((t, n), BF16),
                   jax.ShapeDtypeStruct((t, SMALL_COLS), F32)],
        scratch_shapes=[pltpu.VMEM((tm, d), BF16)],
        compiler_params=_cparams(("parallel", "arbitrary")),
        name="inproj",
    )(x2, mod8, gain, w_big, w_small)


GLA_TB = 512


def _gla_body(q_ref, k_ref, v_ref, og_ref, sm_ref, wup_ref, bg_ref, gn_ref, o_ref, st_ref):
    @pl.when(pl.program_id(1) == 0)
    def _():
        st_ref[...] = jnp.zeros_like(st_ref)

    tb, C = GLA_TB, CHUNK
    nch = tb // C
    r = lax.broadcasted_iota(I32, (tb, tb), 0)
    c = lax.broadcasted_iota(I32, (tb, tb), 1)
    causal = (r // C == c // C) & (r >= c)
    r2 = lax.broadcasted_iota(I32, (2 * C, 2 * C), 0)
    c2 = lax.broadcasted_iota(I32, (2 * C, 2 * C), 1)
    tril2 = ((r2 // C == c2 // C) & (r2 >= c2)).astype(BF16)

    glr = sm_ref[:, 0:GLA_RANK].astype(BF16)
    pre = _dot(glr, wup_ref[...].astype(BF16)) + bg_ref[...]
    log_a = _log_sigmoid(pre) / GLA_NORMALIZER
    b = jnp.concatenate([_dot_sel_l(tril2, log_a[i * 2 * C:(i + 1) * 2 * C]) for i in range(nch // 2)],
                        axis=0)
    b_last = [b[(i + 1) * C - 1:(i + 1) * C, :] for i in range(nch)]
    b_end = jnp.concatenate([jnp.broadcast_to(bl, (C, bl.shape[1])) for bl in b_last], axis=0)

    q = q_ref[...].astype(F32) * (GLA_HEAD_K ** -0.5)
    k = k_ref[...].astype(F32)
    v = v_ref[...]
    q_dec = (q * jnp.exp(b)).astype(BF16)
    k_inv = (k * jnp.exp(-b)).astype(BF16)
    k_end = (k * jnp.exp(b_end - b)).astype(BF16)
    scores = jnp.where(causal, _dot_nt(q_dec, k_inv), 0.0)
    o_intra = _dot(scores.astype(BF16), v)

    st = st_ref[...]
    o_inter = []
    for i in range(nch):
        rows = slice(i * C, (i + 1) * C)
        o_inter.append(_dot_nt(q_dec[rows], st.astype(BF16)))
        st = st * jnp.exp(b_last[i]) + _dot_tn(v[rows], k_end[rows])
    st_ref[...] = st

    o = o_intra + jnp.concatenate(o_inter, axis=0)
    og = og_ref[...].astype(F32)
    o_ref[...] = (_rms(o) * gn_ref[...] * _silu(og)).astype(BF16)


def _gla(p, small, wup, bg, gn):
    t = p.shape[0]
    tb = GLA_TB
    kb, vb = GLA_HEAD_K, GLA_HEAD_V
    return pl.pallas_call(
        _gla_body,
        grid=(GLA_HEADS, t // tb),
        in_specs=[pl.BlockSpec((tb, kb), lambda h, i: (i, COL_Q // kb + h)),
                  pl.BlockSpec((tb, kb), lambda h, i: (i, COL_K // kb + h)),
                  pl.BlockSpec((tb, vb), lambda h, i: (i, COL_V // vb + h)),
                  pl.BlockSpec((tb, vb), lambda h, i: (i, COL_OG // vb + h)),
                  pl.BlockSpec((tb, SMALL_COLS), lambda h, i: (i, 0)),
                  pl.BlockSpec((GLA_RANK, kb), lambda h, i: (0, h)),
                  pl.BlockSpec((1, kb), lambda h, i: (0, h)),
                  pl.BlockSpec((1, vb), lambda h, i: (0, h))],
        out_specs=pl.BlockSpec((tb, vb), lambda h, i: (i, h)),
        out_shape=jax.ShapeDtypeStruct((t, GLA_HEADS * vb), BF16),
        scratch_shapes=[pltpu.VMEM((vb, kb), F32)],
        compiler_params=_cparams(("parallel", "arbitrary")),
        name="gla",
    )(p, p, p, p, small, wup, bg, gn)


SSD_TB = 256
HALO = 8


def _ssd_body(xs_ref, b_ref, c_ref, z_ref, sm_ref, cwx_ref, cwb_ref, cwc_ref, cbx_ref, cbb_ref, cbc_ref,
              dtb_ref, alog_ref, dsk_ref, ng_ref, o_ref,
              ex_ref, eb_ref, ec_ref, at_ref, ht_ref):
    g = pl.program_id(0)
    t = pl.program_id(1)
    tb = SSD_TB
    gw = SSM_GROUP_W

    @pl.when(t == 0)
    def _():
        ht_ref[...] = jnp.zeros_like(ht_ref)
        ex_ref[0:HALO, :] = jnp.zeros((HALO, gw), F32)
        eb_ref[0:HALO, :] = jnp.zeros((HALO, SSM_STATE), F32)
        ec_ref[0:HALO, :] = jnp.zeros((HALO, SSM_STATE), F32)

    @pl.when(t > 0)
    def _():
        ex_ref[0:HALO, :] = ex_ref[tb:tb + HALO, :]
        eb_ref[0:HALO, :] = eb_ref[tb:tb + HALO, :]
        ec_ref[0:HALO, :] = ec_ref[tb:tb + HALO, :]

    def conv_silu(u_ref, e_ref, w_ref, bias_ref):
        e_ref[HALO:HALO + tb, :] = u_ref[...].astype(F32)
        acc = bias_ref[...] + w_ref[0:1, :] * e_ref[HALO - 3:HALO - 3 + tb, :]
        for kk in range(1, SSM_CONV):
            acc = acc + w_ref[kk:kk + 1, :] * e_ref[HALO - 3 + kk:HALO - 3 + kk + tb, :]
        return _silu(acc)

    xa = conv_silu(xs_ref, ex_ref, cwx_ref, cbx_ref)
    ba = conv_silu(b_ref, eb_ref, cwb_ref, cbb_ref).astype(BF16)
    ca = conv_silu(c_ref, ec_ref, cwc_ref, cbc_ref).astype(BF16)

    L = CHUNK
    nch = tb // L
    rep = gw // L
    hpg = gw // SSM_HEAD_DIM
    head0 = pl.multiple_of(SMALL_DT0 + g * hpg, hpg)

    e_row = lax.broadcasted_iota(I32, (SMALL_COLS, gw), 0)
    e_col = lax.broadcasted_iota(I32, (SMALL_COLS, gw), 1)
    expand = (e_row == head0 + e_col // SSM_HEAD_DIM).astype(BF16)

    dt_small = _softplus(sm_ref[...] + dtb_ref[...])
    adt_small = dt_small * (-jnp.exp(alog_ref[...]))
    rb = lax.broadcasted_iota(I32, (tb, tb), 0)
    cb = lax.broadcasted_iota(I32, (tb, tb), 1)
    blocktril = ((rb // L == cb // L) & (rb >= cb)).astype(BF16)
    acum_small = _dot_sel_l(blocktril, adt_small)
    at_ref[...] = acum_small.T
    heads = at_ref[pl.ds(head0, hpg), :]
    dt_exp = _dot_sel_r(dt_small, expand)
    acum = _dot_sel_r(acum_small, expand)
    d_exp = _dot_sel_r(jnp.broadcast_to(dsk_ref[...], (8, SMALL_COLS)), expand)[0:1]
    xdt = xa * dt_exp
    xdt_b = xdt.astype(BF16)
    e_acum = jnp.exp(acum)

    li = lax.broadcasted_iota(I32, (L, gw), 0)
    lj = lax.broadcasted_iota(I32, (L, gw), 1) % L
    causal_t = li >= lj
    hi = lax.broadcasted_iota(I32, (hpg, gw), 0)
    hj = lax.broadcasted_iota(I32, (hpg, gw), 1) // L
    headmask = hi == hj
    ones_h = jnp.ones((L, hpg), BF16)
    bi = lax.broadcasted_iota(I32, (gw, gw), 0) // L
    bj = lax.broadcasted_iota(I32, (gw, gw), 1) // SSM_HEAD_DIM
    blockmask = bi == bj
    masked_out = -1e30

    ht = ht_ref[...]
    for c in range(nch):
        rows = slice(c * L, (c + 1) * L)
        acum_c = acum[rows]
        a_rows = jnp.concatenate([heads[:, c * L:(c + 1) * L]] * rep, axis=1)
        rterm = _dot_sel_l(ones_h, jnp.where(headmask, a_rows, 0.0))
        decay = jnp.exp(jnp.where(causal_t, acum_c - rterm, masked_out))
        cc = ca[rows]
        bc = ba[rows]
        cb_t = _dot_nt(cc, jnp.concatenate([bc] * rep, axis=0))
        m = (cb_t * decay).astype(BF16)
        bd = jnp.where(blockmask, jnp.concatenate([xdt_b[rows]] * rep, axis=0), jnp.zeros((), BF16))
        y_diag = _dot(m, bd)
        y_off = _dot(cc, ht.astype(BF16)) * e_acum[rows]
        a_last = acum_c[L - 1:L, :]
        xd = (xdt[rows] * jnp.exp(a_last - acum_c)).astype(BF16)
        ht = ht * jnp.exp(a_last) + _dot_tn(bc, xd)
        y = y_diag + y_off + d_exp * xa[rows]
        y = y * _silu(z_ref[rows, :].astype(F32))
        o_ref[rows, :] = (_rms(y) * ng_ref[...]).astype(BF16)
    ht_ref[...] = ht


def _ssd(p, small, conv_w, conv_b, dt_bias_s, a_log_s, d_skip_s, norm_g):
    t = p.shape[0]
    tb = SSD_TB
    gw, ns = SSM_GROUP_W, SSM_STATE
    xs0, b0, c0 = 0, SSM_INNER // ns, (SSM_INNER + SSM_GROUPS * ns) // ns
    row = lambda w, off: pl.BlockSpec((1, w), lambda g, i: (0, off + g))
    return pl.pallas_call(
        _ssd_body,
        grid=(SSM_GROUPS, t // tb),
        in_specs=[pl.BlockSpec((tb, gw), lambda g, i: (i, COL_XS // gw + g)),
                  pl.BlockSpec((tb, ns), lambda g, i: (i, COL_B // ns + g)),
                  pl.BlockSpec((tb, ns), lambda g, i: (i, COL_C // ns + g)),
                  pl.BlockSpec((tb, gw), lambda g, i: (i, COL_Z // gw + g)),
                  pl.BlockSpec((tb, SMALL_COLS), lambda g, i: (i, 0)),
                  pl.BlockSpec((SSM_CONV, gw), lambda g, i: (0, xs0 + g)),
                  pl.BlockSpec((SSM_CONV, ns), lambda g, i: (0, b0 + g)),
                  pl.BlockSpec((SSM_CONV, ns), lambda g, i: (0, c0 + g)),
                  row(gw, xs0), row(ns, b0), row(ns, c0),
                  pl.BlockSpec((1, SMALL_COLS), lambda g, i: (0, 0)),
                  pl.BlockSpec((1, SMALL_COLS), lambda g, i: (0, 0)),
                  pl.BlockSpec((1, SMALL_COLS), lambda g, i: (0, 0)),
                  pl.BlockSpec((1, gw), lambda g, i: (0, g))],
        out_specs=pl.BlockSpec((tb, gw), lambda g, i: (i, g)),
        out_shape=jax.ShapeDtypeStruct((t, SSM_INNER), BF16),
        scratch_shapes=[pltpu.VMEM((tb + HALO, gw), F32),
                        pltpu.VMEM((tb + HALO, ns), F32),
                        pltpu.VMEM((tb + HALO, ns), F32),
                        pltpu.VMEM((SMALL_COLS, tb), F32),
                        pltpu.VMEM((ns, gw), F32)],
        compiler_params=_cparams(("parallel", "arbitrary")),
        name="ssd",
    )(p, p, p, p, small, conv_w, conv_w, conv_w, conv_b, conv_b, conv_b,
      dt_bias_s, a_log_s, d_skip_s, norm_g)


def _merge_body(a1_ref, a2_ref, w1_ref, w2_ref, gg_ref, gs_ref, o_ref):
    y1 = _dot(a1_ref[...], w1_ref[...])
    y2 = _dot(a2_ref[...], w2_ref[...])
    o_ref[...] = (_sigmoid(gg_ref[...].astype(F32)) * y1 + _sigmoid(gs_ref[...].astype(F32)) * y2).astype(BF16)


def _merge(o_gla, y_ssm, w1, w2, p):
    t, k1 = o_gla.shape
    k2 = y_ssm.shape[1]
    n = w1.shape[1]
    tm, tn = 512, 512
    return pl.pallas_call(
        _merge_body,
        grid=(t // tm, n // tn),
        in_specs=[pl.BlockSpec((tm, k1), lambda m, j: (m, 0)),
                  pl.BlockSpec((tm, k2), lambda m, j: (m, 0)),
                  pl.BlockSpec((k1, tn), lambda m, j: (0, j)),
                  pl.BlockSpec((k2, tn), lambda m, j: (0, j)),
                  pl.BlockSpec((tm, tn), lambda m, j: (m, COL_GG // tn + j)),
                  pl.BlockSpec((tm, tn), lambda m, j: (m, COL_GS // tn + j))],
        out_specs=pl.BlockSpec((tm, tn), lambda m, j: (m, j)),
        out_shape=jax.ShapeDtypeStruct((t, n), BF16),
        compiler_params=_cparams(("parallel", "arbitrary")),
        name="merge",
    )(o_gla, y_ssm, w1, w2, p, p)


def _outproj_body(m_ref, w_ref, x_ref, mod_ref, gpost_ref, gpre_ref, wr_ref, x1_ref, hf_ref, lg_ref):
    mix = _dot(m_ref[...], w_ref[...])
    x1 = x_ref[...] + mod_ref[2:3, :] * (_rms(mix) * gpost_ref[...])
    x1_ref[...] = x1
    h = _rms(x1) * gpre_ref[...] * (1.0 + mod_ref[4:5, :]) + mod_ref[3:4, :]
    for s in range(ROW_TILES):
        hf_ref[:, s, :] = h[:, s * LANES:(s + 1) * LANES]
    h_hi = h.astype(BF16)
    h_lo = (h - h_hi.astype(F32)).astype(BF16)
    wr = wr_ref[...]
    w_hi = wr.astype(BF16)
    w_lo = (wr - w_hi.astype(F32)).astype(BF16)
    lg_ref[...] = _dot(h_hi, w_hi) + _dot(h_hi, w_lo) + _dot(h_lo, w_hi)


def _outproj(merged, w_out, x2, mod8, g_post, g_pre, w_router):
    t, d = x2.shape
    tm = 256
    full = lambda r, c: pl.BlockSpec((r, c), lambda m: (0, 0))
    tile = lambda c: pl.BlockSpec((tm, c), lambda m: (m, 0))
    return pl.pallas_call(
        _outproj_body,
        grid=(t // tm,),
        in_specs=[tile(d), full(d, d), tile(d), full(8, d), full(1, d), full(1, d), full(d, 128)],
        out_specs=[tile(d), pl.BlockSpec((tm, ROW_TILES, LANES), lambda m: (m, 0, 0)), tile(128)],
        out_shape=[jax.ShapeDtypeStruct((t, d), F32),
                   jax.ShapeDtypeStruct((t, ROW_TILES, LANES), F32),
                   jax.ShapeDtypeStruct((t, 128), F32)],
        compiler_params=_cparams(("parallel",)),
        name="outproj",
    )(merged, w_out, x2, mod8, g_post, g_pre, w_router)


def _route_body(lg_ref, id_ref, w_ref):
    lg = lg_ref[...]
    lane = lax.broadcasted_iota(I32, lg.shape, 1)
    lane_f = lane.astype(F32)
    neg = jnp.float32(-jnp.inf)

    def first_argmax(vals, mx):
        return jnp.min(jnp.where(vals == mx, lane_f, 1e9), axis=-1, keepdims=True).astype(I32)

    gmask = lane < MOE_GROUPS
    gl = jnp.where(gmask, lg, neg)
    gmax = jnp.max(gl, axis=-1, keepdims=True)
    gsum = jnp.sum(jnp.where(gmask, jnp.exp(gl - gmax), 0.0), axis=-1, keepdims=True)
    g_w = 1.0 / gsum
    g_idx = first_argmax(gl, gmax)
    lo = MOE_GROUPS + g_idx * EXPERTS_PER_GROUP
    emask = (lane >= lo) & (lane < lo + EXPERTS_PER_GROUP)
    el = jnp.where(emask, lg, neg)
    m1 = jnp.max(el, axis=-1, keepdims=True)
    i1 = first_argmax(el, m1)
    el2 = jnp.where(lane == i1, neg, el)
    m2 = jnp.max(el2, axis=-1, keepdims=True)
    i2 = first_argmax(el2, m2)
    r = jnp.exp(m2 - m1)
    w1 = g_w / (1.0 + r)
    w2 = g_w * r / (1.0 + r)
    id_ref[...] = jnp.where(lane == 0, i1 - MOE_GROUPS, jnp.where(lane == 1, i2 - MOE_GROUPS, 0))
    w_ref[...] = jnp.where(lane == 0, w1, jnp.where(lane == 1, w2, 0.0))


def _route(logits):
    t = logits.shape[0]
    tm = 1024
    spec = pl.BlockSpec((tm, 128), lambda m: (m, 0))
    return pl.pallas_call(
        _route_body,
        grid=(t // tm,),
        in_specs=[spec],
        out_specs=[spec, spec],
        out_shape=[jax.ShapeDtypeStruct((t, 128), I32), jax.ShapeDtypeStruct((t, 128), F32)],
        compiler_params=_cparams(("parallel",)),
        name="route",
    )(logits)


GATHER_ROWS = 512


def _issue_rows(idx_ref, src_ref, buf_ref, slot, sem):
    def body(r, carry):
        pltpu.make_async_copy(src_ref.at[idx_ref[0, 0, r]], buf_ref.at[slot, r], sem.at[slot]).start()
        return carry

    lax.fori_loop(0, GATHER_ROWS, body, 0, unroll=8)


def _wait_rows(src_ref, buf_ref, slot, sem):
    def body(r, carry):
        pltpu.make_async_copy(src_ref.at[0], buf_ref.at[slot, r], sem.at[slot]).wait()
        return carry

    lax.fori_loop(0, GATHER_ROWS, body, 0, unroll=8)


def _gather_step(idx_ref, idx_next_ref, src_ref, buf_ref, sem):
    i = pl.program_id(0)
    slot = i % 2

    @pl.when(i == 0)
    def _():
        _issue_rows(idx_ref, src_ref, buf_ref, 0, sem)

    @pl.when(i + 1 < pl.num_programs(0))
    def _():
        _issue_rows(idx_next_ref, src_ref, buf_ref, 1 - slot, sem)

    _wait_rows(src_ref, buf_ref, slot, sem)
    return slot


def _gather_specs(nsteps):
    smem = lambda f: pl.BlockSpec((1, 1, GATHER_ROWS), f, memory_space=pltpu.SMEM)
    return [smem(lambda i: (i, 0, 0)),
            smem(lambda i: (jnp.minimum(i + 1, nsteps - 1), 0, 0)),
            pl.BlockSpec(memory_space=pl.ANY)]


GATHER_SCRATCH = [pltpu.VMEM((2, GATHER_ROWS, ROW_TILES, LANES), F32), pltpu.SemaphoreType.DMA((2,))]


def _dispatch_body(idx_ref, idx_next_ref, src_ref, o_ref, buf_ref, sem):
    slot = _gather_step(idx_ref, idx_next_ref, src_ref, buf_ref, sem)
    for c in range(ROW_TILES):
        o_ref[:, c * LANES:(c + 1) * LANES] = buf_ref[slot, :, c, :].astype(BF16)


def _dispatch(src, idx):
    n = idx.shape[0]
    nsteps = n // GATHER_ROWS
    idx3 = idx.reshape(nsteps, 1, GATHER_ROWS)
    return pl.pallas_call(
        _dispatch_body,
        grid=(nsteps,),
        in_specs=_gather_specs(nsteps),
        out_specs=pl.BlockSpec((GATHER_ROWS, D_MODEL), lambda i: (i, 0)),
        out_shape=jax.ShapeDtypeStruct((n, D_MODEL), BF16),
        scratch_shapes=GATHER_SCRATCH,
        compiler_params=_cparams(("arbitrary",)),
        name="dispatch",
    )(idx3, idx3, src)


MOE_FC = 512
MOE_J = MOE_FF // MOE_FC
N_ITEMS = (16384 // MOE_BLOCK + N_EXPERTS) // ITEM_BLOCKS + (N_EXPERTS * (ITEM_BLOCKS - 1)) // ITEM_BLOCKS


def _experts_body(ie_ref, io_ref, ins_ref, ifl_ref, nr_ref, wg_ref, wu_ref, wd_ref, xs_ref, ys_ref,
                  yst_ref, xb_ref, acc_ref, sem_in, sem_out):
    i = pl.program_id(0)
    j = pl.program_id(1)
    n_items = pl.num_programs(0)
    nsub = ins_ref[i]
    nfill = ifl_ref[i]
    slot = i % 2
    blk = MOE_BLOCK

    def row0(item):
        return pl.multiple_of(io_ref[item] * blk, blk)

    def x_copy(item, sl, s):
        return pltpu.make_async_copy(xs_ref.at[pl.ds(row0(item) + s * blk, blk)], xb_ref.at[sl, s],
                                     sem_in.at[sl, s])

    def y_copy(item, s):
        return pltpu.make_async_copy(yst_ref.at[s], ys_ref.at[pl.ds(row0(item) + s * blk, blk)], sem_out.at[s])

    def for_blocks(count, fn):
        for s in range(ITEM_BLOCKS):
            @pl.when(s < count)
            def _():
                fn(s)

    @pl.when(j == 0)
    def _():
        @pl.when(i == 0)
        def _():
            for_blocks(nsub, lambda s: x_copy(0, 0, s).start())

        for_blocks(nsub, lambda s: x_copy(i, slot, s).wait())

    @pl.when((j == MOE_J - 1) & (i + 1 < n_items))
    def _():
        nxt = jnp.minimum(i + 1, n_items - 1)
        for_blocks(ins_ref[nxt], lambda s: x_copy(nxt, 1 - slot, s).start())

    for n in range(1, ITEM_BLOCKS + 1):
        @pl.when(nsub == n)
        def _():
            x = xb_ref[slot, 0:n].reshape(n * blk, D_MODEL)
            gate = _dot(x, wg_ref[...].astype(BF16))
            up = _dot(x, wu_ref[...].astype(BF16))
            hid = (_silu(gate) * up).astype(BF16)
            y = _dot(hid, wd_ref[...].astype(BF16)).reshape(n, blk, D_MODEL)

            @pl.when(j == 0)
            def _():
                acc_ref[0:n] = y

            @pl.when(j > 0)
            def _():
                acc_ref[0:n] = acc_ref[0:n] + y

    def wait_prev_y():
        @pl.when(i > 0)
        def _():
            prev = jnp.maximum(i - 1, 0)
            for_blocks(ins_ref[prev], lambda s: y_copy(prev, s).wait())

    @pl.when((nsub > 0) & (j == MOE_J - 1))
    def _():
        wait_prev_y()

        def emit(s):
            for c in range(ROW_TILES):
                yst_ref[s, :, c, :] = acc_ref[s, :, c * LANES:(c + 1) * LANES]
            y_copy(i, s).start()

        for_blocks(nsub, emit)

        @pl.when(i == n_items - 1)
        def _():
            for_blocks(nsub, lambda s: y_copy(i, s).wait())

    @pl.when((nsub == 0) & (j == 0))
    def _():
        wait_prev_y()

        @pl.when(nfill > 0)
        def _():
            yst_ref[...] = jnp.zeros_like(yst_ref)
            for_blocks(nfill, lambda s: y_copy(i, s).start())
            for_blocks(nfill, lambda s: y_copy(i, s).wait())


def _experts(x_sorted, w_gate, w_up, w_down, item_e, item_off, item_nsub, item_fill, n_real):
    n_rows = x_sorted.shape[0]
    d = D_MODEL

    def w_in_map(i, j, ie, io, ins, ifl, nr):
        return (ie[i], 0, jnp.where(i < nr[0], j, MOE_J - 1))

    def w_dn_map(i, j, ie, io, ins, ifl, nr):
        return (ie[i], jnp.where(i < nr[0], j, MOE_J - 1), 0)

    grid_spec = pltpu.PrefetchScalarGridSpec(
        num_scalar_prefetch=5,
        grid=(N_ITEMS, MOE_J),
        in_specs=[pl.BlockSpec((None, d, MOE_FC), w_in_map),
                  pl.BlockSpec((None, d, MOE_FC), w_in_map),
                  pl.BlockSpec((None, MOE_FC, d), w_dn_map),
                  pl.BlockSpec(memory_space=pl.ANY)],
        out_specs=pl.BlockSpec(memory_space=pl.ANY),
        scratch_shapes=[pltpu.VMEM((ITEM_BLOCKS, MOE_BLOCK, ROW_TILES, LANES), F32),
                        pltpu.VMEM((2, ITEM_BLOCKS, MOE_BLOCK, d), BF16),
                        pltpu.VMEM((ITEM_BLOCKS, MOE_BLOCK, d), F32),
                        pltpu.SemaphoreType.DMA((2, ITEM_BLOCKS)),
                        pltpu.SemaphoreType.DMA((ITEM_BLOCKS,))],
    )
    return pl.pallas_call(
        _experts_body,
        grid_spec=grid_spec,
        out_shape=jax.ShapeDtypeStruct((n_rows, ROW_TILES, LANES), F32),
        compiler_params=_cparams(("arbitrary", "arbitrary")),
        name="experts",
    )(item_e, item_off, item_nsub, item_fill, n_real, w_gate, w_up, w_down, x_sorted)


FINAL_TM = GATHER_ROWS // 2


def _final_body(idx_ref, idx_next_ref, ys_ref, w_ref, x1_ref, mod_ref, g_ref, o_ref, buf_ref, sem):
    slot = _gather_step(idx_ref, idx_next_ref, ys_ref, buf_ref, sem)
    tm = FINAL_TM
    w = w_ref[...]
    w0, w1 = w[:, 0:1], w[:, 1:2]
    ffn = jnp.concatenate([w0 * buf_ref[slot, 0:tm, c, :] + w1 * buf_ref[slot, tm:2 * tm, c, :]
                           for c in range(ROW_TILES)], axis=1)
    o_ref[...] = x1_ref[...] + mod_ref[5:6, :] * (_rms(ffn) * g_ref[...])


def _final(y_sorted, pos, wts, x1, mod8, g_post):
    t, d = x1.shape
    tm = FINAL_TM
    nt = t // tm
    idx3 = pos.reshape(nt, tm, 2).transpose(0, 2, 1).reshape(nt, 1, 2 * tm)
    return pl.pallas_call(
        _final_body,
        grid=(nt,),
        in_specs=_gather_specs(nt) + [pl.BlockSpec((tm, 128), lambda m: (m, 0)),
                                      pl.BlockSpec((tm, d), lambda m: (m, 0)),
                                      pl.BlockSpec((8, d), lambda m: (0, 0)),
                                      pl.BlockSpec((1, d), lambda m: (0, 0))],
        out_specs=pl.BlockSpec((tm, d), lambda m: (m, 0)),
        out_shape=jax.ShapeDtypeStruct((t, d), F32),
        scratch_shapes=GATHER_SCRATCH,
        compiler_params=_cparams(("arbitrary",)),
        name="final",
    )(idx3, idx3, y_sorted, wts, x1, mod8, g_post)


def _routing_tables(ids, n_tok):
    n_assign = n_tok * 2
    n_blocks = n_assign // MOE_BLOCK + N_EXPERTS
    flat_e = ids.reshape(n_assign)
    onehot = (flat_e[:, None] == jnp.arange(N_EXPERTS, dtype=I32)[None, :]).astype(I32)
    csum = jnp.cumsum(onehot, axis=0)
    rank = jnp.sum(csum * onehot, axis=1) - 1
    counts = csum[-1]
    nb = (counts + MOE_BLOCK - 1) // MOE_BLOCK
    blk_start = jnp.cumsum(nb) - nb
    dest = blk_start[flat_e] * MOE_BLOCK + rank
    n_rows = n_blocks * MOE_BLOCK
    row_tok = (jnp.arange(n_rows, dtype=I32) % n_tok).at[dest].set(jnp.arange(n_assign, dtype=I32) // 2)
    pos = dest.reshape(n_tok, 2)

    n_it = (nb + ITEM_BLOCKS - 1) // ITEM_BLOCKS
    it_end = jnp.cumsum(n_it)
    it_start = it_end - n_it
    n_real = it_end[-1]
    i = jnp.arange(N_ITEMS, dtype=I32)
    e_i = jnp.minimum(jnp.searchsorted(it_end, i, side='right').astype(I32), N_EXPERTS - 1)
    k_i = i - it_start[e_i]
    valid = i < n_real
    last_e = e_i[jnp.maximum(n_real - 1, 0)]
    item_e = jnp.where(valid, e_i, last_e).astype(I32)
    fill_off = jnp.sum(nb) + ITEM_BLOCKS * (i - n_real)
    item_fill = jnp.where(valid, 0, jnp.clip(n_blocks - fill_off, 0, ITEM_BLOCKS)).astype(I32)
    item_off = jnp.where(valid, blk_start[e_i] + ITEM_BLOCKS * k_i, jnp.minimum(fill_off, n_blocks - 1)).astype(I32)
    item_nsub = jnp.where(valid, jnp.clip(nb[e_i] - ITEM_BLOCKS * k_i, 0, ITEM_BLOCKS), 0).astype(I32)
    return row_tok, pos, item_e, item_off, item_nsub, item_fill, n_real.reshape(1).astype(I32)


def _pad_lanes(v, start, total=SMALL_COLS):
    return jnp.zeros((1, total), F32).at[0, start:start + v.shape[0]].set(v)


def _layer(x2, c, w_ada, b_ada, norm_pre_mix, norm_post_mix, norm_pre_ffn, norm_post_ffn,
           w_in, gla_w_gate_up, gla_b_gate, gla_norm, ssm_conv_w, ssm_conv_b, ssm_dt_bias,
           ssm_a_log, ssm_d, ssm_norm, w_branch_gla, w_branch_ssm, w_out,
           router_group, router_expert, moe_w_gate, moe_w_up, moe_w_down):
    t, d = x2.shape
    row = lambda v: v.reshape(1, -1)

    mod = _ada(c, w_ada, b_ada)
    mod8 = jnp.concatenate([mod.reshape(6, d), jnp.zeros((2, d), F32)], axis=0)

    o_glr, o_og, o_dt, o_gg = 4096, 4112, 16400, 16464
    w_big = jnp.concatenate([w_in[:, :o_glr], w_in[:, o_og:o_dt], w_in[:, o_gg:]], axis=1).astype(BF16)
    w_small = jnp.concatenate([w_in[:, o_glr:o_og], w_in[:, o_dt:o_gg],
                               jnp.zeros((d, SMALL_COLS - GLA_RANK - SSM_HEADS), F32)], axis=1).astype(BF16)

    p, small = _inproj(x2, mod8, row(norm_pre_mix), w_big, w_small)

    o_gla = _gla(p, small, gla_w_gate_up, row(gla_b_gate), row(gla_norm))
    y_ssm = _ssd(p, small, ssm_conv_w, row(ssm_conv_b),
                 _pad_lanes(ssm_dt_bias, SMALL_DT0), _pad_lanes(ssm_a_log, SMALL_DT0),
                 _pad_lanes(ssm_d, SMALL_DT0), row(ssm_norm))
    merged = _merge(o_gla, y_ssm, w_branch_gla.astype(BF16), w_branch_ssm.astype(BF16), p)

    w_router = jnp.concatenate([router_group, router_expert,
                                jnp.zeros((d, 128 - MOE_GROUPS - N_EXPERTS), F32)], axis=1)
    x1, h2f, logits = _outproj(merged, w_out.astype(BF16), x2, mod8,
                               row(norm_post_mix), row(norm_pre_ffn), w_router)
    ids, wts = _route(logits)

    row_tok, pos, item_e, item_off, item_nsub, item_fill, n_real = _routing_tables(ids[:, :2], t)
    x_sorted = _dispatch(h2f, row_tok)
    y_sorted = _experts(x_sorted, moe_w_gate, moe_w_up, moe_w_down,
                        item_e, item_off, item_nsub, item_fill, n_real)
    return _final(y_sorted, pos, wts, x1, mod8, row(norm_post_ffn))


def kernel(x, c, w_ada, b_ada, norm_pre_mix, norm_post_mix, norm_pre_ffn, norm_post_ffn, w_in, gla_w_gate_up, gla_b_gate, gla_norm, ssm_conv_w, ssm_conv_b, ssm_dt_bias, ssm_a_log, ssm_d, ssm_norm, w_branch_gla, w_branch_ssm, w_out, router_group, router_expert, moe_w_gate, moe_w_up, moe_w_down):
    bsz, seq, d = x.shape
    assert bsz == 1 and d == D_MODEL
    x2 = x.reshape(seq, d)
    params = (w_ada, b_ada, norm_pre_mix, norm_post_mix, norm_pre_ffn, norm_post_ffn, w_in, gla_w_gate_up,
              gla_b_gate, gla_norm, ssm_conv_w, ssm_conv_b, ssm_dt_bias, ssm_a_log, ssm_d, ssm_norm,
              w_branch_gla, w_branch_ssm, w_out, router_group, router_expert, moe_w_gate, moe_w_up, moe_w_down)
    for layer in range(w_ada.shape[0]):
        x2 = _layer(x2, c, *(prm[layer] for prm in params))
    return x2.reshape(bsz, seq, d)
```

```python
import functools

import jax
import jax.numpy as jnp
import numpy as np
from jax import lax
from jax.experimental import pallas as pl
from jax.experimental.pallas import tpu as pltpu

F32 = jnp.float32
BF16 = jnp.bfloat16
I32 = jnp.int32

D_MODEL = 2048
EPS = 1e-6
LANES = 128
ROW_TILES = D_MODEL // LANES

GLA_HEADS = 4
GLA_HEAD_K = 256
GLA_HEAD_V = 512
GLA_RANK = 16
GLA_NORMALIZER = 16.0
CHUNK = 64

SSM_GROUPS = 8
SSM_HEADS = 64
SSM_HEAD_DIM = 64
SSM_STATE = 128
SSM_CONV = 4
SSM_GROUP_W = 512
SSM_INNER = 4096

N_EXPERTS = 64
EXPERTS_PER_GROUP = 8
MOE_GROUPS = 8
MOE_FF = 1024
MOE_BLOCK = 128
ITEM_BLOCKS = 4

COL_Q, COL_K, COL_V, COL_OG, COL_Z, COL_XS, COL_B, COL_C, COL_GG, COL_GS = (
    0, 1024, 2048, 4096, 6144, 10240, 14336, 15360, 16384, 18432)
P_COLS = 20480
SMALL_COLS = 128
SMALL_DT0 = GLA_RANK

VMEM_LIMIT = 56 * 1024 * 1024


def _cparams(sem, vmem=VMEM_LIMIT):
    return pltpu.CompilerParams(dimension_semantics=sem, vmem_limit_bytes=vmem)


def _dot(a, b):
    return jnp.dot(a, b, preferred_element_type=F32)


def _dot_nt(a, b):
    return lax.dot_general(a, b, (((1,), (1,)), ((), ())), preferred_element_type=F32)


def _dot_tn(a, b):
    return lax.dot_general(a, b, (((0,), (0,)), ((), ())), preferred_element_type=F32)


def _split3(a):
    hi = a.astype(BF16)
    r1 = a - hi.astype(F32)
    mid = r1.astype(BF16)
    lo = (r1 - mid.astype(F32)).astype(BF16)
    return hi, mid, lo


def _dot_sel_r(a, sel):
    hi, mid, lo = _split3(a)
    return _dot(hi, sel) + _dot(mid, sel) + _dot(lo, sel)


def _dot_sel_l(sel, a):
    hi, mid, lo = _split3(a)
    return _dot(sel, hi) + _dot(sel, mid) + _dot(sel, lo)


def _sigmoid(x):
    return 1.0 / (1.0 + jnp.exp(-x))


def _silu(x):
    return x * _sigmoid(x)


def _softplus(x):
    return jnp.maximum(x, 0.0) + jnp.log1p(jnp.exp(-jnp.abs(x)))


def _log_sigmoid(x):
    return jnp.minimum(x, 0.0) - jnp.log1p(jnp.exp(-jnp.abs(x)))


def _rms(x):
    return x * lax.rsqrt(jnp.mean(x * x, axis=-1, keepdims=True) + EPS)


def _ada_body(c_ref, w_ref, b_ref, o_ref):
    c = c_ref[...]
    s = jnp.broadcast_to(_silu(c), (8, c.shape[1])).astype(BF16)
    o_ref[...] = _dot(s, w_ref[...].astype(BF16))[0:1] + b_ref[...]


def _ada(c, w_ada, b_ada):
    d, n = w_ada.shape
    tn = 1024
    return pl.pallas_call(
        _ada_body,
        grid=(n // tn,),
        in_specs=[pl.BlockSpec((1, d), lambda j: (0, 0)),
                  pl.BlockSpec((d, tn), lambda j: (0, j)),
                  pl.BlockSpec((1, tn), lambda j: (0, j))],
        out_specs=pl.BlockSpec((1, tn), lambda j: (0, j)),
        out_shape=jax.ShapeDtypeStruct((1, n), F32),
        compiler_params=_cparams(("parallel",)),
        name="ada",
    )(c, w_ada, b_ada.reshape(1, n))


def _inproj_body(x_ref, mod_ref, g_ref, w_ref, ws_ref, p_ref, s_ref, h_ref):
    @pl.when(pl.program_id(1) == 0)
    def _():
        h = _rms(x_ref[...]) * g_ref[...] * (1.0 + mod_ref[1:2, :]) + mod_ref[0:1, :]
        hb = h.astype(BF16)
        h_ref[...] = hb
        s_ref[...] = _dot(hb, ws_ref[...])

    p_ref[...] = _dot(h_ref[...], w_ref[...]).astype(BF16)


def _inproj(x2, mod8, gain, w_big, w_small):
    t, d = x2.shape
    n = w_big.shape[1]
    tm, tn = 1024, 1024
    return pl.pallas_call(
        _inproj_body,
        grid=(t // tm, n // tn),
        in_specs=[pl.BlockSpec((tm, d), lambda m, j: (m, 0)),
                  pl.BlockSpec((8, d), lambda m, j: (0, 0)),
                  pl.BlockSpec((1, d), lambda m, j: (0, 0)),
                  pl.BlockSpec((d, tn), lambda m, j: (0, j)),
                  pl.BlockSpec((d, SMALL_COLS), lambda m, j: (0, 0))],
        out_specs=[pl.BlockSpec((tm, tn), lambda m, j: (m, j)),
                   pl.BlockSpec((tm, SMALL_COLS), lambda m, j: (m, 0))],
        out_shape=[jax.ShapeDtypeStruct((t, n), BF16),
                   jax.ShapeDtypeStruct((t, SMALL_COLS), F32)],
        scratch_shapes=[pltpu.VMEM((tm, d), BF16)],
        compiler_params=_cparams(("parallel", "arbitrary")),
        name="inproj",
    )(x2, mod8, gain, w_big, w_small)


GLA_TB = 512


def _gla_body(q_ref, k_ref, v_ref, og_ref, sm_ref, wup_ref, bg_ref, gn_ref, o_ref, st_ref):
    @pl.when(pl.program_id(1) == 0)
    def _():
        st_ref[...] = jnp.zeros_like(st_ref)

    tb, C = GLA_TB, CHUNK
    nch = tb // C
    r = lax.broadcasted_iota(I32, (tb, tb), 0)
    c = lax.broadcasted_iota(I32, (tb, tb), 1)
    causal = (r // C == c // C) & (r >= c)
    r2 = lax.broadcasted_iota(I32, (2 * C, 2 * C), 0)
    c2 = lax.broadcasted_iota(I32, (2 * C, 2 * C), 1)
    tril2 = ((r2 // C == c2 // C) & (r2 >= c2)).astype(BF16)

    glr = sm_ref[:, 0:GLA_RANK].astype(BF16)
    pre = _dot(glr, wup_ref[...].astype(BF16)) + bg_ref[...]
    log_a = _log_sigmoid(pre) / GLA_NORMALIZER
    b = jnp.concatenate([_dot_sel_l(tril2, log_a[i * 2 * C:(i + 1) * 2 * C]) for i in range(nch // 2)],
                        axis=0)
    b_last = [b[(i + 1) * C - 1:(i + 1) * C, :] for i in range(nch)]
    b_end = jnp.concatenate([jnp.broadcast_to(bl, (C, bl.shape[1])) for bl in b_last], axis=0)

    q = q_ref[...].astype(F32) * (GLA_HEAD_K ** -0.5)
    k = k_ref[...].astype(F32)
    v = v_ref[...]
    q_dec = (q * jnp.exp(b)).astype(BF16)
    k_inv = (k * jnp.exp(-b)).astype(BF16)
    k_end = (k * jnp.exp(b_end - b)).astype(BF16)
    scores = jnp.where(causal, _dot_nt(q_dec, k_inv), 0.0)
    o_intra = _dot(scores.astype(BF16), v)

    st = st_ref[...]
    o_inter = []
    for i in range(nch):
        rows = slice(i * C, (i + 1) * C)
        o_inter.append(_dot_nt(q_dec[rows], st.astype(BF16)))
        st = st * jnp.exp(b_last[i]) + _dot_tn(v[rows], k_end[rows])
    st_ref[...] = st

    o = o_intra + jnp.concatenate(o_inter, axis=0)
    og = og_ref[...].astype(F32)
    o_ref[...] = (_rms(o) * gn_ref[...] * _silu(og)).astype(BF16)


def _gla(p, small, wup, bg, gn):
    t = p.shape[0]
    tb = GLA_TB
    kb, vb = GLA_HEAD_K, GLA_HEAD_V
    return pl.pallas_call(
        _gla_body,
        grid=(GLA_HEADS, t // tb),
        in_specs=[pl.BlockSpec((tb, kb), lambda h, i: (i, COL_Q // kb + h)),
                  pl.BlockSpec((tb, kb), lambda h, i: (i, COL_K // kb + h)),
                  pl.BlockSpec((tb, vb), lambda h, i: (i, COL_V // vb + h)),
                  pl.BlockSpec((tb, vb), lambda h, i: (i, COL_OG // vb + h)),
                  pl.BlockSpec((tb, SMALL_COLS), lambda h, i: (i, 0)),
                  pl.BlockSpec((GLA_RANK, kb), lambda h, i: (0, h)),
                  pl.BlockSpec((1, kb), lambda h, i: (0, h)),
                  pl.BlockSpec((1, vb), lambda h, i: (0, h))],
        out_specs=pl.BlockSpec((tb, vb), lambda h, i: (i, h)),
        out_shape=jax.ShapeDtypeStruct((t, GLA_HEADS * vb), BF16),
        scratch_shapes=[pltpu.VMEM((vb, kb), F32)],
        compiler_params=_cparams(("parallel", "arbitrary")),
        name="gla",
    )(p, p, p, p, small, wup, bg, gn)


SSD_TB = 512
HALO = 8


def _ssd_body(xs_ref, b_ref, c_ref, z_ref, sm_ref, cwx_ref, cwb_ref, cwc_ref, cbx_ref, cbb_ref, cbc_ref,
              dtb_ref, alog_ref, dsk_ref, ng_ref, o_ref,
              ex_ref, eb_ref, ec_ref, at_ref, ht_ref):
    g = pl.program_id(0)
    t = pl.program_id(1)
    tb = SSD_TB
    gw = SSM_GROUP_W

    @pl.when(t == 0)
    def _():
        ht_ref[...] = jnp.zeros_like(ht_ref)
        ex_ref[0:HALO, :] = jnp.zeros((HALO, gw), F32)
        eb_ref[0:HALO, :] = jnp.zeros((HALO, SSM_STATE), F32)
        ec_ref[0:HALO, :] = jnp.zeros((HALO, SSM_STATE), F32)

    @pl.when(t > 0)
    def _():
        ex_ref[0:HALO, :] = ex_ref[tb:tb + HALO, :]
        eb_ref[0:HALO, :] = eb_ref[tb:tb + HALO, :]
        ec_ref[0:HALO, :] = ec_ref[tb:tb + HALO, :]

    def conv_silu(u_ref, e_ref, w_ref, bias_ref):
        e_ref[HALO:HALO + tb, :] = u_ref[...].astype(F32)
        acc = bias_ref[...] + w_ref[0:1, :] * e_ref[HALO - 3:HALO - 3 + tb, :]
        for kk in range(1, SSM_CONV):
            acc = acc + w_ref[kk:kk + 1, :] * e_ref[HALO - 3 + kk:HALO - 3 + kk + tb, :]
        return _silu(acc)

    xa = conv_silu(xs_ref, ex_ref, cwx_ref, cbx_ref)
    ba = conv_silu(b_ref, eb_ref, cwb_ref, cbb_ref).astype(BF16)
    ca = conv_silu(c_ref, ec_ref, cwc_ref, cbc_ref).astype(BF16)

    L = CHUNK
    nch = tb // L
    rep = gw // L
    hpg = gw // SSM_HEAD_DIM
    head0 = pl.multiple_of(SMALL_DT0 + g * hpg, hpg)

    e_row = lax.broadcasted_iota(I32, (SMALL_COLS, gw), 0)
    e_col = lax.broadcasted_iota(I32, (SMALL_COLS, gw), 1)
    expand = (e_row == head0 + e_col // SSM_HEAD_DIM).astype(BF16)

    dt_small = _softplus(sm_ref[...] + dtb_ref[...])
    adt_small = dt_small * (-jnp.exp(alog_ref[...]))
    rb = lax.broadcasted_iota(I32, (tb, tb), 0)
    cb = lax.broadcasted_iota(I32, (tb, tb), 1)
    blocktril = ((rb // L == cb // L) & (rb >= cb)).astype(BF16)
    acum_small = _dot_sel_l(blocktril, adt_small)
    at_ref[...] = acum_small.T
    heads = at_ref[pl.ds(head0, hpg), :]
    dt_exp = _dot_sel_r(dt_small, expand)
    acum = _dot_sel_r(acum_small, expand)
    d_exp = _dot_sel_r(jnp.broadcast_to(dsk_ref[...], (8, SMALL_COLS)), expand)[0:1]
    xdt = xa * dt_exp
    xdt_b = xdt.astype(BF16)
    e_acum = jnp.exp(acum)

    li = lax.broadcasted_iota(I32, (L, gw), 0)
    lj = lax.broadcasted_iota(I32, (L, gw), 1) % L
    causal_t = li >= lj
    hi = lax.broadcasted_iota(I32, (hpg, gw), 0)
    hj = lax.broadcasted_iota(I32, (hpg, gw), 1) // L
    headmask = hi == hj
    ones_h = jnp.ones((L, hpg), BF16)
    bi = lax.broadcasted_iota(I32, (gw, gw), 0) // L
    bj = lax.broadcasted_iota(I32, (gw, gw), 1) // SSM_HEAD_DIM
    blockmask = bi == bj
    masked_out = -1e30

    ht = ht_ref[...]
    for c in range(nch):
        rows = slice(c * L, (c + 1) * L)
        acum_c = acum[rows]
        a_rows = jnp.concatenate([heads[:, c * L:(c + 1) * L]] * rep, axis=1)
        rterm = _dot_sel_l(ones_h, jnp.where(headmask, a_rows, 0.0))
        decay = jnp.exp(jnp.where(causal_t, acum_c - rterm, masked_out))
        cc = ca[rows]
        bc = ba[rows]
        cb_t = _dot_nt(cc, jnp.concatenate([bc] * rep, axis=0))
        m = (cb_t * decay).astype(BF16)
        bd = jnp.where(blockmask, jnp.concatenate([xdt_b[rows]] * rep, axis=0), jnp.zeros((), BF16))
        y_diag = _dot(m, bd)
        y_off = _dot(cc, ht.astype(BF16)) * e_acum[rows]
        a_last = acum_c[L - 1:L, :]
        xd = (xdt[rows] * jnp.exp(a_last - acum_c)).astype(BF16)
        ht = ht * jnp.exp(a_last) + _dot_tn(bc, xd)
        y = y_diag + y_off + d_exp * xa[rows]
        y = y * _silu(z_ref[rows, :].astype(F32))
        o_ref[rows, :] = (_rms(y) * ng_ref[...]).astype(BF16)
    ht_ref[...] = ht


def _ssd(p, small, conv_w, conv_b, dt_bias_s, a_log_s, d_skip_s, norm_g):
    t = p.shape[0]
    tb = SSD_TB
    gw, ns = SSM_GROUP_W, SSM_STATE
    xs0, b0, c0 = 0, SSM_INNER // ns, (SSM_INNER + SSM_GROUPS * ns) // ns
    row = lambda w, off: pl.BlockSpec((1, w), lambda g, i: (0, off + g))
    return pl.pallas_call(
        _ssd_body,
        grid=(SSM_GROUPS, t // tb),
        in_specs=[pl.BlockSpec((tb, gw), lambda g, i: (i, COL_XS // gw + g)),
                  pl.BlockSpec((tb, ns), lambda g, i: (i, COL_B // ns + g)),
                  pl.BlockSpec((tb, ns), lambda g, i: (i, COL_C // ns + g)),
                  pl.BlockSpec((tb, gw), lambda g, i: (i, COL_Z // gw + g)),
                  pl.BlockSpec((tb, SMALL_COLS), lambda g, i: (i, 0)),
                  pl.BlockSpec((SSM_CONV, gw), lambda g, i: (0, xs0 + g)),
                  pl.BlockSpec((SSM_CONV, ns), lambda g, i: (0, b0 + g)),
                  pl.BlockSpec((SSM_CONV, ns), lambda g, i: (0, c0 + g)),
                  row(gw, xs0), row(ns, b0), row(ns, c0),
                  pl.BlockSpec((1, SMALL_COLS), lambda g, i: (0, 0)),
                  pl.BlockSpec((1, SMALL_COLS), lambda g, i: (0, 0)),
                  pl.BlockSpec((1, SMALL_COLS), lambda g, i: (0, 0)),
                  pl.BlockSpec((1, gw), lambda g, i: (0, g))],
        out_specs=pl.BlockSpec((tb, gw), lambda g, i: (i, g)),
        out_shape=jax.ShapeDtypeStruct((t, SSM_INNER), BF16),
        scratch_shapes=[pltpu.VMEM((tb + HALO, gw), F32),
                        pltpu.VMEM((tb + HALO, ns), F32),
                        pltpu.VMEM((tb + HALO, ns), F32),
                        pltpu.VMEM((SMALL_COLS, tb), F32),
                        pltpu.VMEM((ns, gw), F32)],
        compiler_params=_cparams(("parallel", "arbitrary")),
        name="ssd",
    )(p, p, p, p, small, conv_w, conv_w, conv_w, conv_b, conv_b, conv_b,
      dt_bias_s, a_log_s, d_skip_s, norm_g)


def _merge_body(a1_ref, a2_ref, w1_ref, w2_ref, gg_ref, gs_ref, o_ref):
    y1 = _dot(a1_ref[...], w1_ref[...])
    y2 = _dot(a2_ref[...], w2_ref[...])
    o_ref[...] = (_sigmoid(gg_ref[...].astype(F32)) * y1 + _sigmoid(gs_ref[...].astype(F32)) * y2).astype(BF16)


def _merge(o_gla, y_ssm, w1, w2, p):
    t, k1 = o_gla.shape
    k2 = y_ssm.shape[1]
    n = w1.shape[1]
    tm, tn = 512, 512
    return pl.pallas_call(
        _merge_body,
        grid=(t // tm, n // tn),
        in_specs=[pl.BlockSpec((tm, k1), lambda m, j: (m, 0)),
                  pl.BlockSpec((tm, k2), lambda m, j: (m, 0)),
                  pl.BlockSpec((k1, tn), lambda m, j: (0, j)),
                  pl.BlockSpec((k2, tn), lambda m, j: (0, j)),
                  pl.BlockSpec((tm, tn), lambda m, j: (m, COL_GG // tn + j)),
                  pl.BlockSpec((tm, tn), lambda m, j: (m, COL_GS // tn + j))],
        out_specs=pl.BlockSpec((tm, tn), lambda m, j: (m, j)),
        out_shape=jax.ShapeDtypeStruct((t, n), BF16),
        compiler_params=_cparams(("parallel", "arbitrary")),
        name="merge",
    )(o_gla, y_ssm, w1, w2, p, p)


def _outproj_body(m_ref, w_ref, x_ref, mod_ref, gpost_ref, gpre_ref, wr_ref, x1_ref, hf_ref, lg_ref):
    mix = _dot(m_ref[...], w_ref[...])
    x1 = x_ref[...] + mod_ref[2:3, :] * (_rms(mix) * gpost_ref[...])
    x1_ref[...] = x1
    h = _rms(x1) * gpre_ref[...] * (1.0 + mod_ref[4:5, :]) + mod_ref[3:4, :]
    for s in range(ROW_TILES):
        hf_ref[:, s, :] = h[:, s * LANES:(s + 1) * LANES]
    h_hi = h.astype(BF16)
    h_lo = (h - h_hi.astype(F32)).astype(BF16)
    wr = wr_ref[...]
    w_hi = wr.astype(BF16)
    w_lo = (wr - w_hi.astype(F32)).astype(BF16)
    lg_ref[...] = _dot(h_hi, w_hi) + _dot(h_hi, w_lo) + _dot(h_lo, w_hi)


def _outproj(merged, w_out, x2, mod8, g_post, g_pre, w_router):
    t, d = x2.shape
    tm = 256
    full = lambda r, c: pl.BlockSpec((r, c), lambda m: (0, 0))
    tile = lambda c: pl.BlockSpec((tm, c), lambda m: (m, 0))
    return pl.pallas_call(
        _outproj_body,
        grid=(t // tm,),
        in_specs=[tile(d), full(d, d), tile(d), full(8, d), full(1, d), full(1, d), full(d, 128)],
        out_specs=[tile(d), pl.BlockSpec((tm, ROW_TILES, LANES), lambda m: (m, 0, 0)), tile(128)],
        out_shape=[jax.ShapeDtypeStruct((t, d), F32),
                   jax.ShapeDtypeStruct((t, ROW_TILES, LANES), F32),
                   jax.ShapeDtypeStruct((t, 128), F32)],
        compiler_params=_cparams(("parallel",)),
        name="outproj",
    )(merged, w_out, x2, mod8, g_post, g_pre, w_router)


def _route_body(lg_ref, id_ref, w_ref):
    lg = lg_ref[...]
    lane = lax.broadcasted_iota(I32, lg.shape, 1)
    lane_f = lane.astype(F32)
    neg = jnp.float32(-jnp.inf)

    def first_argmax(vals, mx):
        return jnp.min(jnp.where(vals == mx, lane_f, 1e9), axis=-1, keepdims=True).astype(I32)

    gmask = lane < MOE_GROUPS
    gl = jnp.where(gmask, lg, neg)
    gmax = jnp.max(gl, axis=-1, keepdims=True)
    gsum = jnp.sum(jnp.where(gmask, jnp.exp(gl - gmax), 0.0), axis=-1, keepdims=True)
    g_w = 1.0 / gsum
    g_idx = first_argmax(gl, gmax)
    lo = MOE_GROUPS + g_idx * EXPERTS_PER_GROUP
    emask = (lane >= lo) & (lane < lo + EXPERTS_PER_GROUP)
    el = jnp.where(emask, lg, neg)
    m1 = jnp.max(el, axis=-1, keepdims=True)
    i1 = first_argmax(el, m1)
    el2 = jnp.where(lane == i1, neg, el)
    m2 = jnp.max(el2, axis=-1, keepdims=True)
    i2 = first_argmax(el2, m2)
    r = jnp.exp(m2 - m1)
    w1 = g_w / (1.0 + r)
    w2 = g_w * r / (1.0 + r)
    id_ref[...] = jnp.where(lane == 0, i1 - MOE_GROUPS, jnp.where(lane == 1, i2 - MOE_GROUPS, 0))
    w_ref[...] = jnp.where(lane == 0, w1, jnp.where(lane == 1, w2, 0.0))


def _route(logits):
    t = logits.shape[0]
    tm = 1024
    spec = pl.BlockSpec((tm, 128), lambda m: (m, 0))
    return pl.pallas_call(
        _route_body,
        grid=(t // tm,),
        in_specs=[spec],
        out_specs=[spec, spec],
        out_shape=[jax.ShapeDtypeStruct((t, 128), I32), jax.ShapeDtypeStruct((t, 128), F32)],
        compiler_params=_cparams(("parallel",)),
        name="route",
    )(logits)


GATHER_ROWS = 512


def _issue_rows(idx_ref, src_ref, buf_ref, slot, sem, skip_pads):
    def body(r, carry):
        tok = idx_ref[0, 0, r]
        copy = pltpu.make_async_copy(src_ref.at[jnp.maximum(tok, 0)], buf_ref.at[slot, :, r, :], sem.at[slot])
        if skip_pads:
            @pl.when(tok >= 0)
            def _():
                copy.start()
        else:
            copy.start()
        return carry

    lax.fori_loop(0, GATHER_ROWS, body, 0, unroll=8)


def _wait_rows(src_ref, buf_ref, slot, sem, count):
    del src_ref
    if count is None:
        pltpu.make_async_copy(buf_ref.at[slot], buf_ref.at[slot], sem.at[slot]).wait()
    else:
        @pl.when(count > 0)
        def _():
            part = buf_ref.at[slot, :, pl.ds(0, count), :]
            pltpu.make_async_copy(part, part, sem.at[slot]).wait()


def _gather_step(idx_ref, idx_next_ref, src_ref, buf_ref, sem, cnt_ref=None):
    i = pl.program_id(0)
    slot = i % 2
    skip_pads = cnt_ref is not None

    @pl.when(i == 0)
    def _():
        if skip_pads:
            buf_ref[...] = jnp.zeros_like(buf_ref)
        _issue_rows(idx_ref, src_ref, buf_ref, 0, sem, skip_pads)

    @pl.when(i + 1 < pl.num_programs(0))
    def _():
        _issue_rows(idx_next_ref, src_ref, buf_ref, 1 - slot, sem, skip_pads)

    _wait_rows(src_ref, buf_ref, slot, sem, cnt_ref[i] if skip_pads else None)
    return slot


def _gather_specs(nsteps):
    smem = lambda f: pl.BlockSpec((1, 1, GATHER_ROWS), f, memory_space=pltpu.SMEM)
    return [smem(lambda i: (i, 0, 0)),
            smem(lambda i: (jnp.minimum(i + 1, nsteps - 1), 0, 0)),
            pl.BlockSpec(memory_space=pl.ANY)]


GATHER_SCRATCH = [pltpu.VMEM((2, ROW_TILES, GATHER_ROWS, LANES), F32), pltpu.SemaphoreType.DMA((2,))]


def _dispatch_body(idx_ref, idx_next_ref, src_ref, cnt_ref, o_ref, buf_ref, sem):
    slot = _gather_step(idx_ref, idx_next_ref, src_ref, buf_ref, sem, cnt_ref)
    for c in range(ROW_TILES):
        o_ref[:, c * LANES:(c + 1) * LANES] = buf_ref[slot, c].astype(BF16)


def _dispatch(src, idx):
    n = idx.shape[0]
    nsteps = n // GATHER_ROWS
    idx3 = idx.reshape(nsteps, 1, GATHER_ROWS)
    counts = jnp.sum((idx3[:, 0, :] >= 0).astype(I32), axis=1)
    return pl.pallas_call(
        _dispatch_body,
        grid=(nsteps,),
        in_specs=_gather_specs(nsteps) + [pl.BlockSpec(memory_space=pltpu.SMEM)],
        out_specs=pl.BlockSpec((GATHER_ROWS, D_MODEL), lambda i: (i, 0)),
        out_shape=jax.ShapeDtypeStruct((n, D_MODEL), BF16),
        scratch_shapes=GATHER_SCRATCH,
        compiler_params=_cparams(("arbitrary",)),
        name="dispatch",
    )(idx3, idx3, src, counts)


MOE_FC = 512
MOE_J = MOE_FF // MOE_FC
N_ITEMS = (16384 // MOE_BLOCK + N_EXPERTS) // ITEM_BLOCKS + (N_EXPERTS * (ITEM_BLOCKS - 1)) // ITEM_BLOCKS


def _experts_body(ie_ref, io_ref, ins_ref, ifl_ref, nr_ref, wg_ref, wu_ref, wd_ref, xs_ref, ys_ref,
                  xb_ref, acc_ref, sem_in, sem_out):
    i = pl.program_id(0)
    j = pl.program_id(1)
    n_items = pl.num_programs(0)
    nsub = ins_ref[i]
    nfill = ifl_ref[i]
    slot = i % 2
    blk = MOE_BLOCK

    def row0(item):
        return pl.multiple_of(io_ref[item] * blk, blk)

    def x_copy(item, sl, s):
        return pltpu.make_async_copy(xs_ref.at[pl.ds(row0(item) + s * blk, blk)], xb_ref.at[sl, s],
                                     sem_in.at[sl, s])

    def start_y(item, s):
        for c in range(ROW_TILES):
            pltpu.make_async_copy(acc_ref.at[item % 2, s, :, pl.ds(c * LANES, LANES)],
                                  ys_ref.at[pl.ds(row0(item) + s * blk, blk), c, :], sem_out.at[s]).start()

    def wait_y(s):
        pltpu.make_async_copy(acc_ref.at[0, s], acc_ref.at[0, s], sem_out.at[s]).wait()

    def for_blocks(count, fn):
        for s in range(ITEM_BLOCKS):
            @pl.when(s < count)
            def _():
                fn(s)

    @pl.when(j == 0)
    def _():
        @pl.when(i == 0)
        def _():
            for_blocks(nsub, lambda s: x_copy(0, 0, s).start())

        for_blocks(nsub, lambda s: x_copy(i, slot, s).wait())

    @pl.when((j == MOE_J - 1) & (i + 1 < n_items))
    def _():
        nxt = jnp.minimum(i + 1, n_items - 1)
        for_blocks(ins_ref[nxt], lambda s: x_copy(nxt, 1 - slot, s).start())

    for n in range(1, ITEM_BLOCKS + 1):
        @pl.when(nsub == n)
        def _():
            x = xb_ref[slot, 0:n].reshape(n * blk, D_MODEL)
            gate = _dot(x, wg_ref[...].astype(BF16))
            up = _dot(x, wu_ref[...].astype(BF16))
            hid = (_silu(gate) * up).astype(BF16)
            y = _dot(hid, wd_ref[...].astype(BF16)).reshape(n, blk, D_MODEL)

            @pl.when(j == 0)
            def _():
                acc_ref[slot, 0:n] = y

            @pl.when(j > 0)
            def _():
                acc_ref[slot, 0:n] = acc_ref[slot, 0:n] + y

    def wait_prev_y():
        @pl.when(i > 0)
        def _():
            for_blocks(ins_ref[jnp.maximum(i - 1, 0)], wait_y)

    @pl.when((nsub > 0) & (j == MOE_J - 1))
    def _():
        wait_prev_y()
        for_blocks(nsub, lambda s: start_y(i, s))

        @pl.when(i == n_items - 1)
        def _():
            for_blocks(nsub, wait_y)

    @pl.when((nsub == 0) & (j == 0))
    def _():
        wait_prev_y()

        @pl.when(nfill > 0)
        def _():
            acc_ref[slot] = jnp.zeros(acc_ref.shape[1:], F32)
            for_blocks(nfill, lambda s: start_y(i, s))
            for_blocks(nfill, wait_y)


def _experts(x_sorted, w_gate, w_up, w_down, item_e, item_off, item_nsub, item_fill, n_real):
    n_rows = x_sorted.shape[0]
    d = D_MODEL

    def w_in_map(i, j, ie, io, ins, ifl, nr):
        return (ie[i], 0, jnp.where(i < nr[0], j, MOE_J - 1))

    def w_dn_map(i, j, ie, io, ins, ifl, nr):
        return (ie[i], jnp.where(i < nr[0], j, MOE_J - 1), 0)

    grid_spec = pltpu.PrefetchScalarGridSpec(
        num_scalar_prefetch=5,
        grid=(N_ITEMS, MOE_J),
        in_specs=[pl.BlockSpec((None, d, MOE_FC), w_in_map),
                  pl.BlockSpec((None, d, MOE_FC), w_in_map),
                  pl.BlockSpec((None, MOE_FC, d), w_dn_map),
                  pl.BlockSpec(memory_space=pl.ANY)],
        out_specs=pl.BlockSpec(memory_space=pl.ANY),
        scratch_shapes=[pltpu.VMEM((2, ITEM_BLOCKS, MOE_BLOCK, d), BF16),
                        pltpu.VMEM((2, ITEM_BLOCKS, MOE_BLOCK, d), F32),
                        pltpu.SemaphoreType.DMA((2, ITEM_BLOCKS)),
                        pltpu.SemaphoreType.DMA((ITEM_BLOCKS,))],
    )
    return pl.pallas_call(
        _experts_body,
        grid_spec=grid_spec,
        out_shape=jax.ShapeDtypeStruct((n_rows, ROW_TILES, LANES), F32),
        compiler_params=_cparams(("arbitrary", "arbitrary")),
        name="experts",
    )(item_e, item_off, item_nsub, item_fill, n_real, w_gate, w_up, w_down, x_sorted)


FINAL_TM = GATHER_ROWS // 2


def _final_body(idx_ref, idx_next_ref, ys_ref, w_ref, x1_ref, mod_ref, g_ref, o_ref, buf_ref, sem):
    slot = _gather_step(idx_ref, idx_next_ref, ys_ref, buf_ref, sem)
    tm = FINAL_TM
    w = w_ref[...]
    w0, w1 = w[:, 0:1], w[:, 1:2]
    ffn = jnp.concatenate([w0 * buf_ref[slot, c, 0:tm, :] + w1 * buf_ref[slot, c, tm:2 * tm, :]
                           for c in range(ROW_TILES)], axis=1)
    o_ref[...] = x1_ref[...] + mod_ref[5:6, :] * (_rms(ffn) * g_ref[...])


def _final(y_sorted, pos, wts, x1, mod8, g_post):
    t, d = x1.shape
    tm = FINAL_TM
    nt = t // tm
    idx3 = pos.reshape(nt, tm, 2).transpose(0, 2, 1).reshape(nt, 1, 2 * tm)
    return pl.pallas_call(
        _final_body,
        grid=(nt,),
        in_specs=_gather_specs(nt) + [pl.BlockSpec((tm, 128), lambda m: (m, 0)),
                                      pl.BlockSpec((tm, d), lambda m: (m, 0)),
                                      pl.BlockSpec((8, d), lambda m: (0, 0)),
                                      pl.BlockSpec((1, d), lambda m: (0, 0))],
        out_specs=pl.BlockSpec((tm, d), lambda m: (m, 0)),
        out_shape=jax.ShapeDtypeStruct((t, d), F32),
        scratch_shapes=GATHER_SCRATCH,
        compiler_params=_cparams(("arbitrary",)),
        name="final",
    )(idx3, idx3, y_sorted, wts, x1, mod8, g_post)


PLAN_TT = 512


def _plan_body(ids_ref, tril_ref, rank_ref, cnt_ref, run_ref):
    @pl.when(pl.program_id(0) == 0)
    def _():
        run_ref[...] = jnp.zeros_like(run_ref)

    ids = ids_ref[...]
    lane = lax.broadcasted_iota(I32, ids.shape, 1)
    oh0 = lane == ids[:, 0:1]
    oh1 = lane == ids[:, 1:2]
    both = jnp.where(oh0 | oh1, 1.0, 0.0).astype(BF16)
    base = _dot(tril_ref[...], both) + run_ref[0:1, :]
    r0 = jnp.sum(jnp.where(oh0, base, 0.0), axis=-1, keepdims=True)
    r1 = jnp.sum(jnp.where(oh1, base, 0.0), axis=-1, keepdims=True)
    rank_ref[...] = jnp.where(lane == 0, r0, jnp.where(lane == 1, r1, 0.0)).astype(I32)
    run = run_ref[...] + _dot(jnp.ones((8, PLAN_TT), BF16), both)
    run_ref[...] = run
    cnt_ref[...] = run.astype(I32)


def _plan(ids):
    t = ids.shape[0]
    r = np.arange(PLAN_TT)
    strict_tril = jnp.asarray(r[:, None] > r[None, :], BF16)
    tile = pl.BlockSpec((PLAN_TT, 128), lambda i: (i, 0))
    return pl.pallas_call(
        _plan_body,
        grid=(t // PLAN_TT,),
        in_specs=[tile, pl.BlockSpec((PLAN_TT, PLAN_TT), lambda i: (0, 0))],
        out_specs=[tile, pl.BlockSpec((8, 128), lambda i: (0, 0))],
        out_shape=[jax.ShapeDtypeStruct((t, 128), I32), jax.ShapeDtypeStruct((8, 128), I32)],
        scratch_shapes=[pltpu.VMEM((8, 128), F32)],
        compiler_params=_cparams(("arbitrary",)),
        name="plan",
    )(ids, strict_tril)


def _routing_tables(ids128, n_tok):
    n_assign = n_tok * 2
    n_blocks = n_assign // MOE_BLOCK + N_EXPERTS
    rank128, cnt = _plan(ids128)
    flat_e = ids128[:, :2].reshape(n_assign)
    rank = rank128[:, :2].reshape(n_assign)
    counts = cnt[0, :N_EXPERTS]
    nb = (counts + MOE_BLOCK - 1) // MOE_BLOCK
    blk_start = jnp.cumsum(nb) - nb
    dest = blk_start[flat_e] * MOE_BLOCK + rank
    n_rows = n_blocks * MOE_BLOCK
    row_tok = jnp.full((n_rows,), -1, I32).at[dest].set(jnp.arange(n_assign, dtype=I32) // 2)
    pos = dest.reshape(n_tok, 2)

    n_it = (nb + ITEM_BLOCKS - 1) // ITEM_BLOCKS
    it_end = jnp.cumsum(n_it)
    it_start = it_end - n_it
    n_real = it_end[-1]
    i = jnp.arange(N_ITEMS, dtype=I32)
    e_i = jnp.minimum(jnp.searchsorted(it_end, i, side='right').astype(I32), N_EXPERTS - 1)
    k_i = i - it_start[e_i]
    valid = i < n_real
    last_e = e_i[jnp.maximum(n_real - 1, 0)]
    item_e = jnp.where(valid, e_i, last_e).astype(I32)
    fill_off = jnp.sum(nb) + ITEM_BLOCKS * (i - n_real)
    item_fill = jnp.where(valid, 0, jnp.clip(n_blocks - fill_off, 0, ITEM_BLOCKS)).astype(I32)
    item_off = jnp.where(valid, blk_start[e_i] + ITEM_BLOCKS * k_i, jnp.minimum(fill_off, n_blocks - 1)).astype(I32)
    item_nsub = jnp.where(valid, jnp.clip(nb[e_i] - ITEM_BLOCKS * k_i, 0, ITEM_BLOCKS), 0).astype(I32)
    return row_tok, pos, item_e, item_off, item_nsub, item_fill, n_real.reshape(1).astype(I32)


def _pad_lanes(v, start, total=SMALL_COLS):
    return jnp.zeros((1, total), F32).at[0, start:start + v.shape[0]].set(v)


def _layer(x2, c, w_ada, b_ada, norm_pre_mix, norm_post_mix, norm_pre_ffn, norm_post_ffn,
           w_in, gla_w_gate_up, gla_b_gate, gla_norm, ssm_conv_w, ssm_conv_b, ssm_dt_bias,
           ssm_a_log, ssm_d, ssm_norm, w_branch_gla, w_branch_ssm, w_out,
           router_group, router_expert, moe_w_gate, moe_w_up, moe_w_down):
    t, d = x2.shape
    row = lambda v: v.reshape(1, -1)

    mod = _ada(c, w_ada, b_ada)
    mod8 = jnp.concatenate([mod.reshape(6, d), jnp.zeros((2, d), F32)], axis=0)

    o_glr, o_og, o_dt, o_gg = 4096, 4112, 16400, 16464
    w_big = jnp.concatenate([w_in[:, :o_glr], w_in[:, o_og:o_dt], w_in[:, o_gg:]], axis=1).astype(BF16)
    w_small = jnp.concatenate([w_in[:, o_glr:o_og], w_in[:, o_dt:o_gg],
                               jnp.zeros((d, SMALL_COLS - GLA_RANK - SSM_HEADS), F32)], axis=1).astype(BF16)

    p, small = _inproj(x2, mod8, row(norm_pre_mix), w_big, w_small)

    o_gla = _gla(p, small, gla_w_gate_up, row(gla_b_gate), row(gla_norm))
    y_ssm = _ssd(p, small, ssm_conv_w, row(ssm_conv_b),
                 _pad_lanes(ssm_dt_bias, SMALL_DT0), _pad_lanes(ssm_a_log, SMALL_DT0),
                 _pad_lanes(ssm_d, SMALL_DT0), row(ssm_norm))
    merged = _merge(o_gla, y_ssm, w_branch_gla.astype(BF16), w_branch_ssm.astype(BF16), p)

    w_router = jnp.concatenate([router_group, router_expert,
                                jnp.zeros((d, 128 - MOE_GROUPS - N_EXPERTS), F32)], axis=1)
    x1, h2f, logits = _outproj(merged, w_out.astype(BF16), x2, mod8,
                               row(norm_post_mix), row(norm_pre_ffn), w_router)
    ids, wts = _route(logits)

    row_tok, pos, item_e, item_off, item_nsub, item_fill, n_real = _routing_tables(ids, t)
    x_sorted = _dispatch(h2f, row_tok)
    y_sorted = _experts(x_sorted, moe_w_gate, moe_w_up, moe_w_down,
                        item_e, item_off, item_nsub, item_fill, n_real)
    return _final(y_sorted, pos, wts, x1, mod8, row(norm_post_ffn))


def kernel(x, c, w_ada, b_ada, norm_pre_mix, norm_post_mix, norm_pre_ffn, norm_post_ffn, w_in, gla_w_gate_up, gla_b_gate, gla_norm, ssm_conv_w, ssm_conv_b, ssm_dt_bias, ssm_a_log, ssm_d, ssm_norm, w_branch_gla, w_branch_ssm, w_out, router_group, router_expert, moe_w_gate, moe_w_up, moe_w_down):
    bsz, seq, d = x.shape
    assert bsz == 1 and d == D_MODEL
    x2 = x.reshape(seq, d)
    params = (w_ada, b_ada, norm_pre_mix, norm_post_mix, norm_pre_ffn, norm_post_ffn, w_in, gla_w_gate_up,
              gla_b_gate, gla_norm, ssm_conv_w, ssm_conv_b, ssm_dt_bias, ssm_a_log, ssm_d, ssm_norm,
              w_branch_gla, w_branch_ssm, w_out, router_group, router_expert, moe_w_gate, moe_w_up, moe_w_down)
    for layer in range(w_ada.shape[0]):
        x2 = _layer(x2, c, *(prm[layer] for prm in params))
    return x2.reshape(bsz, seq, d)
```

```python
import functools

import jax
import jax.numpy as jnp
import numpy as np
from jax import lax
from jax.experimental import pallas as pl
from jax.experimental.pallas import tpu as pltpu

F32 = jnp.float32
BF16 = jnp.bfloat16
I32 = jnp.int32

D_MODEL = 2048
EPS = 1e-6
LANES = 128
ROW_TILES = D_MODEL // LANES

GLA_HEADS = 4
GLA_HEAD_K = 256
GLA_HEAD_V = 512
GLA_RANK = 16
GLA_NORMALIZER = 16.0
CHUNK = 64

SSM_GROUPS = 8
SSM_HEADS = 64
SSM_HEAD_DIM = 64
SSM_STATE = 128
SSM_CONV = 4
SSM_GROUP_W = 512
SSM_INNER = 4096

N_EXPERTS = 64
EXPERTS_PER_GROUP = 8
MOE_GROUPS = 8
MOE_FF = 1024
MOE_BLOCK = 128
ITEM_BLOCKS = 4

COL_Q, COL_K, COL_V, COL_OG, COL_Z, COL_XS, COL_B, COL_C, COL_GG, COL_GS = (
    0, 1024, 2048, 4096, 6144, 10240, 14336, 15360, 16384, 18432)
P_COLS = 20480
SMALL_COLS = 128
SMALL_DT0 = GLA_RANK

VMEM_LIMIT = 56 * 1024 * 1024


def _cparams(sem, vmem=VMEM_LIMIT):
    return pltpu.CompilerParams(dimension_semantics=sem, vmem_limit_bytes=vmem)


def _dot(a, b):
    return jnp.dot(a, b, preferred_element_type=F32)


def _dot_nt(a, b):
    return lax.dot_general(a, b, (((1,), (1,)), ((), ())), preferred_element_type=F32)


def _dot_tn(a, b):
    return lax.dot_general(a, b, (((0,), (0,)), ((), ())), preferred_element_type=F32)


def _split3(a):
    hi = a.astype(BF16)
    r1 = a - hi.astype(F32)
    mid = r1.astype(BF16)
    lo = (r1 - mid.astype(F32)).astype(BF16)
    return hi, mid, lo


def _dot_sel_r(a, sel):
    hi, mid, lo = _split3(a)
    return _dot(hi, sel) + _dot(mid, sel) + _dot(lo, sel)


def _dot_sel_l(sel, a):
    hi, mid, lo = _split3(a)
    return _dot(sel, hi) + _dot(sel, mid) + _dot(sel, lo)


def _sigmoid(x):
    return 1.0 / (1.0 + jnp.exp(-x))


def _silu(x):
    return x * _sigmoid(x)


def _softplus(x):
    return jnp.maximum(x, 0.0) + jnp.log1p(jnp.exp(-jnp.abs(x)))


def _log_sigmoid(x):
    return jnp.minimum(x, 0.0) - jnp.log1p(jnp.exp(-jnp.abs(x)))


def _rms(x):
    return x * lax.rsqrt(jnp.mean(x * x, axis=-1, keepdims=True) + EPS)


def _ada_body(c_ref, w_ref, b_ref, o_ref):
    c = c_ref[...]
    s = jnp.broadcast_to(_silu(c), (8, c.shape[1])).astype(BF16)
    o_ref[...] = _dot(s, w_ref[...].astype(BF16))[0:1] + b_ref[...]


def _ada(c, w_ada, b_ada):
    d, n = w_ada.shape
    tn = 1024
    return pl.pallas_call(
        _ada_body,
        grid=(n // tn,),
        in_specs=[pl.BlockSpec((1, d), lambda j: (0, 0)),
                  pl.BlockSpec((d, tn), lambda j: (0, j)),
                  pl.BlockSpec((1, tn), lambda j: (0, j))],
        out_specs=pl.BlockSpec((1, tn), lambda j: (0, j)),
        out_shape=jax.ShapeDtypeStruct((1, n), F32),
        compiler_params=_cparams(("parallel",)),
        name="ada",
    )(c, w_ada, b_ada.reshape(1, n))


W_GLR, W_MID, W_DT, W_GATES, W_END = 4096, 4112, 16400, 16464, 20560
REPACK_CHUNK = 2048


def _repack_body(w_ref, big_ref, small_ref):
    def shifted(src0, dst0, width):
        base = src0 // LANES * LANES
        off = src0 - base
        for k in range(0, width, REPACK_CHUNK):
            n = min(REPACK_CHUNK, width - k)
            stop = min(base + k + n + LANES, W_END)
            win = w_ref[:, base + k:stop]
            big_ref[:, dst0 + k:dst0 + k + n] = win[:, off:off + n].astype(BF16)

    big_ref[:, 0:W_GLR] = w_ref[:, 0:W_GLR].astype(BF16)
    shifted(W_MID, COL_OG, W_DT - W_MID)
    shifted(W_GATES, COL_GG, W_END - W_GATES)
    lane = lax.broadcasted_iota(I32, small_ref.shape, 1)
    glr = w_ref[:, W_GLR:W_GLR + LANES]
    dt = w_ref[:, W_DT // LANES * LANES:W_DT // LANES * LANES + LANES]
    small_ref[...] = jnp.where(lane < GLA_RANK, glr, jnp.where(lane < GLA_RANK + SSM_HEADS, dt, 0.0)).astype(BF16)


def _repack(w_in):
    d, n_in = w_in.shape
    assert n_in == W_END and W_GLR + GLA_RANK == W_MID and W_DT % LANES == SMALL_DT0
    tr = 128
    return pl.pallas_call(
        _repack_body,
        grid=(d // tr,),
        in_specs=[pl.BlockSpec((tr, n_in), lambda i: (i, 0))],
        out_specs=[pl.BlockSpec((tr, P_COLS), lambda i: (i, 0)),
                   pl.BlockSpec((tr, SMALL_COLS), lambda i: (i, 0))],
        out_shape=[jax.ShapeDtypeStruct((d, P_COLS), BF16), jax.ShapeDtypeStruct((d, SMALL_COLS), BF16)],
        compiler_params=_cparams(("parallel",)),
        name="repack",
    )(w_in)


def _inproj_body(x_ref, mod_ref, g_ref, w_ref, ws_ref, p_ref, s_ref, h_ref):
    @pl.when(pl.program_id(1) == 0)
    def _():
        h = _rms(x_ref[...]) * g_ref[...] * (1.0 + mod_ref[1:2, :]) + mod_ref[0:1, :]
        hb = h.astype(BF16)
        h_ref[...] = hb
        s_ref[...] = _dot(hb, ws_ref[...])

    p_ref[...] = _dot(h_ref[...], w_ref[...]).astype(BF16)


def _inproj(x2, mod8, gain, w_big, w_small):
    t, d = x2.shape
    n = w_big.shape[1]
    tm, tn = 1024, 1024
    return pl.pallas_call(
        _inproj_body,
        grid=(t // tm, n // tn),
        in_specs=[pl.BlockSpec((tm, d), lambda m, j: (m, 0)),
                  pl.BlockSpec((8, d), lambda m, j: (0, 0)),
                  pl.BlockSpec((1, d), lambda m, j: (0, 0)),
                  pl.BlockSpec((d, tn), lambda m, j: (0, j)),
                  pl.BlockSpec((d, SMALL_COLS), lambda m, j: (0, 0))],
        out_specs=[pl.BlockSpec((tm, tn), lambda m, j: (m, j)),
                   pl.BlockSpec((tm, SMALL_COLS), lambda m, j: (m, 0))],
        out_shape=[jax.ShapeDtypeStruct((t, n), BF16),
                   jax.ShapeDtypeStruct((t, SMALL_COLS), F32)],
        scratch_shapes=[pltpu.VMEM((tm, d), BF16)],
        compiler_params=_cparams(("parallel", "arbitrary")),
        name="inproj",
    )(x2, mod8, gain, w_big, w_small)


GLA_TB = 512


def _gla_body(q_ref, k_ref, v_ref, og_ref, sm_ref, wup_ref, bg_ref, gn_ref, o_ref, st_ref):
    @pl.when(pl.program_id(1) == 0)
    def _():
        st_ref[...] = jnp.zeros_like(st_ref)

    tb, C = GLA_TB, CHUNK
    nch = tb // C
    r = lax.broadcasted_iota(I32, (tb, tb), 0)
    c = lax.broadcasted_iota(I32, (tb, tb), 1)
    causal = (r // C == c // C) & (r >= c)
    r2 = lax.broadcasted_iota(I32, (2 * C, 2 * C), 0)
    c2 = lax.broadcasted_iota(I32, (2 * C, 2 * C), 1)
    tril2 = ((r2 // C == c2 // C) & (r2 >= c2)).astype(BF16)

    glr = sm_ref[:, 0:GLA_RANK].astype(BF16)
    pre = _dot(glr, wup_ref[...].astype(BF16)) + bg_ref[...]
    log_a = _log_sigmoid(pre) / GLA_NORMALIZER
    b = jnp.concatenate([_dot_sel_l(tril2, log_a[i * 2 * C:(i + 1) * 2 * C]) for i in range(nch // 2)],
                        axis=0)
    b_last = [b[(i + 1) * C - 1:(i + 1) * C, :] for i in range(nch)]
    b_end = jnp.concatenate([jnp.broadcast_to(bl, (C, bl.shape[1])) for bl in b_last], axis=0)

    q = q_ref[...].astype(F32) * (GLA_HEAD_K ** -0.5)
    k = k_ref[...].astype(F32)
    v = v_ref[...]
    q_dec = (q * jnp.exp(b)).astype(BF16)
    k_inv = (k * jnp.exp(-b)).astype(BF16)
    k_end = (k * jnp.exp(b_end - b)).astype(BF16)
    scores = jnp.where(causal, _dot_nt(q_dec, k_inv), 0.0)
    o_intra = _dot(scores.astype(BF16), v)

    st = st_ref[...]
    o_inter = []
    for i in range(nch):
        rows = slice(i * C, (i + 1) * C)
        o_inter.append(_dot_nt(q_dec[rows], st.astype(BF16)))
        st = st * jnp.exp(b_last[i]) + _dot_tn(v[rows], k_end[rows])
    st_ref[...] = st

    o = o_intra + jnp.concatenate(o_inter, axis=0)
    og = og_ref[...].astype(F32)
    o_ref[...] = (_rms(o) * gn_ref[...] * _silu(og)).astype(BF16)


def _gla(p, small, wup, bg, gn):
    t = p.shape[0]
    tb = GLA_TB
    kb, vb = GLA_HEAD_K, GLA_HEAD_V
    return pl.pallas_call(
        _gla_body,
        grid=(GLA_HEADS, t // tb),
        in_specs=[pl.BlockSpec((tb, kb), lambda h, i: (i, COL_Q // kb + h)),
                  pl.BlockSpec((tb, kb), lambda h, i: (i, COL_K // kb + h)),
                  pl.BlockSpec((tb, vb), lambda h, i: (i, COL_V // vb + h)),
                  pl.BlockSpec((tb, vb), lambda h, i: (i, COL_OG // vb + h)),
                  pl.BlockSpec((tb, SMALL_COLS), lambda h, i: (i, 0)),
                  pl.BlockSpec((GLA_RANK, kb), lambda h, i: (0, h)),
                  pl.BlockSpec((1, kb), lambda h, i: (0, h)),
                  pl.BlockSpec((1, vb), lambda h, i: (0, h))],
        out_specs=pl.BlockSpec((tb, vb), lambda h, i: (i, h)),
        out_shape=jax.ShapeDtypeStruct((t, GLA_HEADS * vb), BF16),
        scratch_shapes=[pltpu.VMEM((vb, kb), F32)],
        compiler_params=_cparams(("parallel", "arbitrary")),
        name="gla",
    )(p, p, p, p, small, wup, bg, gn)


SSD_TB = 512
HALO = 8


def _ssd_body(xs_ref, b_ref, c_ref, z_ref, sm_ref, cwx_ref, cwb_ref, cwc_ref, cbx_ref, cbb_ref, cbc_ref,
              dtb_ref, alog_ref, dsk_ref, ng_ref, o_ref,
              ex_ref, eb_ref, ec_ref, at_ref, ht_ref):
    g = pl.program_id(0)
    t = pl.program_id(1)
    tb = SSD_TB
    gw = SSM_GROUP_W

    @pl.when(t == 0)
    def _():
        ht_ref[...] = jnp.zeros_like(ht_ref)
        ex_ref[0:HALO, :] = jnp.zeros((HALO, gw), F32)
        eb_ref[0:HALO, :] = jnp.zeros((HALO, SSM_STATE), F32)
        ec_ref[0:HALO, :] = jnp.zeros((HALO, SSM_STATE), F32)

    @pl.when(t > 0)
    def _():
        ex_ref[0:HALO, :] = ex_ref[tb:tb + HALO, :]
        eb_ref[0:HALO, :] = eb_ref[tb:tb + HALO, :]
        ec_ref[0:HALO, :] = ec_ref[tb:tb + HALO, :]

    def conv_silu(u_ref, e_ref, w_ref, bias_ref):
        e_ref[HALO:HALO + tb, :] = u_ref[...].astype(F32)
        acc = bias_ref[...] + w_ref[0:1, :] * e_ref[HALO - 3:HALO - 3 + tb, :]
        for kk in range(1, SSM_CONV):
            acc = acc + w_ref[kk:kk + 1, :] * e_ref[HALO - 3 + kk:HALO - 3 + kk + tb, :]
        return _silu(acc)

    xa = conv_silu(xs_ref, ex_ref, cwx_ref, cbx_ref)
    ba = conv_silu(b_ref, eb_ref, cwb_ref, cbb_ref).astype(BF16)
    ca = conv_silu(c_ref, ec_ref, cwc_ref, cbc_ref).astype(BF16)

    L = CHUNK
    nch = tb // L
    rep = gw // L
    hpg = gw // SSM_HEAD_DIM
    head0 = pl.multiple_of(SMALL_DT0 + g * hpg, hpg)

    e_row = lax.broadcasted_iota(I32, (SMALL_COLS, gw), 0)
    e_col = lax.broadcasted_iota(I32, (SMALL_COLS, gw), 1)
    expand = (e_row == head0 + e_col // SSM_HEAD_DIM).astype(BF16)

    dt_small = _softplus(sm_ref[...] + dtb_ref[...])
    adt_small = dt_small * (-jnp.exp(alog_ref[...]))
    rb = lax.broadcasted_iota(I32, (tb, tb), 0)
    cb = lax.broadcasted_iota(I32, (tb, tb), 1)
    blocktril = ((rb // L == cb // L) & (rb >= cb)).astype(BF16)
    acum_small = _dot_sel_l(blocktril, adt_small)
    at_ref[...] = acum_small.T
    heads = at_ref[pl.ds(head0, hpg), :]
    dt_exp = _dot_sel_r(dt_small, expand)
    acum = _dot_sel_r(acum_small, expand)
    d_exp = _dot_sel_r(jnp.broadcast_to(dsk_ref[...], (8, SMALL_COLS)), expand)[0:1]
    xdt = xa * dt_exp
    xdt_b = xdt.astype(BF16)
    e_acum = jnp.exp(acum)

    li = lax.broadcasted_iota(I32, (L, gw), 0)
    lj = lax.broadcasted_iota(I32, (L, gw), 1) % L
    causal_t = li >= lj
    hi = lax.broadcasted_iota(I32, (hpg, gw), 0)
    hj = lax.broadcasted_iota(I32, (hpg, gw), 1) // L
    headmask = hi == hj
    ones_h = jnp.ones((L, hpg), BF16)
    bi = lax.broadcasted_iota(I32, (gw, gw), 0) // L
    bj = lax.broadcasted_iota(I32, (gw, gw), 1) // SSM_HEAD_DIM
    blockmask = bi == bj
    masked_out = -1e30

    ht = ht_ref[...]
    for c in range(nch):
        rows = slice(c * L, (c + 1) * L)
        acum_c = acum[rows]
        a_rows = jnp.concatenate([heads[:, c * L:(c + 1) * L]] * rep, axis=1)
        rterm = _dot_sel_l(ones_h, jnp.where(headmask, a_rows, 0.0))
        decay = jnp.exp(jnp.where(causal_t, acum_c - rterm, masked_out))
        cc = ca[rows]
        bc = ba[rows]
        cb_t = _dot_nt(cc, jnp.concatenate([bc] * rep, axis=0))
        m = (cb_t * decay).astype(BF16)
        bd = jnp.where(blockmask, jnp.concatenate([xdt_b[rows]] * rep, axis=0), jnp.zeros((), BF16))
        y_diag = _dot(m, bd)
        y_off = _dot(cc, ht.astype(BF16)) * e_acum[rows]
        a_last = acum_c[L - 1:L, :]
        xd = (xdt[rows] * jnp.exp(a_last - acum_c)).astype(BF16)
        ht = ht * jnp.exp(a_last) + _dot_tn(bc, xd)
        y = y_diag + y_off + d_exp * xa[rows]
        y = y * _silu(z_ref[rows, :].astype(F32))
        o_ref[rows, :] = (_rms(y) * ng_ref[...]).astype(BF16)
    ht_ref[...] = ht


def _ssd(p, small, conv_w, conv_b, dt_bias_s, a_log_s, d_skip_s, norm_g):
    t = p.shape[0]
    tb = SSD_TB
    gw, ns = SSM_GROUP_W, SSM_STATE
    xs0, b0, c0 = 0, SSM_INNER // ns, (SSM_INNER + SSM_GROUPS * ns) // ns
    row = lambda w, off: pl.BlockSpec((1, w), lambda g, i: (0, off + g))
    return pl.pallas_call(
        _ssd_body,
        grid=(SSM_GROUPS, t // tb),
        in_specs=[pl.BlockSpec((tb, gw), lambda g, i: (i, COL_XS // gw + g)),
                  pl.BlockSpec((tb, ns), lambda g, i: (i, COL_B // ns + g)),
                  pl.BlockSpec((tb, ns), lambda g, i: (i, COL_C // ns + g)),
                  pl.BlockSpec((tb, gw), lambda g, i: (i, COL_Z // gw + g)),
                  pl.BlockSpec((tb, SMALL_COLS), lambda g, i: (i, 0)),
                  pl.BlockSpec((SSM_CONV, gw), lambda g, i: (0, xs0 + g)),
                  pl.BlockSpec((SSM_CONV, ns), lambda g, i: (0, b0 + g)),
                  pl.BlockSpec((SSM_CONV, ns), lambda g, i: (0, c0 + g)),
                  row(gw, xs0), row(ns, b0), row(ns, c0),
                  pl.BlockSpec((1, SMALL_COLS), lambda g, i: (0, 0)),
                  pl.BlockSpec((1, SMALL_COLS), lambda g, i: (0, 0)),
                  pl.BlockSpec((1, SMALL_COLS), lambda g, i: (0, 0)),
                  pl.BlockSpec((1, gw), lambda g, i: (0, g))],
        out_specs=pl.BlockSpec((tb, gw), lambda g, i: (i, g)),
        out_shape=jax.ShapeDtypeStruct((t, SSM_INNER), BF16),
        scratch_shapes=[pltpu.VMEM((tb + HALO, gw), F32),
                        pltpu.VMEM((tb + HALO, ns), F32),
                        pltpu.VMEM((tb + HALO, ns), F32),
                        pltpu.VMEM((SMALL_COLS, tb), F32),
                        pltpu.VMEM((ns, gw), F32)],
        compiler_params=_cparams(("parallel", "arbitrary")),
        name="ssd",
    )(p, p, p, p, small, conv_w, conv_w, conv_w, conv_b, conv_b, conv_b,
      dt_bias_s, a_log_s, d_skip_s, norm_g)


def _merge_body(a1_ref, a2_ref, w1_ref, w2_ref, gg_ref, gs_ref, o_ref):
    y1 = _dot(a1_ref[...], w1_ref[...])
    y2 = _dot(a2_ref[...], w2_ref[...])
    o_ref[...] = (_sigmoid(gg_ref[...].astype(F32)) * y1 + _sigmoid(gs_ref[...].astype(F32)) * y2).astype(BF16)


def _merge(o_gla, y_ssm, w1, w2, p):
    t, k1 = o_gla.shape
    k2 = y_ssm.shape[1]
    n = w1.shape[1]
    tm, tn = 512, 512
    return pl.pallas_call(
        _merge_body,
        grid=(t // tm, n // tn),
        in_specs=[pl.BlockSpec((tm, k1), lambda m, j: (m, 0)),
                  pl.BlockSpec((tm, k2), lambda m, j: (m, 0)),
                  pl.BlockSpec((k1, tn), lambda m, j: (0, j)),
                  pl.BlockSpec((k2, tn), lambda m, j: (0, j)),
                  pl.BlockSpec((tm, tn), lambda m, j: (m, COL_GG // tn + j)),
                  pl.BlockSpec((tm, tn), lambda m, j: (m, COL_GS // tn + j))],
        out_specs=pl.BlockSpec((tm, tn), lambda m, j: (m, j)),
        out_shape=jax.ShapeDtypeStruct((t, n), BF16),
        compiler_params=_cparams(("parallel", "arbitrary")),
        name="merge",
    )(o_gla, y_ssm, w1, w2, p, p)


def _outproj_body(m_ref, w_ref, x_ref, mod_ref, gpost_ref, gpre_ref, wr_ref, x1_ref, hf_ref, lg_ref):
    mix = _dot(m_ref[...], w_ref[...])
    x1 = x_ref[...] + mod_ref[2:3, :] * (_rms(mix) * gpost_ref[...])
    x1_ref[...] = x1
    h = _rms(x1) * gpre_ref[...] * (1.0 + mod_ref[4:5, :]) + mod_ref[3:4, :]
    for s in range(ROW_TILES):
        hf_ref[:, s, :] = h[:, s * LANES:(s + 1) * LANES]
    h_hi = h.astype(BF16)
    h_lo = (h - h_hi.astype(F32)).astype(BF16)
    wr = wr_ref[...]
    w_hi = wr.astype(BF16)
    w_lo = (wr - w_hi.astype(F32)).astype(BF16)
    lg_ref[...] = _dot(h_hi, w_hi) + _dot(h_hi, w_lo) + _dot(h_lo, w_hi)


def _outproj(merged, w_out, x2, mod8, g_post, g_pre, w_router):
    t, d = x2.shape
    tm = 256
    full = lambda r, c: pl.BlockSpec((r, c), lambda m: (0, 0))
    tile = lambda c: pl.BlockSpec((tm, c), lambda m: (m, 0))
    return pl.pallas_call(
        _outproj_body,
        grid=(t // tm,),
        in_specs=[tile(d), full(d, d), tile(d), full(8, d), full(1, d), full(1, d), full(d, 128)],
        out_specs=[tile(d), pl.BlockSpec((tm, ROW_TILES, LANES), lambda m: (m, 0, 0)), tile(128)],
        out_shape=[jax.ShapeDtypeStruct((t, d), F32),
                   jax.ShapeDtypeStruct((t, ROW_TILES, LANES), F32),
                   jax.ShapeDtypeStruct((t, 128), F32)],
        compiler_params=_cparams(("parallel",)),
        name="outproj",
    )(merged, w_out, x2, mod8, g_post, g_pre, w_router)


def _route_body(lg_ref, id_ref, w_ref):
    lg = lg_ref[...]
    lane = lax.broadcasted_iota(I32, lg.shape, 1)
    lane_f = lane.astype(F32)
    neg = jnp.float32(-jnp.inf)

    def first_argmax(vals, mx):
        return jnp.min(jnp.where(vals == mx, lane_f, 1e9), axis=-1, keepdims=True).astype(I32)

    gmask = lane < MOE_GROUPS
    gl = jnp.where(gmask, lg, neg)
    gmax = jnp.max(gl, axis=-1, keepdims=True)
    gsum = jnp.sum(jnp.where(gmask, jnp.exp(gl - gmax), 0.0), axis=-1, keepdims=True)
    g_w = 1.0 / gsum
    g_idx = first_argmax(gl, gmax)
    lo = MOE_GROUPS + g_idx * EXPERTS_PER_GROUP
    emask = (lane >= lo) & (lane < lo + EXPERTS_PER_GROUP)
    el = jnp.where(emask, lg, neg)
    m1 = jnp.max(el, axis=-1, keepdims=True)
    i1 = first_argmax(el, m1)
    el2 = jnp.where(lane == i1, neg, el)
    m2 = jnp.max(el2, axis=-1, keepdims=True)
    i2 = first_argmax(el2, m2)
    r = jnp.exp(m2 - m1)
    w1 = g_w / (1.0 + r)
    w2 = g_w * r / (1.0 + r)
    id_ref[...] = jnp.where(lane == 0, i1 - MOE_GROUPS, jnp.where(lane == 1, i2 - MOE_GROUPS, 0))
    w_ref[...] = jnp.where(lane == 0, w1, jnp.where(lane == 1, w2, 0.0))


def _route(logits):
    t = logits.shape[0]
    tm = 1024
    spec = pl.BlockSpec((tm, 128), lambda m: (m, 0))
    return pl.pallas_call(
        _route_body,
        grid=(t // tm,),
        in_specs=[spec],
        out_specs=[spec, spec],
        out_shape=[jax.ShapeDtypeStruct((t, 128), I32), jax.ShapeDtypeStruct((t, 128), F32)],
        compiler_params=_cparams(("parallel",)),
        name="route",
    )(logits)


GATHER_ROWS = 512


def _issue_rows(idx_ref, src_ref, buf_ref, slot, sem, skip_pads):
    def body(r, carry):
        tok = idx_ref[0, 0, r]
        copy = pltpu.make_async_copy(src_ref.at[jnp.maximum(tok, 0)], buf_ref.at[slot, :, r, :], sem.at[slot])
        if skip_pads:
            @pl.when(tok >= 0)
            def _():
                copy.start()
        else:
            copy.start()
        return carry

    lax.fori_loop(0, GATHER_ROWS, body, 0, unroll=8)


def _wait_rows(src_ref, buf_ref, slot, sem, count):
    del src_ref
    if count is None:
        pltpu.make_async_copy(buf_ref.at[slot], buf_ref.at[slot], sem.at[slot]).wait()
    else:
        @pl.when(count > 0)
        def _():
            part = buf_ref.at[slot, :, pl.ds(0, count), :]
            pltpu.make_async_copy(part, part, sem.at[slot]).wait()


def _gather_step(idx_ref, idx_next_ref, src_ref, buf_ref, sem, cnt_ref=None):
    i = pl.program_id(0)
    slot = i % 2
    skip_pads = cnt_ref is not None

    @pl.when(i == 0)
    def _():
        if skip_pads:
            buf_ref[...] = jnp.zeros_like(buf_ref)
        _issue_rows(idx_ref, src_ref, buf_ref, 0, sem, skip_pads)

    @pl.when(i + 1 < pl.num_programs(0))
    def _():
        _issue_rows(idx_next_ref, src_ref, buf_ref, 1 - slot, sem, skip_pads)

    _wait_rows(src_ref, buf_ref, slot, sem, cnt_ref[i] if skip_pads else None)
    return slot


def _gather_specs(nsteps):
    smem = lambda f: pl.BlockSpec((1, 1, GATHER_ROWS), f, memory_space=pltpu.SMEM)
    return [smem(lambda i: (i, 0, 0)),
            smem(lambda i: (jnp.minimum(i + 1, nsteps - 1), 0, 0)),
            pl.BlockSpec(memory_space=pl.ANY)]


GATHER_SCRATCH = [pltpu.VMEM((2, ROW_TILES, GATHER_ROWS, LANES), F32), pltpu.SemaphoreType.DMA((2,))]


def _dispatch_body(idx_ref, idx_next_ref, src_ref, cnt_ref, o_ref, buf_ref, sem):
    slot = _gather_step(idx_ref, idx_next_ref, src_ref, buf_ref, sem, cnt_ref)
    for c in range(ROW_TILES):
        o_ref[:, c * LANES:(c + 1) * LANES] = buf_ref[slot, c].astype(BF16)


def _dispatch(src, idx):
    n = idx.shape[0]
    nsteps = n // GATHER_ROWS
    idx3 = idx.reshape(nsteps, 1, GATHER_ROWS)
    counts = jnp.sum((idx3[:, 0, :] >= 0).astype(I32), axis=1)
    return pl.pallas_call(
        _dispatch_body,
        grid=(nsteps,),
        in_specs=_gather_specs(nsteps) + [pl.BlockSpec(memory_space=pltpu.SMEM)],
        out_specs=pl.BlockSpec((GATHER_ROWS, D_MODEL), lambda i: (i, 0)),
        out_shape=jax.ShapeDtypeStruct((n, D_MODEL), BF16),
        scratch_shapes=GATHER_SCRATCH,
        compiler_params=_cparams(("arbitrary",)),
        name="dispatch",
    )(idx3, idx3, src, counts)


MOE_FC = 512
MOE_J = MOE_FF // MOE_FC
N_ITEMS = (16384 // MOE_BLOCK + N_EXPERTS) // ITEM_BLOCKS + (N_EXPERTS * (ITEM_BLOCKS - 1)) // ITEM_BLOCKS


def _experts_body(ie_ref, io_ref, ins_ref, ifl_ref, nr_ref, wg_ref, wu_ref, wd_ref, xs_ref, ys_ref,
                  xb_ref, acc_ref, sem_in, sem_out):
    i = pl.program_id(0)
    j = pl.program_id(1)
    n_items = pl.num_programs(0)
    nsub = ins_ref[i]
    nfill = ifl_ref[i]
    slot = i % 2
    blk = MOE_BLOCK

    def row0(item):
        return pl.multiple_of(io_ref[item] * blk, blk)

    def x_copy(item, sl, s):
        return pltpu.make_async_copy(xs_ref.at[pl.ds(row0(item) + s * blk, blk)], xb_ref.at[sl, s],
                                     sem_in.at[sl, s])

    def start_y(item, s):
        for c in range(ROW_TILES):
            pltpu.make_async_copy(acc_ref.at[item % 2, s, :, pl.ds(c * LANES, LANES)],
                                  ys_ref.at[pl.ds(row0(item) + s * blk, blk), c, :], sem_out.at[s]).start()

    def wait_y(s):
        pltpu.make_async_copy(acc_ref.at[0, s], acc_ref.at[0, s], sem_out.at[s]).wait()

    def for_blocks(count, fn):
        for s in range(ITEM_BLOCKS):
            @pl.when(s < count)
            def _():
                fn(s)

    @pl.when(j == 0)
    def _():
        @pl.when(i == 0)
        def _():
            for_blocks(nsub, lambda s: x_copy(0, 0, s).start())

        for_blocks(nsub, lambda s: x_copy(i, slot, s).wait())

    @pl.when((j == MOE_J - 1) & (i + 1 < n_items))
    def _():
        nxt = jnp.minimum(i + 1, n_items - 1)
        for_blocks(ins_ref[nxt], lambda s: x_copy(nxt, 1 - slot, s).start())

    for n in range(1, ITEM_BLOCKS + 1):
        @pl.when(nsub == n)
        def _():
            x = xb_ref[slot, 0:n].reshape(n * blk, D_MODEL)
            gate = _dot(x, wg_ref[...].astype(BF16))
            up = _dot(x, wu_ref[...].astype(BF16))
            hid = (_silu(gate) * up).astype(BF16)
            y = _dot(hid, wd_ref[...].astype(BF16)).reshape(n, blk, D_MODEL)

            @pl.when(j == 0)
            def _():
                acc_ref[slot, 0:n] = y

            @pl.when(j > 0)
            def _():
                acc_ref[slot, 0:n] = acc_ref[slot, 0:n] + y

    def wait_prev_y():
        @pl.when(i > 0)
        def _():
            for_blocks(ins_ref[jnp.maximum(i - 1, 0)], wait_y)

    @pl.when((nsub > 0) & (j == MOE_J - 1))
    def _():
        wait_prev_y()
        for_blocks(nsub, lambda s: start_y(i, s))

        @pl.when(i == n_items - 1)
        def _():
            for_blocks(nsub, wait_y)

    @pl.when((nsub == 0) & (j == 0))
    def _():
        wait_prev_y()

        @pl.when(nfill > 0)
        def _():
            acc_ref[slot] = jnp.zeros(acc_ref.shape[1:], F32)
            for_blocks(nfill, lambda s: start_y(i, s))
            for_blocks(nfill, wait_y)


def _experts(x_sorted, w_gate, w_up, w_down, item_e, item_off, item_nsub, item_fill, n_real):
    n_rows = x_sorted.shape[0]
    d = D_MODEL

    def w_in_map(i, j, ie, io, ins, ifl, nr):
        return (ie[i], 0, jnp.where(i < nr[0], j, MOE_J - 1))

    def w_dn_map(i, j, ie, io, ins, ifl, nr):
        return (ie[i], jnp.where(i < nr[0], j, MOE_J - 1), 0)

    grid_spec = pltpu.PrefetchScalarGridSpec(
        num_scalar_prefetch=5,
        grid=(N_ITEMS, MOE_J),
        in_specs=[pl.BlockSpec((None, d, MOE_FC), w_in_map),
                  pl.BlockSpec((None, d, MOE_FC), w_in_map),
                  pl.BlockSpec((None, MOE_FC, d), w_dn_map),
                  pl.BlockSpec(memory_space=pl.ANY)],
        out_specs=pl.BlockSpec(memory_space=pl.ANY),
        scratch_shapes=[pltpu.VMEM((2, ITEM_BLOCKS, MOE_BLOCK, d), BF16),
                        pltpu.VMEM((2, ITEM_BLOCKS, MOE_BLOCK, d), F32),
                        pltpu.SemaphoreType.DMA((2, ITEM_BLOCKS)),
                        pltpu.SemaphoreType.DMA((ITEM_BLOCKS,))],
    )
    return pl.pallas_call(
        _experts_body,
        grid_spec=grid_spec,
        out_shape=jax.ShapeDtypeStruct((n_rows, ROW_TILES, LANES), F32),
        compiler_params=_cparams(("arbitrary", "arbitrary")),
        name="experts",
    )(item_e, item_off, item_nsub, item_fill, n_real, w_gate, w_up, w_down, x_sorted)


FINAL_TM = GATHER_ROWS // 2


def _final_body(idx_ref, idx_next_ref, ys_ref, w_ref, x1_ref, mod_ref, g_ref, o_ref, buf_ref, sem):
    slot = _gather_step(idx_ref, idx_next_ref, ys_ref, buf_ref, sem)
    tm = FINAL_TM
    w = w_ref[...]
    w0, w1 = w[:, 0:1], w[:, 1:2]
    ffn = jnp.concatenate([w0 * buf_ref[slot, c, 0:tm, :] + w1 * buf_ref[slot, c, tm:2 * tm, :]
                           for c in range(ROW_TILES)], axis=1)
    o_ref[...] = x1_ref[...] + mod_ref[5:6, :] * (_rms(ffn) * g_ref[...])


def _final(y_sorted, pos, wts, x1, mod8, g_post):
    t, d = x1.shape
    tm = FINAL_TM
    nt = t // tm
    idx3 = pos.reshape(nt, tm, 2).transpose(0, 2, 1).reshape(nt, 1, 2 * tm)
    return pl.pallas_call(
        _final_body,
        grid=(nt,),
        in_specs=_gather_specs(nt) + [pl.BlockSpec((tm, 128), lambda m: (m, 0)),
                                      pl.BlockSpec((tm, d), lambda m: (m, 0)),
                                      pl.BlockSpec((8, d), lambda m: (0, 0)),
                                      pl.BlockSpec((1, d), lambda m: (0, 0))],
        out_specs=pl.BlockSpec((tm, d), lambda m: (m, 0)),
        out_shape=jax.ShapeDtypeStruct((t, d), F32),
        scratch_shapes=GATHER_SCRATCH,
        compiler_params=_cparams(("arbitrary",)),
        name="final",
    )(idx3, idx3, y_sorted, wts, x1, mod8, g_post)


PLAN_TT = 512


def _plan_body(ids_ref, tril_ref, upper_ref, dest_ref, cnt_ref, run_ref):
    p = pl.program_id(0)
    i = pl.program_id(1)

    @pl.when((p == 0) & (i == 0))
    def _():
        run_ref[...] = jnp.zeros_like(run_ref)

    @pl.when((p == 1) & (i == 0))
    def _():
        counts = run_ref[...]
        cnt_ref[...] = counts.astype(I32)
        nblk = jnp.floor((counts + (MOE_BLOCK - 1.0)) * (1.0 / MOE_BLOCK))
        blk_start = _dot(nblk.astype(BF16), upper_ref[...])
        run_ref[...] = blk_start * MOE_BLOCK

    ids = ids_ref[...]
    lane = lax.broadcasted_iota(I32, ids.shape, 1)
    oh0 = lane == ids[:, 0:1]
    oh1 = lane == ids[:, 1:2]
    both = jnp.where(oh0 | oh1, 1.0, 0.0).astype(BF16)

    @pl.when(p == 1)
    def _():
        nxt = _dot(tril_ref[...], both) + run_ref[0:1, :]
        d0 = jnp.sum(jnp.where(oh0, nxt, 0.0), axis=-1, keepdims=True)
        d1 = jnp.sum(jnp.where(oh1, nxt, 0.0), axis=-1, keepdims=True)
        dest_ref[...] = jnp.where(lane == 0, d0, jnp.where(lane == 1, d1, 0.0)).astype(I32)

    run_ref[...] = run_ref[...] + _dot(jnp.ones((8, PLAN_TT), BF16), both)


def _plan(ids):
    t = ids.shape[0]
    r = np.arange(PLAN_TT)
    e = np.arange(128)
    strict_tril = jnp.asarray(r[:, None] > r[None, :], BF16)
    strict_upper = jnp.asarray(e[:, None] < e[None, :], BF16)
    return pl.pallas_call(
        _plan_body,
        grid=(2, t // PLAN_TT),
        in_specs=[pl.BlockSpec((PLAN_TT, 128), lambda p, i: (i, 0)),
                  pl.BlockSpec((PLAN_TT, PLAN_TT), lambda p, i: (0, 0)),
                  pl.BlockSpec((128, 128), lambda p, i: (0, 0))],
        out_specs=[pl.BlockSpec((PLAN_TT, 128), lambda p, i: (i * p, 0)),
                   pl.BlockSpec((8, 128), lambda p, i: (0, 0))],
        out_shape=[jax.ShapeDtypeStruct((t, 128), I32), jax.ShapeDtypeStruct((8, 128), I32)],
        scratch_shapes=[pltpu.VMEM((8, 128), F32)],
        compiler_params=_cparams(("arbitrary", "arbitrary")),
        name="plan",
    )(ids, strict_tril, strict_upper)


def _routing_tables(ids128, n_tok):
    n_assign = n_tok * 2
    n_blocks = n_assign // MOE_BLOCK + N_EXPERTS
    dest128, cnt = _plan(ids128)
    dest = dest128[:, :2].reshape(n_assign)
    counts = cnt[0, :N_EXPERTS]
    nb = (counts + MOE_BLOCK - 1) // MOE_BLOCK
    blk_start = jnp.cumsum(nb) - nb
    n_rows = n_blocks * MOE_BLOCK
    row_tok = jnp.full((n_rows,), -1, I32).at[dest].set(jnp.arange(n_assign, dtype=I32) // 2)
    pos = dest.reshape(n_tok, 2)

    n_it = (nb + ITEM_BLOCKS - 1) // ITEM_BLOCKS
    it_end = jnp.cumsum(n_it)
    it_start = it_end - n_it
    n_real = it_end[-1]
    i = jnp.arange(N_ITEMS, dtype=I32)
    e_i = jnp.minimum(jnp.searchsorted(it_end, i, side='right').astype(I32), N_EXPERTS - 1)
    k_i = i - it_start[e_i]
    valid = i < n_real
    last_e = e_i[jnp.maximum(n_real - 1, 0)]
    item_e = jnp.where(valid, e_i, last_e).astype(I32)
    fill_off = jnp.sum(nb) + ITEM_BLOCKS * (i - n_real)
    item_fill = jnp.where(valid, 0, jnp.clip(n_blocks - fill_off, 0, ITEM_BLOCKS)).astype(I32)
    item_off = jnp.where(valid, blk_start[e_i] + ITEM_BLOCKS * k_i, jnp.minimum(fill_off, n_blocks - 1)).astype(I32)
    item_nsub = jnp.where(valid, jnp.clip(nb[e_i] - ITEM_BLOCKS * k_i, 0, ITEM_BLOCKS), 0).astype(I32)
    return row_tok, pos, item_e, item_off, item_nsub, item_fill, n_real.reshape(1).astype(I32)


def _pad_lanes(v, start, total=SMALL_COLS):
    return jnp.zeros((1, total), F32).at[0, start:start + v.shape[0]].set(v)


def _layer(x2, c, w_ada, b_ada, norm_pre_mix, norm_post_mix, norm_pre_ffn, norm_post_ffn,
           w_in, gla_w_gate_up, gla_b_gate, gla_norm, ssm_conv_w, ssm_conv_b, ssm_dt_bias,
           ssm_a_log, ssm_d, ssm_norm, w_branch_gla, w_branch_ssm, w_out,
           router_group, router_expert, moe_w_gate, moe_w_up, moe_w_down):
    t, d = x2.shape
    row = lambda v: v.reshape(1, -1)

    mod = _ada(c, w_ada, b_ada)
    mod8 = jnp.concatenate([mod.reshape(6, d), jnp.zeros((2, d), F32)], axis=0)

    w_big, w_small = _repack(w_in)
    p, small = _inproj(x2, mod8, row(norm_pre_mix), w_big, w_small)

    o_gla = _gla(p, small, gla_w_gate_up, row(gla_b_gate), row(gla_norm))
    y_ssm = _ssd(p, small, ssm_conv_w, row(ssm_conv_b),
                 _pad_lanes(ssm_dt_bias, SMALL_DT0), _pad_lanes(ssm_a_log, SMALL_DT0),
                 _pad_lanes(ssm_d, SMALL_DT0), row(ssm_norm))
    merged = _merge(o_gla, y_ssm, w_branch_gla.astype(BF16), w_branch_ssm.astype(BF16), p)

    w_router = jnp.concatenate([router_group, router_expert,
                                jnp.zeros((d, 128 - MOE_GROUPS - N_EXPERTS), F32)], axis=1)
    x1, h2f, logits = _outproj(merged, w_out.astype(BF16), x2, mod8,
                               row(norm_post_mix), row(norm_pre_ffn), w_router)
    ids, wts = _route(logits)

    row_tok, pos, item_e, item_off, item_nsub, item_fill, n_real = _routing_tables(ids, t)
    x_sorted = _dispatch(h2f, row_tok)
    y_sorted = _experts(x_sorted, moe_w_gate, moe_w_up, moe_w_down,
                        item_e, item_off, item_nsub, item_fill, n_real)
    return _final(y_sorted, pos, wts, x1, mod8, row(norm_post_ffn))


def kernel(x, c, w_ada, b_ada, norm_pre_mix, norm_post_mix, norm_pre_ffn, norm_post_ffn, w_in, gla_w_gate_up, gla_b_gate, gla_norm, ssm_conv_w, ssm_conv_b, ssm_dt_bias, ssm_a_log, ssm_d, ssm_norm, w_branch_gla, w_branch_ssm, w_out, router_group, router_expert, moe_w_gate, moe_w_up, moe_w_down):
    bsz, seq, d = x.shape
    assert bsz == 1 and d == D_MODEL
    x2 = x.reshape(seq, d)
    params = (w_ada, b_ada, norm_pre_mix, norm_post_mix, norm_pre_ffn, norm_post_ffn, w_in, gla_w_gate_up,
              gla_b_gate, gla_norm, ssm_conv_w, ssm_conv_b, ssm_dt_bias, ssm_a_log, ssm_d, ssm_norm,
              w_branch_gla, w_branch_ssm, w_out, router_group, router_expert, moe_w_gate, moe_w_up, moe_w_down)
    for layer in range(w_ada.shape[0]):
        x2 = _layer(x2, c, *(prm[layer] for prm in params))
    return x2.reshape(bsz, seq, d)
```

```python
import functools

import jax
import jax.numpy as jnp
import numpy as np
from jax import lax
from jax.experimental import pallas as pl
from jax.experimental.pallas import tpu as pltpu

F32 = jnp.float32
BF16 = jnp.bfloat16
I32 = jnp.int32

D_MODEL = 2048
EPS = 1e-6
LANES = 128
ROW_TILES = D_MODEL // LANES

GLA_HEADS = 4
GLA_HEAD_K = 256
GLA_HEAD_V = 512
GLA_RANK = 16
GLA_NORMALIZER = 16.0
CHUNK = 64

SSM_GROUPS = 8
SSM_HEADS = 64
SSM_HEAD_DIM = 64
SSM_STATE = 128
SSM_CONV = 4
SSM_GROUP_W = 512
SSM_INNER = 4096

N_EXPERTS = 64
EXPERTS_PER_GROUP = 8
MOE_GROUPS = 8
MOE_FF = 1024
MOE_BLOCK = 128
ITEM_BLOCKS = 4

COL_Q, COL_K, COL_V, COL_OG, COL_Z, COL_XS, COL_B, COL_C, COL_GG, COL_GS = (
    0, 1024, 2048, 4096, 6144, 10240, 14336, 15360, 16384, 18432)
P_COLS = 20480
SMALL_COLS = 128
SMALL_DT0 = GLA_RANK

VMEM_LIMIT = 56 * 1024 * 1024


def _cparams(sem, vmem=VMEM_LIMIT):
    return pltpu.CompilerParams(dimension_semantics=sem, vmem_limit_bytes=vmem)


def _dot(a, b):
    return jnp.dot(a, b, preferred_element_type=F32)


def _dot_nt(a, b):
    return lax.dot_general(a, b, (((1,), (1,)), ((), ())), preferred_element_type=F32)


def _dot_tn(a, b):
    return lax.dot_general(a, b, (((0,), (0,)), ((), ())), preferred_element_type=F32)


def _split3(a):
    hi = a.astype(BF16)
    r1 = a - hi.astype(F32)
    mid = r1.astype(BF16)
    lo = (r1 - mid.astype(F32)).astype(BF16)
    return hi, mid, lo


def _dot_sel_r(a, sel):
    hi, mid, lo = _split3(a)
    return _dot(hi, sel) + _dot(mid, sel) + _dot(lo, sel)


def _dot_sel_l(sel, a):
    hi, mid, lo = _split3(a)
    return _dot(sel, hi) + _dot(sel, mid) + _dot(sel, lo)


def _sigmoid(x):
    return 1.0 / (1.0 + jnp.exp(-x))


def _silu(x):
    return x * _sigmoid(x)


def _softplus(x):
    return jnp.maximum(x, 0.0) + jnp.log1p(jnp.exp(-jnp.abs(x)))


def _log_sigmoid(x):
    return jnp.minimum(x, 0.0) - jnp.log1p(jnp.exp(-jnp.abs(x)))


def _rms(x):
    return x * lax.rsqrt(jnp.mean(x * x, axis=-1, keepdims=True) + EPS)


def _ada_body(c_ref, w_ref, b_ref, o_ref):
    c = c_ref[...]
    s = jnp.broadcast_to(_silu(c), (8, c.shape[1])).astype(BF16)
    o_ref[...] = _dot(s, w_ref[...].astype(BF16))[0:1] + b_ref[...]


def _ada(c, w_ada, b_ada):
    d, n = w_ada.shape
    tn = 1024
    return pl.pallas_call(
        _ada_body,
        grid=(n // tn,),
        in_specs=[pl.BlockSpec((1, d), lambda j: (0, 0)),
                  pl.BlockSpec((d, tn), lambda j: (0, j)),
                  pl.BlockSpec((1, tn), lambda j: (0, j))],
        out_specs=pl.BlockSpec((1, tn), lambda j: (0, j)),
        out_shape=jax.ShapeDtypeStruct((1, n), F32),
        compiler_params=_cparams(("parallel",)),
        name="ada",
    )(c, w_ada, b_ada.reshape(1, n))


W_GLR, W_MID, W_DT, W_GATES, W_END = 4096, 4112, 16400, 16464, 20560
REPACK_ROWS = 1024
REPACK_TAIL = 128


def _repack_body(a_ref, b_ref, big_ref, small_ref):
    j = pl.program_id(0)
    n = REPACK_ROWS

    def emit(shift):
        if shift == 0:
            big_ref[...] = a_ref[...].astype(BF16)
        else:
            big_ref[0:n - shift, :] = a_ref[shift:n, :].astype(BF16)
            big_ref[n - shift:n, :] = b_ref[0:shift, :].astype(BF16)

    first_mid, first_gates = COL_OG // n, COL_GG // n

    @pl.when(j < first_mid)
    def _():
        emit(0)

    @pl.when((j >= first_mid) & (j < first_gates))
    def _():
        emit(W_MID - COL_OG)

    @pl.when(j >= first_gates)
    def _():
        emit(W_GATES - COL_GG)

    @pl.when(j == 0)
    def _():
        small_ref[...] = jnp.zeros_like(small_ref)

    @pl.when(j == W_GLR // n)
    def _():
        small_ref[0:GLA_RANK, :] = a_ref[0:GLA_RANK, :].astype(BF16)

    @pl.when(j == W_DT // n)
    def _():
        small_ref[SMALL_DT0:SMALL_DT0 + SSM_HEADS, :] = a_ref[SMALL_DT0:SMALL_DT0 + SSM_HEADS, :].astype(BF16)


def _repack(w_t):
    n_in, d = w_t.shape
    n = REPACK_ROWS
    assert n_in == W_END and W_GLR % n == 0 and W_GLR + GLA_RANK == W_MID and W_DT % n == SMALL_DT0
    assert COL_OG % n == 0 and COL_GG % n == 0 and W_GATES - COL_GG <= REPACK_TAIL
    return pl.pallas_call(
        _repack_body,
        grid=(P_COLS // n,),
        in_specs=[pl.BlockSpec((n, d), lambda j: (j, 0)),
                  pl.BlockSpec((REPACK_TAIL, d), lambda j: ((j + 1) * (n // REPACK_TAIL), 0))],
        out_specs=[pl.BlockSpec((n, d), lambda j: (j, 0)),
                   pl.BlockSpec((SMALL_COLS, d), lambda j: (0, 0))],
        out_shape=[jax.ShapeDtypeStruct((P_COLS, d), BF16), jax.ShapeDtypeStruct((SMALL_COLS, d), BF16)],
        compiler_params=_cparams(("arbitrary",)),
        name="repack",
    )(w_t, w_t)


def _inproj_body(x_ref, mod_ref, g_ref, w_ref, ws_ref, p_ref, s_ref, h_ref):
    @pl.when(pl.program_id(1) == 0)
    def _():
        h = _rms(x_ref[...]) * g_ref[...] * (1.0 + mod_ref[1:2, :]) + mod_ref[0:1, :]
        hb = h.astype(BF16)
        h_ref[...] = hb
        s_ref[...] = _dot_nt(hb, ws_ref[...])

    p_ref[...] = _dot_nt(h_ref[...], w_ref[...]).astype(BF16)


def _inproj(x2, mod8, gain, w_big, w_small):
    t, d = x2.shape
    n = w_big.shape[0]
    tm, tn = 1024, 1024
    return pl.pallas_call(
        _inproj_body,
        grid=(t // tm, n // tn),
        in_specs=[pl.BlockSpec((tm, d), lambda m, j: (m, 0)),
                  pl.BlockSpec((8, d), lambda m, j: (0, 0)),
                  pl.BlockSpec((1, d), lambda m, j: (0, 0)),
                  pl.BlockSpec((tn, d), lambda m, j: (j, 0)),
                  pl.BlockSpec((SMALL_COLS, d), lambda m, j: (0, 0))],
        out_specs=[pl.BlockSpec((tm, tn), lambda m, j: (m, j)),
                   pl.BlockSpec((tm, SMALL_COLS), lambda m, j: (m, 0))],
        out_shape=[jax.ShapeDtypeStruct((t, n), BF16),
                   jax.ShapeDtypeStruct((t, SMALL_COLS), F32)],
        scratch_shapes=[pltpu.VMEM((tm, d), BF16)],
        compiler_params=_cparams(("parallel", "arbitrary")),
        name="inproj",
    )(x2, mod8, gain, w_big, w_small)


GLA_TB = 512


def _gla_body(q_ref, k_ref, v_ref, og_ref, sm_ref, wup_ref, bg_ref, gn_ref, o_ref, st_ref):
    @pl.when(pl.program_id(1) == 0)
    def _():
        st_ref[...] = jnp.zeros_like(st_ref)

    tb, C = GLA_TB, CHUNK
    nch = tb // C
    r = lax.broadcasted_iota(I32, (tb, tb), 0)
    c = lax.broadcasted_iota(I32, (tb, tb), 1)
    causal = (r // C == c // C) & (r >= c)
    r2 = lax.broadcasted_iota(I32, (2 * C, 2 * C), 0)
    c2 = lax.broadcasted_iota(I32, (2 * C, 2 * C), 1)
    tril2 = ((r2 // C == c2 // C) & (r2 >= c2)).astype(BF16)

    glr = sm_ref[:, 0:GLA_RANK].astype(BF16)
    pre = _dot(glr, wup_ref[...].astype(BF16)) + bg_ref[...]
    log_a = _log_sigmoid(pre) / GLA_NORMALIZER
    b = jnp.concatenate([_dot_sel_l(tril2, log_a[i * 2 * C:(i + 1) * 2 * C]) for i in range(nch // 2)],
                        axis=0)
    b_last = [b[(i + 1) * C - 1:(i + 1) * C, :] for i in range(nch)]
    b_end = jnp.concatenate([jnp.broadcast_to(bl, (C, bl.shape[1])) for bl in b_last], axis=0)

    q = q_ref[...].astype(F32) * (GLA_HEAD_K ** -0.5)
    k = k_ref[...].astype(F32)
    v = v_ref[...]
    q_dec = (q * jnp.exp(b)).astype(BF16)
    k_inv = (k * jnp.exp(-b)).astype(BF16)
    k_end = (k * jnp.exp(b_end - b)).astype(BF16)
    scores = jnp.where(causal, _dot_nt(q_dec, k_inv), 0.0)
    o_intra = _dot(scores.astype(BF16), v)

    st = st_ref[...]
    o_inter = []
    for i in range(nch):
        rows = slice(i * C, (i + 1) * C)
        o_inter.append(_dot_nt(q_dec[rows], st.astype(BF16)))
        st = st * jnp.exp(b_last[i]) + _dot_tn(v[rows], k_end[rows])
    st_ref[...] = st

    o = o_intra + jnp.concatenate(o_inter, axis=0)
    og = og_ref[...].astype(F32)
    o_ref[...] = (_rms(o) * gn_ref[...] * _silu(og)).astype(BF16)


def _gla(p, small, wup, bg, gn):
    t = p.shape[0]
    tb = GLA_TB
    kb, vb = GLA_HEAD_K, GLA_HEAD_V
    return pl.pallas_call(
        _gla_body,
        grid=(GLA_HEADS, t // tb),
        in_specs=[pl.BlockSpec((tb, kb), lambda h, i: (i, COL_Q // kb + h)),
                  pl.BlockSpec((tb, kb), lambda h, i: (i, COL_K // kb + h)),
                  pl.BlockSpec((tb, vb), lambda h, i: (i, COL_V // vb + h)),
                  pl.BlockSpec((tb, vb), lambda h, i: (i, COL_OG // vb + h)),
                  pl.BlockSpec((tb, SMALL_COLS), lambda h, i: (i, 0)),
                  pl.BlockSpec((GLA_RANK, kb), lambda h, i: (0, h)),
                  pl.BlockSpec((1, kb), lambda h, i: (0, h)),
                  pl.BlockSpec((1, vb), lambda h, i: (0, h))],
        out_specs=pl.BlockSpec((tb, vb), lambda h, i: (i, h)),
        out_shape=jax.ShapeDtypeStruct((t, GLA_HEADS * vb), BF16),
        scratch_shapes=[pltpu.VMEM((vb, kb), F32)],
        compiler_params=_cparams(("parallel", "arbitrary")),
        name="gla",
    )(p, p, p, p, small, wup, bg, gn)


SSD_TB = 512
HALO = 8


def _ssd_body(xs_ref, b_ref, c_ref, z_ref, sm_ref, cwx_ref, cwb_ref, cwc_ref, cbx_ref, cbb_ref, cbc_ref,
              dtb_ref, alog_ref, dsk_ref, ng_ref, o_ref,
              ex_ref, eb_ref, ec_ref, at_ref, ht_ref):
    g = pl.program_id(0)
    t = pl.program_id(1)
    tb = SSD_TB
    gw = SSM_GROUP_W

    @pl.when(t == 0)
    def _():
        ht_ref[...] = jnp.zeros_like(ht_ref)
        ex_ref[0:HALO, :] = jnp.zeros((HALO, gw), F32)
        eb_ref[0:HALO, :] = jnp.zeros((HALO, SSM_STATE), F32)
        ec_ref[0:HALO, :] = jnp.zeros((HALO, SSM_STATE), F32)

    @pl.when(t > 0)
    def _():
        ex_ref[0:HALO, :] = ex_ref[tb:tb + HALO, :]
        eb_ref[0:HALO, :] = eb_ref[tb:tb + HALO, :]
        ec_ref[0:HALO, :] = ec_ref[tb:tb + HALO, :]

    def conv_silu(u_ref, e_ref, w_ref, bias_ref):
        e_ref[HALO:HALO + tb, :] = u_ref[...].astype(F32)
        acc = bias_ref[...] + w_ref[0:1, :] * e_ref[HALO - 3:HALO - 3 + tb, :]
        for kk in range(1, SSM_CONV):
            acc = acc + w_ref[kk:kk + 1, :] * e_ref[HALO - 3 + kk:HALO - 3 + kk + tb, :]
        return _silu(acc)

    xa = conv_silu(xs_ref, ex_ref, cwx_ref, cbx_ref)
    ba = conv_silu(b_ref, eb_ref, cwb_ref, cbb_ref).astype(BF16)
    ca = conv_silu(c_ref, ec_ref, cwc_ref, cbc_ref).astype(BF16)

    L = CHUNK
    nch = tb // L
    rep = gw // L
    hpg = gw // SSM_HEAD_DIM
    head0 = pl.multiple_of(SMALL_DT0 + g * hpg, hpg)

    e_row = lax.broadcasted_iota(I32, (SMALL_COLS, gw), 0)
    e_col = lax.broadcasted_iota(I32, (SMALL_COLS, gw), 1)
    expand = (e_row == head0 + e_col // SSM_HEAD_DIM).astype(BF16)

    dt_small = _softplus(sm_ref[...] + dtb_ref[...])
    adt_small = dt_small * (-jnp.exp(alog_ref[...]))
    rb = lax.broadcasted_iota(I32, (tb, tb), 0)
    cb = lax.broadcasted_iota(I32, (tb, tb), 1)
    blocktril = ((rb // L == cb // L) & (rb >= cb)).astype(BF16)
    acum_small = _dot_sel_l(blocktril, adt_small)
    at_ref[...] = acum_small.T
    heads = at_ref[pl.ds(head0, hpg), :]
    dt_exp = _dot_sel_r(dt_small, expand)
    acum = _dot_sel_r(acum_small, expand)
    d_exp = _dot_sel_r(jnp.broadcast_to(dsk_ref[...], (8, SMALL_COLS)), expand)[0:1]
    xdt = xa * dt_exp
    xdt_b = xdt.astype(BF16)
    e_acum = jnp.exp(acum)

    li = lax.broadcasted_iota(I32, (L, gw), 0)
    lj = lax.broadcasted_iota(I32, (L, gw), 1) % L
    causal_t = li >= lj
    hi = lax.broadcasted_iota(I32, (hpg, gw), 0)
    hj = lax.broadcasted_iota(I32, (hpg, gw), 1) // L
    headmask = hi == hj
    ones_h = jnp.ones((L, hpg), BF16)
    bi = lax.broadcasted_iota(I32, (gw, gw), 0) // L
    bj = lax.broadcasted_iota(I32, (gw, gw), 1) // SSM_HEAD_DIM
    blockmask = bi == bj
    masked_out = -1e30

    ht = ht_ref[...]
    for c in range(nch):
        rows = slice(c * L, (c + 1) * L)
        acum_c = acum[rows]
        a_rows = jnp.concatenate([heads[:, c * L:(c + 1) * L]] * rep, axis=1)
        rterm = _dot_sel_l(ones_h, jnp.where(headmask, a_rows, 0.0))
        decay = jnp.exp(jnp.where(causal_t, acum_c - rterm, masked_out))
        cc = ca[rows]
        bc = ba[rows]
        cb_t = _dot_nt(cc, jnp.concatenate([bc] * rep, axis=0))
        m = (cb_t * decay).astype(BF16)
        bd = jnp.where(blockmask, jnp.concatenate([xdt_b[rows]] * rep, axis=0), jnp.zeros((), BF16))
        y_diag = _dot(m, bd)
        y_off = _dot(cc, ht.astype(BF16)) * e_acum[rows]
        a_last = acum_c[L - 1:L, :]
        xd = (xdt[rows] * jnp.exp(a_last - acum_c)).astype(BF16)
        ht = ht * jnp.exp(a_last) + _dot_tn(bc, xd)
        y = y_diag + y_off + d_exp * xa[rows]
        y = y * _silu(z_ref[rows, :].astype(F32))
        o_ref[rows, :] = (_rms(y) * ng_ref[...]).astype(BF16)
    ht_ref[...] = ht


def _ssd(p, small, conv_w, conv_b, dt_bias_s, a_log_s, d_skip_s, norm_g):
    t = p.shape[0]
    tb = SSD_TB
    gw, ns = SSM_GROUP_W, SSM_STATE
    xs0, b0, c0 = 0, SSM_INNER // ns, (SSM_INNER + SSM_GROUPS * ns) // ns
    row = lambda w, off: pl.BlockSpec((1, w), lambda g, i: (0, off + g))
    return pl.pallas_call(
        _ssd_body,
        grid=(SSM_GROUPS, t // tb),
        in_specs=[pl.BlockSpec((tb, gw), lambda g, i: (i, COL_XS // gw + g)),
                  pl.BlockSpec((tb, ns), lambda g, i: (i, COL_B // ns + g)),
                  pl.BlockSpec((tb, ns), lambda g, i: (i, COL_C // ns + g)),
                  pl.BlockSpec((tb, gw), lambda g, i: (i, COL_Z // gw + g)),
                  pl.BlockSpec((tb, SMALL_COLS), lambda g, i: (i, 0)),
                  pl.BlockSpec((SSM_CONV, gw), lambda g, i: (0, xs0 + g)),
                  pl.BlockSpec((SSM_CONV, ns), lambda g, i: (0, b0 + g)),
                  pl.BlockSpec((SSM_CONV, ns), lambda g, i: (0, c0 + g)),
                  row(gw, xs0), row(ns, b0), row(ns, c0),
                  pl.BlockSpec((1, SMALL_COLS), lambda g, i: (0, 0)),
                  pl.BlockSpec((1, SMALL_COLS), lambda g, i: (0, 0)),
                  pl.BlockSpec((1, SMALL_COLS), lambda g, i: (0, 0)),
                  pl.BlockSpec((1, gw), lambda g, i: (0, g))],
        out_specs=pl.BlockSpec((tb, gw), lambda g, i: (i, g)),
        out_shape=jax.ShapeDtypeStruct((t, SSM_INNER), BF16),
        scratch_shapes=[pltpu.VMEM((tb + HALO, gw), F32),
                        pltpu.VMEM((tb + HALO, ns), F32),
                        pltpu.VMEM((tb + HALO, ns), F32),
                        pltpu.VMEM((SMALL_COLS, tb), F32),
                        pltpu.VMEM((ns, gw), F32)],
        compiler_params=_cparams(("parallel", "arbitrary")),
        name="ssd",
    )(p, p, p, p, small, conv_w, conv_w, conv_w, conv_b, conv_b, conv_b,
      dt_bias_s, a_log_s, d_skip_s, norm_g)


def _merge_body(a1_ref, a2_ref, w1_ref, w2_ref, gg_ref, gs_ref, o_ref, w1b_ref, w2b_ref):
    @pl.when(pl.program_id(1) == 0)
    def _():
        w1b_ref[...] = w1_ref[...].astype(BF16)
        w2b_ref[...] = w2_ref[...].astype(BF16)

    y1 = _dot(a1_ref[...], w1b_ref[...])
    y2 = _dot(a2_ref[...], w2b_ref[...])
    o_ref[...] = (_sigmoid(gg_ref[...].astype(F32)) * y1 + _sigmoid(gs_ref[...].astype(F32)) * y2).astype(BF16)


def _merge(o_gla, y_ssm, w1, w2, p):
    t, k1 = o_gla.shape
    k2 = y_ssm.shape[1]
    n = w1.shape[1]
    tm, tn = 512, 512
    return pl.pallas_call(
        _merge_body,
        grid=(n // tn, t // tm),
        in_specs=[pl.BlockSpec((tm, k1), lambda j, m: (m, 0)),
                  pl.BlockSpec((tm, k2), lambda j, m: (m, 0)),
                  pl.BlockSpec((k1, tn), lambda j, m: (0, j)),
                  pl.BlockSpec((k2, tn), lambda j, m: (0, j)),
                  pl.BlockSpec((tm, tn), lambda j, m: (m, COL_GG // tn + j)),
                  pl.BlockSpec((tm, tn), lambda j, m: (m, COL_GS // tn + j))],
        out_specs=pl.BlockSpec((tm, tn), lambda j, m: (m, j)),
        out_shape=jax.ShapeDtypeStruct((t, n), BF16),
        scratch_shapes=[pltpu.VMEM((k1, tn), BF16), pltpu.VMEM((k2, tn), BF16)],
        compiler_params=_cparams(("parallel", "arbitrary")),
        name="merge",
    )(o_gla, y_ssm, w1, w2, p, p)


def _outproj_body(m_ref, w_ref, x_ref, mod_ref, gpost_ref, gpre_ref, wr_ref, x1_ref, hf_ref, lg_ref):
    mix = _dot(m_ref[...], w_ref[...])
    x1 = x_ref[...] + mod_ref[2:3, :] * (_rms(mix) * gpost_ref[...])
    x1_ref[...] = x1
    h = _rms(x1) * gpre_ref[...] * (1.0 + mod_ref[4:5, :]) + mod_ref[3:4, :]
    for s in range(ROW_TILES):
        hf_ref[:, s, :] = h[:, s * LANES:(s + 1) * LANES]
    h_hi = h.astype(BF16)
    h_lo = (h - h_hi.astype(F32)).astype(BF16)
    wr = wr_ref[...]
    w_hi = wr.astype(BF16)
    w_lo = (wr - w_hi.astype(F32)).astype(BF16)
    lg_ref[...] = _dot(h_hi, w_hi) + _dot(h_hi, w_lo) + _dot(h_lo, w_hi)


def _outproj(merged, w_out, x2, mod8, g_post, g_pre, w_router):
    t, d = x2.shape
    tm = 256
    full = lambda r, c: pl.BlockSpec((r, c), lambda m: (0, 0))
    tile = lambda c: pl.BlockSpec((tm, c), lambda m: (m, 0))
    return pl.pallas_call(
        _outproj_body,
        grid=(t // tm,),
        in_specs=[tile(d), full(d, d), tile(d), full(8, d), full(1, d), full(1, d), full(d, 128)],
        out_specs=[tile(d), pl.BlockSpec((tm, ROW_TILES, LANES), lambda m: (m, 0, 0)), tile(128)],
        out_shape=[jax.ShapeDtypeStruct((t, d), F32),
                   jax.ShapeDtypeStruct((t, ROW_TILES, LANES), F32),
                   jax.ShapeDtypeStruct((t, 128), F32)],
        compiler_params=_cparams(("parallel",)),
        name="outproj",
    )(merged, w_out, x2, mod8, g_post, g_pre, w_router)


def _route_body(lg_ref, id_ref, w_ref):
    lg = lg_ref[...]
    lane = lax.broadcasted_iota(I32, lg.shape, 1)
    lane_f = lane.astype(F32)
    neg = jnp.float32(-jnp.inf)

    def first_argmax(vals, mx):
        return jnp.min(jnp.where(vals == mx, lane_f, 1e9), axis=-1, keepdims=True).astype(I32)

    gmask = lane < MOE_GROUPS
    gl = jnp.where(gmask, lg, neg)
    gmax = jnp.max(gl, axis=-1, keepdims=True)
    gsum = jnp.sum(jnp.where(gmask, jnp.exp(gl - gmax), 0.0), axis=-1, keepdims=True)
    g_w = 1.0 / gsum
    g_idx = first_argmax(gl, gmax)
    lo = MOE_GROUPS + g_idx * EXPERTS_PER_GROUP
    emask = (lane >= lo) & (lane < lo + EXPERTS_PER_GROUP)
    el = jnp.where(emask, lg, neg)
    m1 = jnp.max(el, axis=-1, keepdims=True)
    i1 = first_argmax(el, m1)
    el2 = jnp.where(lane == i1, neg, el)
    m2 = jnp.max(el2, axis=-1, keepdims=True)
    i2 = first_argmax(el2, m2)
    r = jnp.exp(m2 - m1)
    w1 = g_w / (1.0 + r)
    w2 = g_w * r / (1.0 + r)
    id_ref[...] = jnp.where(lane == 0, i1 - MOE_GROUPS, jnp.where(lane == 1, i2 - MOE_GROUPS, 0))
    w_ref[...] = jnp.where(lane == 0, w1, jnp.where(lane == 1, w2, 0.0))


def _route(logits):
    t = logits.shape[0]
    tm = 1024
    spec = pl.BlockSpec((tm, 128), lambda m: (m, 0))
    return pl.pallas_call(
        _route_body,
        grid=(t // tm,),
        in_specs=[spec],
        out_specs=[spec, spec],
        out_shape=[jax.ShapeDtypeStruct((t, 128), I32), jax.ShapeDtypeStruct((t, 128), F32)],
        compiler_params=_cparams(("parallel",)),
        name="route",
    )(logits)


GATHER_ROWS = 512


def _issue_rows(idx_ref, src_ref, buf_ref, slot, sem, skip_pads):
    def body(r, carry):
        tok = idx_ref[0, 0, r]
        copy = pltpu.make_async_copy(src_ref.at[jnp.maximum(tok, 0)], buf_ref.at[slot, :, r, :], sem.at[slot])
        if skip_pads:
            @pl.when(tok >= 0)
            def _():
                copy.start()
        else:
            copy.start()
        return carry

    lax.fori_loop(0, GATHER_ROWS, body, 0, unroll=8)


def _wait_rows(src_ref, buf_ref, slot, sem, count):
    del src_ref
    if count is None:
        pltpu.make_async_copy(buf_ref.at[slot], buf_ref.at[slot], sem.at[slot]).wait()
    else:
        @pl.when(count > 0)
        def _():
            part = buf_ref.at[slot, :, pl.ds(0, count), :]
            pltpu.make_async_copy(part, part, sem.at[slot]).wait()


def _gather_step(idx_ref, idx_next_ref, src_ref, buf_ref, sem, cnt_ref=None):
    i = pl.program_id(0)
    slot = i % 2
    skip_pads = cnt_ref is not None

    @pl.when(i == 0)
    def _():
        if skip_pads:
            buf_ref[...] = jnp.zeros_like(buf_ref)
        _issue_rows(idx_ref, src_ref, buf_ref, 0, sem, skip_pads)

    @pl.when(i + 1 < pl.num_programs(0))
    def _():
        _issue_rows(idx_next_ref, src_ref, buf_ref, 1 - slot, sem, skip_pads)

    _wait_rows(src_ref, buf_ref, slot, sem, cnt_ref[i] if skip_pads else None)
    return slot


def _gather_specs(nsteps):
    smem = lambda f: pl.BlockSpec((1, 1, GATHER_ROWS), f, memory_space=pltpu.SMEM)
    return [smem(lambda i: (i, 0, 0)),
            smem(lambda i: (jnp.minimum(i + 1, nsteps - 1), 0, 0)),
            pl.BlockSpec(memory_space=pl.ANY)]


GATHER_SCRATCH = [pltpu.VMEM((2, ROW_TILES, GATHER_ROWS, LANES), F32), pltpu.SemaphoreType.DMA((2,))]


def _dispatch_body(idx_ref, idx_next_ref, src_ref, cnt_ref, o_ref, buf_ref, sem):
    slot = _gather_step(idx_ref, idx_next_ref, src_ref, buf_ref, sem, cnt_ref)
    for c in range(ROW_TILES):
        o_ref[:, c * LANES:(c + 1) * LANES] = buf_ref[slot, c].astype(BF16)


def _dispatch(src, idx):
    n = idx.shape[0]
    nsteps = n // GATHER_ROWS
    idx3 = idx.reshape(nsteps, 1, GATHER_ROWS)
    counts = jnp.sum((idx3[:, 0, :] >= 0).astype(I32), axis=1)
    return pl.pallas_call(
        _dispatch_body,
        grid=(nsteps,),
        in_specs=_gather_specs(nsteps) + [pl.BlockSpec(memory_space=pltpu.SMEM)],
        out_specs=pl.BlockSpec((GATHER_ROWS, D_MODEL), lambda i: (i, 0)),
        out_shape=jax.ShapeDtypeStruct((n, D_MODEL), BF16),
        scratch_shapes=GATHER_SCRATCH,
        compiler_params=_cparams(("arbitrary",)),
        name="dispatch",
    )(idx3, idx3, src, counts)


MOE_FC = 512
MOE_J = MOE_FF // MOE_FC
N_ITEMS = (16384 // MOE_BLOCK + N_EXPERTS) // ITEM_BLOCKS + (N_EXPERTS * (ITEM_BLOCKS - 1)) // ITEM_BLOCKS


def _experts_body(ie_ref, io_ref, ins_ref, ifl_ref, nr_ref, wg_ref, wu_ref, wd_ref, xs_ref, ys_ref,
                  xb_ref, acc_ref, sem_in, sem_out):
    i = pl.program_id(0)
    j = pl.program_id(1)
    n_items = pl.num_programs(0)
    nsub = ins_ref[i]
    nfill = ifl_ref[i]
    slot = i % 2
    blk = MOE_BLOCK

    def row0(item):
        return pl.multiple_of(io_ref[item] * blk, blk)

    def x_copy(item, sl, s):
        return pltpu.make_async_copy(xs_ref.at[pl.ds(row0(item) + s * blk, blk)], xb_ref.at[sl, s],
                                     sem_in.at[sl, s])

    def start_y(item, s):
        for c in range(ROW_TILES):
            pltpu.make_async_copy(acc_ref.at[item % 2, s, :, pl.ds(c * LANES, LANES)],
                                  ys_ref.at[pl.ds(row0(item) + s * blk, blk), c, :], sem_out.at[s]).start()

    def wait_y(s):
        pltpu.make_async_copy(acc_ref.at[0, s], acc_ref.at[0, s], sem_out.at[s]).wait()

    def for_blocks(count, fn):
        for s in range(ITEM_BLOCKS):
            @pl.when(s < count)
            def _():
                fn(s)

    @pl.when(j == 0)
    def _():
        @pl.when(i == 0)
        def _():
            for_blocks(nsub, lambda s: x_copy(0, 0, s).start())

        for_blocks(nsub, lambda s: x_copy(i, slot, s).wait())

    @pl.when((j == MOE_J - 1) & (i + 1 < n_items))
    def _():
        nxt = jnp.minimum(i + 1, n_items - 1)
        for_blocks(ins_ref[nxt], lambda s: x_copy(nxt, 1 - slot, s).start())

    for n in range(1, ITEM_BLOCKS + 1):
        @pl.when(nsub == n)
        def _():
            x = xb_ref[slot, 0:n].reshape(n * blk, D_MODEL)
            gate = _dot(x, wg_ref[...].astype(BF16))
            up = _dot(x, wu_ref[...].astype(BF16))
            hid = (_silu(gate) * up).astype(BF16)
            y = _dot(hid, wd_ref[...].astype(BF16)).reshape(n, blk, D_MODEL)

            @pl.when(j == 0)
            def _():
                acc_ref[slot, 0:n] = y

            @pl.when(j > 0)
            def _():
                acc_ref[slot, 0:n] = acc_ref[slot, 0:n] + y

    def wait_prev_y():
        @pl.when(i > 0)
        def _():
            for_blocks(ins_ref[jnp.maximum(i - 1, 0)], wait_y)

    @pl.when((nsub > 0) & (j == MOE_J - 1))
    def _():
        wait_prev_y()
        for_blocks(nsub, lambda s: start_y(i, s))

        @pl.when(i == n_items - 1)
        def _():
            for_blocks(nsub, wait_y)

    @pl.when((nsub == 0) & (j == 0))
    def _():
        wait_prev_y()

        @pl.when(nfill > 0)
        def _():
            acc_ref[slot] = jnp.zeros(acc_ref.shape[1:], F32)
            for_blocks(nfill, lambda s: start_y(i, s))
            for_blocks(nfill, wait_y)


def _experts(x_sorted, w_gate, w_up, w_down, item_e, item_off, item_nsub, item_fill, n_real):
    n_rows = x_sorted.shape[0]
    d = D_MODEL

    def w_in_map(i, j, ie, io, ins, ifl, nr):
        return (ie[i], 0, jnp.where(i < nr[0], j, MOE_J - 1))

    def w_dn_map(i, j, ie, io, ins, ifl, nr):
        return (ie[i], jnp.where(i < nr[0], j, MOE_J - 1), 0)

    grid_spec = pltpu.PrefetchScalarGridSpec(
        num_scalar_prefetch=5,
        grid=(N_ITEMS, MOE_J),
        in_specs=[pl.BlockSpec((None, d, MOE_FC), w_in_map),
                  pl.BlockSpec((None, d, MOE_FC), w_in_map),
                  pl.BlockSpec((None, MOE_FC, d), w_dn_map),
                  pl.BlockSpec(memory_space=pl.ANY)],
        out_specs=pl.BlockSpec(memory_space=pl.ANY),
        scratch_shapes=[pltpu.VMEM((2, ITEM_BLOCKS, MOE_BLOCK, d), BF16),
                        pltpu.VMEM((2, ITEM_BLOCKS, MOE_BLOCK, d), F32),
                        pltpu.SemaphoreType.DMA((2, ITEM_BLOCKS)),
                        pltpu.SemaphoreType.DMA((ITEM_BLOCKS,))],
    )
    return pl.pallas_call(
        _experts_body,
        grid_spec=grid_spec,
        out_shape=jax.ShapeDtypeStruct((n_rows, ROW_TILES, LANES), F32),
        compiler_params=_cparams(("arbitrary", "arbitrary")),
        name="experts",
    )(item_e, item_off, item_nsub, item_fill, n_real, w_gate, w_up, w_down, x_sorted)


FINAL_TM = GATHER_ROWS // 2


def _final_body(idx_ref, idx_next_ref, ys_ref, w_ref, x1_ref, mod_ref, g_ref, o_ref, buf_ref, sem):
    slot = _gather_step(idx_ref, idx_next_ref, ys_ref, buf_ref, sem)
    tm = FINAL_TM
    w = w_ref[...]
    w0, w1 = w[:, 0:1], w[:, 1:2]
    ffn = jnp.concatenate([w0 * buf_ref[slot, c, 0:tm, :] + w1 * buf_ref[slot, c, tm:2 * tm, :]
                           for c in range(ROW_TILES)], axis=1)
    o_ref[...] = x1_ref[...] + mod_ref[5:6, :] * (_rms(ffn) * g_ref[...])


def _final(y_sorted, pos, wts, x1, mod8, g_post):
    t, d = x1.shape
    tm = FINAL_TM
    nt = t // tm
    idx3 = pos.reshape(nt, tm, 2).transpose(0, 2, 1).reshape(nt, 1, 2 * tm)
    return pl.pallas_call(
        _final_body,
        grid=(nt,),
        in_specs=_gather_specs(nt) + [pl.BlockSpec((tm, 128), lambda m: (m, 0)),
                                      pl.BlockSpec((tm, d), lambda m: (m, 0)),
                                      pl.BlockSpec((8, d), lambda m: (0, 0)),
                                      pl.BlockSpec((1, d), lambda m: (0, 0))],
        out_specs=pl.BlockSpec((tm, d), lambda m: (m, 0)),
        out_shape=jax.ShapeDtypeStruct((t, d), F32),
        scratch_shapes=GATHER_SCRATCH,
        compiler_params=_cparams(("arbitrary",)),
        name="final",
    )(idx3, idx3, y_sorted, wts, x1, mod8, g_post)


PLAN_TT = 512


def _plan_body(ids_ref, tril_ref, upper_ref, dest_ref, cnt_ref, run_ref):
    p = pl.program_id(0)
    i = pl.program_id(1)

    @pl.when((p == 0) & (i == 0))
    def _():
        run_ref[...] = jnp.zeros_like(run_ref)

    @pl.when((p == 1) & (i == 0))
    def _():
        counts = run_ref[...]
        cnt_ref[...] = counts.astype(I32)
        nblk = jnp.floor((counts + (MOE_BLOCK - 1.0)) * (1.0 / MOE_BLOCK))
        blk_start = _dot(nblk.astype(BF16), upper_ref[...])
        run_ref[...] = blk_start * MOE_BLOCK

    ids = ids_ref[...]
    lane = lax.broadcasted_iota(I32, ids.shape, 1)
    oh0 = lane == ids[:, 0:1]
    oh1 = lane == ids[:, 1:2]
    both = jnp.where(oh0 | oh1, 1.0, 0.0).astype(BF16)

    @pl.when(p == 1)
    def _():
        nxt = _dot(tril_ref[...], both) + run_ref[0:1, :]
        d0 = jnp.sum(jnp.where(oh0, nxt, 0.0), axis=-1, keepdims=True)
        d1 = jnp.sum(jnp.where(oh1, nxt, 0.0), axis=-1, keepdims=True)
        dest_ref[...] = jnp.where(lane == 0, d0, jnp.where(lane == 1, d1, 0.0)).astype(I32)

    run_ref[...] = run_ref[...] + _dot(jnp.ones((8, PLAN_TT), BF16), both)


def _plan(ids):
    t = ids.shape[0]
    r = np.arange(PLAN_TT)
    e = np.arange(128)
    strict_tril = jnp.asarray(r[:, None] > r[None, :], BF16)
    strict_upper = jnp.asarray(e[:, None] < e[None, :], BF16)
    return pl.pallas_call(
        _plan_body,
        grid=(2, t // PLAN_TT),
        in_specs=[pl.BlockSpec((PLAN_TT, 128), lambda p, i: (i, 0)),
                  pl.BlockSpec((PLAN_TT, PLAN_TT), lambda p, i: (0, 0)),
                  pl.BlockSpec((128, 128), lambda p, i: (0, 0))],
        out_specs=[pl.BlockSpec((PLAN_TT, 128), lambda p, i: (i * p, 0)),
                   pl.BlockSpec((8, 128), lambda p, i: (0, 0))],
        out_shape=[jax.ShapeDtypeStruct((t, 128), I32), jax.ShapeDtypeStruct((8, 128), I32)],
        scratch_shapes=[pltpu.VMEM((8, 128), F32)],
        compiler_params=_cparams(("arbitrary", "arbitrary")),
        name="plan",
    )(ids, strict_tril, strict_upper)


def _routing_tables(ids128, n_tok):
    n_assign = n_tok * 2
    n_blocks = n_assign // MOE_BLOCK + N_EXPERTS
    dest128, cnt = _plan(ids128)
    dest = dest128[:, :2].reshape(n_assign)
    counts = cnt[0, :N_EXPERTS]
    nb = (counts + MOE_BLOCK - 1) // MOE_BLOCK
    blk_start = jnp.cumsum(nb) - nb
    n_rows = n_blocks * MOE_BLOCK
    row_tok = jnp.full((n_rows,), -1, I32).at[dest].set(jnp.arange(n_assign, dtype=I32) // 2)
    pos = dest.reshape(n_tok, 2)

    n_it = (nb + ITEM_BLOCKS - 1) // ITEM_BLOCKS
    it_end = jnp.cumsum(n_it)
    it_start = it_end - n_it
    n_real = it_end[-1]
    i = jnp.arange(N_ITEMS, dtype=I32)
    e_i = jnp.minimum(jnp.searchsorted(it_end, i, side='right').astype(I32), N_EXPERTS - 1)
    k_i = i - it_start[e_i]
    valid = i < n_real
    last_e = e_i[jnp.maximum(n_real - 1, 0)]
    item_e = jnp.where(valid, e_i, last_e).astype(I32)
    fill_off = jnp.sum(nb) + ITEM_BLOCKS * (i - n_real)
    item_fill = jnp.where(valid, 0, jnp.clip(n_blocks - fill_off, 0, ITEM_BLOCKS)).astype(I32)
    item_off = jnp.where(valid, blk_start[e_i] + ITEM_BLOCKS * k_i, jnp.minimum(fill_off, n_blocks - 1)).astype(I32)
    item_nsub = jnp.where(valid, jnp.clip(nb[e_i] - ITEM_BLOCKS * k_i, 0, ITEM_BLOCKS), 0).astype(I32)
    return row_tok, pos, item_e, item_off, item_nsub, item_fill, n_real.reshape(1).astype(I32)


def _pad_lanes(v, start, total=SMALL_COLS):
    return jnp.zeros((1, total), F32).at[0, start:start + v.shape[0]].set(v)


def _layer(x2, c, w_ada, b_ada, norm_pre_mix, norm_post_mix, norm_pre_ffn, norm_post_ffn,
           w_in, gla_w_gate_up, gla_b_gate, gla_norm, ssm_conv_w, ssm_conv_b, ssm_dt_bias,
           ssm_a_log, ssm_d, ssm_norm, w_branch_gla, w_branch_ssm, w_out,
           router_group, router_expert, moe_w_gate, moe_w_up, moe_w_down):
    t, d = x2.shape
    row = lambda v: v.reshape(1, -1)

    mod = _ada(c, w_ada, b_ada)
    mod8 = jnp.concatenate([mod.reshape(6, d), jnp.zeros((2, d), F32)], axis=0)

    w_big, w_small = _repack(w_in.T)
    p, small = _inproj(x2, mod8, row(norm_pre_mix), w_big, w_small)

    o_gla = _gla(p, small, gla_w_gate_up, row(gla_b_gate), row(gla_norm))
    y_ssm = _ssd(p, small, ssm_conv_w, row(ssm_conv_b),
                 _pad_lanes(ssm_dt_bias, SMALL_DT0), _pad_lanes(ssm_a_log, SMALL_DT0),
                 _pad_lanes(ssm_d, SMALL_DT0), row(ssm_norm))
    merged = _merge(o_gla, y_ssm, w_branch_gla, w_branch_ssm, p)

    w_router = jnp.concatenate([router_group, router_expert,
                                jnp.zeros((d, 128 - MOE_GROUPS - N_EXPERTS), F32)], axis=1)
    x1, h2f, logits = _outproj(merged, w_out.astype(BF16), x2, mod8,
                               row(norm_post_mix), row(norm_pre_ffn), w_router)
    ids, wts = _route(logits)

    row_tok, pos, item_e, item_off, item_nsub, item_fill, n_real = _routing_tables(ids, t)
    x_sorted = _dispatch(h2f, row_tok)
    y_sorted = _experts(x_sorted, moe_w_gate, moe_w_up, moe_w_down,
                        item_e, item_off, item_nsub, item_fill, n_real)
    return _final(y_sorted, pos, wts, x1, mod8, row(norm_post_ffn))


def kernel(x, c, w_ada, b_ada, norm_pre_mix, norm_post_mix, norm_pre_ffn, norm_post_ffn, w_in, gla_w_gate_up, gla_b_gate, gla_norm, ssm_conv_w, ssm_conv_b, ssm_dt_bias, ssm_a_log, ssm_d, ssm_norm, w_branch_gla, w_branch_ssm, w_out, router_group, router_expert, moe_w_gate, moe_w_up, moe_w_down):
    bsz, seq, d = x.shape
    assert bsz == 1 and d == D_MODEL
    x2 = x.reshape(seq, d)
    params = (w_ada, b_ada, norm_pre_mix, norm_post_mix, norm_pre_ffn, norm_post_ffn, w_in, gla_w_gate_up,
              gla_b_gate, gla_norm, ssm_conv_w, ssm_conv_b, ssm_dt_bias, ssm_a_log, ssm_d, ssm_norm,
              w_branch_gla, w_branch_ssm, w_out, router_group, router_expert, moe_w_gate, moe_w_up, moe_w_down)
    for layer in range(w_ada.shape[0]):
        x2 = _layer(x2, c, *(prm[layer] for prm in params))
    return x2.reshape(bsz, seq, d)
```

```python
import functools

import jax
import jax.numpy as jnp
import numpy as np
from jax import lax
from jax.experimental import pallas as pl
from jax.experimental.pallas import tpu as pltpu

F32 = jnp.float32
BF16 = jnp.bfloat16
I32 = jnp.int32

D_MODEL = 2048
EPS = 1e-6
LANES = 128
ROW_TILES = D_MODEL // LANES

GLA_HEADS = 4
GLA_HEAD_K = 256
GLA_HEAD_V = 512
GLA_RANK = 16
GLA_NORMALIZER = 16.0
CHUNK = 64

SSM_GROUPS = 8
SSM_HEADS = 64
SSM_HEAD_DIM = 64
SSM_STATE = 128
SSM_CONV = 4
SSM_GROUP_W = 512
SSM_INNER = 4096

N_EXPERTS = 64
EXPERTS_PER_GROUP = 8
MOE_GROUPS = 8
MOE_FF = 1024
MOE_BLOCK = 128
ITEM_BLOCKS = 4

COL_Q, COL_K, COL_V, COL_OG, COL_Z, COL_XS, COL_B, COL_C, COL_GG, COL_GS = (
    0, 1024, 2048, 4096, 6144, 10240, 14336, 15360, 16384, 18432)
P_COLS = 20480
SMALL_COLS = 128
SMALL_DT0 = GLA_RANK

VMEM_LIMIT = 56 * 1024 * 1024


def _cparams(sem, vmem=VMEM_LIMIT):
    return pltpu.CompilerParams(dimension_semantics=sem, vmem_limit_bytes=vmem)


def _dot(a, b):
    return jnp.dot(a, b, preferred_element_type=F32)


def _dot_nt(a, b):
    return lax.dot_general(a, b, (((1,), (1,)), ((), ())), preferred_element_type=F32)


def _dot_tn(a, b):
    return lax.dot_general(a, b, (((0,), (0,)), ((), ())), preferred_element_type=F32)


def _split3(a):
    hi = a.astype(BF16)
    r1 = a - hi.astype(F32)
    mid = r1.astype(BF16)
    lo = (r1 - mid.astype(F32)).astype(BF16)
    return hi, mid, lo


def _dot_sel_r(a, sel):
    hi, mid, lo = _split3(a)
    return _dot(hi, sel) + _dot(mid, sel) + _dot(lo, sel)


def _dot_sel_l(sel, a):
    hi, mid, lo = _split3(a)
    return _dot(sel, hi) + _dot(sel, mid) + _dot(sel, lo)


def _sigmoid(x):
    return 1.0 / (1.0 + jnp.exp(-x))


def _silu(x):
    return x * _sigmoid(x)


def _softplus(x):
    return jnp.maximum(x, 0.0) + jnp.log1p(jnp.exp(-jnp.abs(x)))


def _log_sigmoid(x):
    return jnp.minimum(x, 0.0) - jnp.log1p(jnp.exp(-jnp.abs(x)))


def _rms(x):
    return x * lax.rsqrt(jnp.mean(x * x, axis=-1, keepdims=True) + EPS)


def _ada_body(c_ref, w_ref, b_ref, o_ref):
    c = c_ref[...]
    s = jnp.broadcast_to(_silu(c), (8, c.shape[1])).astype(BF16)
    o_ref[...] = _dot(s, w_ref[...].astype(BF16))[0:1] + b_ref[...]


def _ada(c, w_ada, b_ada):
    d, n = w_ada.shape
    tn = 1024
    return pl.pallas_call(
        _ada_body,
        grid=(n // tn,),
        in_specs=[pl.BlockSpec((1, d), lambda j: (0, 0)),
                  pl.BlockSpec((d, tn), lambda j: (0, j)),
                  pl.BlockSpec((1, tn), lambda j: (0, j))],
        out_specs=pl.BlockSpec((1, tn), lambda j: (0, j)),
        out_shape=jax.ShapeDtypeStruct((1, n), F32),
        compiler_params=_cparams(("parallel",)),
        name="ada",
    )(c, w_ada, b_ada.reshape(1, n))


W_GLR, W_MID, W_DT, W_GATES, W_END = 4096, 4112, 16400, 16464, 20560
REPACK_ROWS = 1024
REPACK_TAIL = 128


def _repack_body(a_ref, b_ref, big_ref, small_ref):
    j = pl.program_id(0)
    n = REPACK_ROWS

    def emit(shift):
        if shift == 0:
            big_ref[...] = a_ref[...].astype(BF16)
        else:
            big_ref[0:n - shift, :] = a_ref[shift:n, :].astype(BF16)
            big_ref[n - shift:n, :] = b_ref[0:shift, :].astype(BF16)

    first_mid, first_gates = COL_OG // n, COL_GG // n

    @pl.when(j < first_mid)
    def _():
        emit(0)

    @pl.when((j >= first_mid) & (j < first_gates))
    def _():
        emit(W_MID - COL_OG)

    @pl.when(j >= first_gates)
    def _():
        emit(W_GATES - COL_GG)

    @pl.when(j == 0)
    def _():
        small_ref[...] = jnp.zeros_like(small_ref)

    @pl.when(j == W_GLR // n)
    def _():
        small_ref[0:GLA_RANK, :] = a_ref[0:GLA_RANK, :].astype(BF16)

    @pl.when(j == W_DT // n)
    def _():
        small_ref[SMALL_DT0:SMALL_DT0 + SSM_HEADS, :] = a_ref[SMALL_DT0:SMALL_DT0 + SSM_HEADS, :].astype(BF16)


def _repack(w_t):
    n_in, d = w_t.shape
    n = REPACK_ROWS
    assert n_in == W_END and W_GLR % n == 0 and W_GLR + GLA_RANK == W_MID and W_DT % n == SMALL_DT0
    assert COL_OG % n == 0 and COL_GG % n == 0 and W_GATES - COL_GG <= REPACK_TAIL
    return pl.pallas_call(
        _repack_body,
        grid=(P_COLS // n,),
        in_specs=[pl.BlockSpec((n, d), lambda j: (j, 0)),
                  pl.BlockSpec((REPACK_TAIL, d), lambda j: ((j + 1) * (n // REPACK_TAIL), 0))],
        out_specs=[pl.BlockSpec((n, d), lambda j: (j, 0)),
                   pl.BlockSpec((SMALL_COLS, d), lambda j: (0, 0))],
        out_shape=[jax.ShapeDtypeStruct((P_COLS, d), BF16), jax.ShapeDtypeStruct((SMALL_COLS, d), BF16)],
        compiler_params=_cparams(("arbitrary",)),
        name="repack",
    )(w_t, w_t)


def _inproj_body(x_ref, mod_ref, g_ref, w_ref, ws_ref, p_ref, s_ref, h_ref):
    @pl.when(pl.program_id(1) == 0)
    def _():
        h = _rms(x_ref[...]) * g_ref[...] * (1.0 + mod_ref[1:2, :]) + mod_ref[0:1, :]
        hb = h.astype(BF16)
        h_ref[...] = hb
        s_ref[...] = _dot_nt(hb, ws_ref[...])

    p_ref[...] = _dot_nt(h_ref[...], w_ref[...]).astype(BF16)


def _inproj(x2, mod8, gain, w_big, w_small):
    t, d = x2.shape
    n = w_big.shape[0]
    tm, tn = 1024, 1024
    return pl.pallas_call(
        _inproj_body,
        grid=(t // tm, n // tn),
        in_specs=[pl.BlockSpec((tm, d), lambda m, j: (m, 0)),
                  pl.BlockSpec((8, d), lambda m, j: (0, 0)),
                  pl.BlockSpec((1, d), lambda m, j: (0, 0)),
                  pl.BlockSpec((tn, d), lambda m, j: (j, 0)),
                  pl.BlockSpec((SMALL_COLS, d), lambda m, j: (0, 0))],
        out_specs=[pl.BlockSpec((tm, tn), lambda m, j: (m, j)),
                   pl.BlockSpec((tm, SMALL_COLS), lambda m, j: (m, 0))],
        out_shape=[jax.ShapeDtypeStruct((t, n), BF16),
                   jax.ShapeDtypeStruct((t, SMALL_COLS), F32)],
        scratch_shapes=[pltpu.VMEM((tm, d), BF16)],
        compiler_params=_cparams(("parallel", "arbitrary")),
        name="inproj",
    )(x2, mod8, gain, w_big, w_small)


GLA_TB = 512


def _gla_body(q_ref, k_ref, v_ref, og_ref, sm_ref, wup_ref, bg_ref, gn_ref, o_ref, st_ref):
    @pl.when(pl.program_id(1) == 0)
    def _():
        st_ref[...] = jnp.zeros_like(st_ref)

    tb, C = GLA_TB, CHUNK
    nch = tb // C
    r = lax.broadcasted_iota(I32, (tb, tb), 0)
    c = lax.broadcasted_iota(I32, (tb, tb), 1)
    causal = (r // C == c // C) & (r >= c)
    r2 = lax.broadcasted_iota(I32, (2 * C, 2 * C), 0)
    c2 = lax.broadcasted_iota(I32, (2 * C, 2 * C), 1)
    tril2 = ((r2 // C == c2 // C) & (r2 >= c2)).astype(BF16)

    glr = sm_ref[:, 0:GLA_RANK].astype(BF16)
    pre = _dot(glr, wup_ref[...].astype(BF16)) + bg_ref[...]
    log_a = _log_sigmoid(pre) / GLA_NORMALIZER
    b = jnp.concatenate([_dot_sel_l(tril2, log_a[i * 2 * C:(i + 1) * 2 * C]) for i in range(nch // 2)],
                        axis=0)
    b_last = [b[(i + 1) * C - 1:(i + 1) * C, :] for i in range(nch)]
    b_end = jnp.concatenate([jnp.broadcast_to(bl, (C, bl.shape[1])) for bl in b_last], axis=0)

    q = q_ref[...].astype(F32) * (GLA_HEAD_K ** -0.5)
    k = k_ref[...].astype(F32)
    v = v_ref[...]
    q_dec = (q * jnp.exp(b)).astype(BF16)
    k_inv = (k * jnp.exp(-b)).astype(BF16)
    k_end = (k * jnp.exp(b_end - b)).astype(BF16)
    scores = jnp.where(causal, _dot_nt(q_dec, k_inv), 0.0)
    o_intra = _dot(scores.astype(BF16), v)

    st = st_ref[...]
    o_inter = []
    for i in range(nch):
        rows = slice(i * C, (i + 1) * C)
        o_inter.append(_dot_nt(q_dec[rows], st.astype(BF16)))
        st = st * jnp.exp(b_last[i]) + _dot_tn(v[rows], k_end[rows])
    st_ref[...] = st

    o = o_intra + jnp.concatenate(o_inter, axis=0)
    og = og_ref[...].astype(F32)
    o_ref[...] = (_rms(o) * gn_ref[...] * _silu(og)).astype(BF16)


def _gla(p, small, wup, bg, gn):
    t = p.shape[0]
    tb = GLA_TB
    kb, vb = GLA_HEAD_K, GLA_HEAD_V
    return pl.pallas_call(
        _gla_body,
        grid=(GLA_HEADS, t // tb),
        in_specs=[pl.BlockSpec((tb, kb), lambda h, i: (i, COL_Q // kb + h)),
                  pl.BlockSpec((tb, kb), lambda h, i: (i, COL_K // kb + h)),
                  pl.BlockSpec((tb, vb), lambda h, i: (i, COL_V // vb + h)),
                  pl.BlockSpec((tb, vb), lambda h, i: (i, COL_OG // vb + h)),
                  pl.BlockSpec((tb, SMALL_COLS), lambda h, i: (i, 0)),
                  pl.BlockSpec((GLA_RANK, kb), lambda h, i: (0, h)),
                  pl.BlockSpec((1, kb), lambda h, i: (0, h)),
                  pl.BlockSpec((1, vb), lambda h, i: (0, h))],
        out_specs=pl.BlockSpec((tb, vb), lambda h, i: (i, h)),
        out_shape=jax.ShapeDtypeStruct((t, GLA_HEADS * vb), BF16),
        scratch_shapes=[pltpu.VMEM((vb, kb), F32)],
        compiler_params=_cparams(("parallel", "arbitrary")),
        name="gla",
    )(p, p, p, p, small, wup, bg, gn)


SSD_TB = 512
HALO = 8


def _ssd_body(xs_ref, b_ref, c_ref, z_ref, sm_ref, cwx_ref, cwb_ref, cwc_ref, cbx_ref, cbb_ref, cbc_ref,
              dtb_ref, alog_ref, dsk_ref, ng_ref, o_ref,
              ex_ref, eb_ref, ec_ref, at_ref, ht_ref):
    g = pl.program_id(0)
    t = pl.program_id(1)
    tb = SSD_TB
    gw = SSM_GROUP_W

    @pl.when(t == 0)
    def _():
        ht_ref[...] = jnp.zeros_like(ht_ref)
        ex_ref[0:HALO, :] = jnp.zeros((HALO, gw), F32)
        eb_ref[0:HALO, :] = jnp.zeros((HALO, SSM_STATE), F32)
        ec_ref[0:HALO, :] = jnp.zeros((HALO, SSM_STATE), F32)

    @pl.when(t > 0)
    def _():
        ex_ref[0:HALO, :] = ex_ref[tb:tb + HALO, :]
        eb_ref[0:HALO, :] = eb_ref[tb:tb + HALO, :]
        ec_ref[0:HALO, :] = ec_ref[tb:tb + HALO, :]

    def conv_silu(u_ref, e_ref, w_ref, bias_ref):
        e_ref[HALO:HALO + tb, :] = u_ref[...].astype(F32)
        acc = bias_ref[...] + w_ref[0:1, :] * e_ref[HALO - 3:HALO - 3 + tb, :]
        for kk in range(1, SSM_CONV):
            acc = acc + w_ref[kk:kk + 1, :] * e_ref[HALO - 3 + kk:HALO - 3 + kk + tb, :]
        return _silu(acc)

    xa = conv_silu(xs_ref, ex_ref, cwx_ref, cbx_ref)
    ba = conv_silu(b_ref, eb_ref, cwb_ref, cbb_ref).astype(BF16)
    ca = conv_silu(c_ref, ec_ref, cwc_ref, cbc_ref).astype(BF16)

    L = CHUNK
    nch = tb // L
    rep = gw // L
    hpg = gw // SSM_HEAD_DIM
    head0 = pl.multiple_of(SMALL_DT0 + g * hpg, hpg)

    e_row = lax.broadcasted_iota(I32, (SMALL_COLS, gw), 0)
    e_col = lax.broadcasted_iota(I32, (SMALL_COLS, gw), 1)
    expand = (e_row == head0 + e_col // SSM_HEAD_DIM).astype(BF16)

    dt_small = _softplus(sm_ref[...] + dtb_ref[...])
    adt_small = dt_small * (-jnp.exp(alog_ref[...]))
    rb = lax.broadcasted_iota(I32, (tb, tb), 0)
    cb = lax.broadcasted_iota(I32, (tb, tb), 1)
    blocktril = ((rb // L == cb // L) & (rb >= cb)).astype(BF16)
    acum_small = _dot_sel_l(blocktril, adt_small)
    at_ref[...] = acum_small.T
    heads = at_ref[pl.ds(head0, hpg), :]
    dt_exp = _dot_sel_r(dt_small, expand)
    acum = _dot_sel_r(acum_small, expand)
    d_exp = _dot_sel_r(jnp.broadcast_to(dsk_ref[...], (8, SMALL_COLS)), expand)[0:1]
    xdt = xa * dt_exp
    xdt_b = xdt.astype(BF16)
    e_acum = jnp.exp(acum)

    li = lax.broadcasted_iota(I32, (L, gw), 0)
    lj = lax.broadcasted_iota(I32, (L, gw), 1) % L
    causal_t = li >= lj
    hi = lax.broadcasted_iota(I32, (hpg, gw), 0)
    hj = lax.broadcasted_iota(I32, (hpg, gw), 1) // L
    headmask = hi == hj
    ones_h = jnp.ones((L, hpg), BF16)
    bi = lax.broadcasted_iota(I32, (gw, gw), 0) // L
    bj = lax.broadcasted_iota(I32, (gw, gw), 1) // SSM_HEAD_DIM
    blockmask = bi == bj
    masked_out = -1e30

    ht = ht_ref[...]
    for c in range(nch):
        rows = slice(c * L, (c + 1) * L)
        acum_c = acum[rows]
        a_rows = jnp.concatenate([heads[:, c * L:(c + 1) * L]] * rep, axis=1)
        rterm = _dot_sel_l(ones_h, jnp.where(headmask, a_rows, 0.0))
        decay = jnp.exp(jnp.where(causal_t, acum_c - rterm, masked_out))
        cc = ca[rows]
        bc = ba[rows]
        cb_t = _dot_nt(cc, jnp.concatenate([bc] * rep, axis=0))
        m = (cb_t * decay).astype(BF16)
        bd = jnp.where(blockmask, jnp.concatenate([xdt_b[rows]] * rep, axis=0), jnp.zeros((), BF16))
        y_diag = _dot(m, bd)
        y_off = _dot(cc, ht.astype(BF16)) * e_acum[rows]
        a_last = acum_c[L - 1:L, :]
        xd = (xdt[rows] * jnp.exp(a_last - acum_c)).astype(BF16)
        ht = ht * jnp.exp(a_last) + _dot_tn(bc, xd)
        y = y_diag + y_off + d_exp * xa[rows]
        y = y * _silu(z_ref[rows, :].astype(F32))
        o_ref[rows, :] = (_rms(y) * ng_ref[...]).astype(BF16)
    ht_ref[...] = ht


def _ssd(p, small, conv_w, conv_b, dt_bias_s, a_log_s, d_skip_s, norm_g):
    t = p.shape[0]
    tb = SSD_TB
    gw, ns = SSM_GROUP_W, SSM_STATE
    xs0, b0, c0 = 0, SSM_INNER // ns, (SSM_INNER + SSM_GROUPS * ns) // ns
    row = lambda w, off: pl.BlockSpec((1, w), lambda g, i: (0, off + g))
    return pl.pallas_call(
        _ssd_body,
        grid=(SSM_GROUPS, t // tb),
        in_specs=[pl.BlockSpec((tb, gw), lambda g, i: (i, COL_XS // gw + g)),
                  pl.BlockSpec((tb, ns), lambda g, i: (i, COL_B // ns + g)),
                  pl.BlockSpec((tb, ns), lambda g, i: (i, COL_C // ns + g)),
                  pl.BlockSpec((tb, gw), lambda g, i: (i, COL_Z // gw + g)),
                  pl.BlockSpec((tb, SMALL_COLS), lambda g, i: (i, 0)),
                  pl.BlockSpec((SSM_CONV, gw), lambda g, i: (0, xs0 + g)),
                  pl.BlockSpec((SSM_CONV, ns), lambda g, i: (0, b0 + g)),
                  pl.BlockSpec((SSM_CONV, ns), lambda g, i: (0, c0 + g)),
                  row(gw, xs0), row(ns, b0), row(ns, c0),
                  pl.BlockSpec((1, SMALL_COLS), lambda g, i: (0, 0)),
                  pl.BlockSpec((1, SMALL_COLS), lambda g, i: (0, 0)),
                  pl.BlockSpec((1, SMALL_COLS), lambda g, i: (0, 0)),
                  pl.BlockSpec((1, gw), lambda g, i: (0, g))],
        out_specs=pl.BlockSpec((tb, gw), lambda g, i: (i, g)),
        out_shape=jax.ShapeDtypeStruct((t, SSM_INNER), BF16),
        scratch_shapes=[pltpu.VMEM((tb + HALO, gw), F32),
                        pltpu.VMEM((tb + HALO, ns), F32),
                        pltpu.VMEM((tb + HALO, ns), F32),
                        pltpu.VMEM((SMALL_COLS, tb), F32),
                        pltpu.VMEM((ns, gw), F32)],
        compiler_params=_cparams(("parallel", "arbitrary")),
        name="ssd",
    )(p, p, p, p, small, conv_w, conv_w, conv_w, conv_b, conv_b, conv_b,
      dt_bias_s, a_log_s, d_skip_s, norm_g)


def _merge_body(a1_ref, a2_ref, w1_ref, w2_ref, gg_ref, gs_ref, o_ref, w1b_ref, w2b_ref):
    @pl.when(pl.program_id(1) == 0)
    def _():
        w1b_ref[...] = w1_ref[...].astype(BF16)
        w2b_ref[...] = w2_ref[...].astype(BF16)

    y1 = _dot(a1_ref[...], w1b_ref[...])
    y2 = _dot(a2_ref[...], w2b_ref[...])
    o_ref[...] = (_sigmoid(gg_ref[...].astype(F32)) * y1 + _sigmoid(gs_ref[...].astype(F32)) * y2).astype(BF16)


def _merge(o_gla, y_ssm, w1, w2, p):
    t, k1 = o_gla.shape
    k2 = y_ssm.shape[1]
    n = w1.shape[1]
    tm, tn = 512, 512
    return pl.pallas_call(
        _merge_body,
        grid=(n // tn, t // tm),
        in_specs=[pl.BlockSpec((tm, k1), lambda j, m: (m, 0)),
                  pl.BlockSpec((tm, k2), lambda j, m: (m, 0)),
                  pl.BlockSpec((k1, tn), lambda j, m: (0, j)),
                  pl.BlockSpec((k2, tn), lambda j, m: (0, j)),
                  pl.BlockSpec((tm, tn), lambda j, m: (m, COL_GG // tn + j)),
                  pl.BlockSpec((tm, tn), lambda j, m: (m, COL_GS // tn + j))],
        out_specs=pl.BlockSpec((tm, tn), lambda j, m: (m, j)),
        out_shape=jax.ShapeDtypeStruct((t, n), BF16),
        scratch_shapes=[pltpu.VMEM((k1, tn), BF16), pltpu.VMEM((k2, tn), BF16)],
        compiler_params=_cparams(("parallel", "arbitrary")),
        name="merge",
    )(o_gla, y_ssm, w1, w2, p, p)


def _outproj_body(m_ref, w_ref, x_ref, mod_ref, gpost_ref, gpre_ref, wr_ref, x1_ref, hf_ref, lg_ref):
    mix = _dot(m_ref[...], w_ref[...])
    x1 = x_ref[...] + mod_ref[2:3, :] * (_rms(mix) * gpost_ref[...])
    x1_ref[...] = x1
    h = _rms(x1) * gpre_ref[...] * (1.0 + mod_ref[4:5, :]) + mod_ref[3:4, :]
    for s in range(ROW_TILES):
        hf_ref[:, s, :] = h[:, s * LANES:(s + 1) * LANES]
    h_hi = h.astype(BF16)
    h_lo = (h - h_hi.astype(F32)).astype(BF16)
    wr = wr_ref[...]
    w_hi = wr.astype(BF16)
    w_lo = (wr - w_hi.astype(F32)).astype(BF16)
    lg_ref[...] = _dot(h_hi, w_hi) + _dot(h_hi, w_lo) + _dot(h_lo, w_hi)


def _outproj(merged, w_out, x2, mod8, g_post, g_pre, w_router):
    t, d = x2.shape
    tm = 256
    full = lambda r, c: pl.BlockSpec((r, c), lambda m: (0, 0))
    tile = lambda c: pl.BlockSpec((tm, c), lambda m: (m, 0))
    return pl.pallas_call(
        _outproj_body,
        grid=(t // tm,),
        in_specs=[tile(d), full(d, d), tile(d), full(8, d), full(1, d), full(1, d), full(d, 128)],
        out_specs=[tile(d), pl.BlockSpec((tm, ROW_TILES, LANES), lambda m: (m, 0, 0)), tile(128)],
        out_shape=[jax.ShapeDtypeStruct((t, d), F32),
                   jax.ShapeDtypeStruct((t, ROW_TILES, LANES), F32),
                   jax.ShapeDtypeStruct((t, 128), F32)],
        compiler_params=_cparams(("parallel",)),
        name="outproj",
    )(merged, w_out, x2, mod8, g_post, g_pre, w_router)


def _route_body(lg_ref, id_ref, w_ref):
    lg = lg_ref[...]
    lane = lax.broadcasted_iota(I32, lg.shape, 1)
    lane_f = lane.astype(F32)
    neg = jnp.float32(-jnp.inf)

    def first_argmax(vals, mx):
        return jnp.min(jnp.where(vals == mx, lane_f, 1e9), axis=-1, keepdims=True).astype(I32)

    gmask = lane < MOE_GROUPS
    gl = jnp.where(gmask, lg, neg)
    gmax = jnp.max(gl, axis=-1, keepdims=True)
    gsum = jnp.sum(jnp.where(gmask, jnp.exp(gl - gmax), 0.0), axis=-1, keepdims=True)
    g_w = 1.0 / gsum
    g_idx = first_argmax(gl, gmax)
    lo = MOE_GROUPS + g_idx * EXPERTS_PER_GROUP
    emask = (lane >= lo) & (lane < lo + EXPERTS_PER_GROUP)
    el = jnp.where(emask, lg, neg)
    m1 = jnp.max(el, axis=-1, keepdims=True)
    i1 = first_argmax(el, m1)
    el2 = jnp.where(lane == i1, neg, el)
    m2 = jnp.max(el2, axis=-1, keepdims=True)
    i2 = first_argmax(el2, m2)
    r = jnp.exp(m2 - m1)
    w1 = g_w / (1.0 + r)
    w2 = g_w * r / (1.0 + r)
    id_ref[...] = jnp.where(lane == 0, i1 - MOE_GROUPS, jnp.where(lane == 1, i2 - MOE_GROUPS, 0))
    w_ref[...] = jnp.where(lane == 0, w1, jnp.where(lane == 1, w2, 0.0))


def _route(logits):
    t = logits.shape[0]
    tm = 1024
    spec = pl.BlockSpec((tm, 128), lambda m: (m, 0))
    return pl.pallas_call(
        _route_body,
        grid=(t // tm,),
        in_specs=[spec],
        out_specs=[spec, spec],
        out_shape=[jax.ShapeDtypeStruct((t, 128), I32), jax.ShapeDtypeStruct((t, 128), F32)],
        compiler_params=_cparams(("parallel",)),
        name="route",
    )(logits)


GATHER_ROWS = 512


def _issue_rows(idx_ref, src_ref, buf_ref, slot, sem, skip_pads):
    def body(r, carry):
        tok = idx_ref[0, 0, r]
        copy = pltpu.make_async_copy(src_ref.at[jnp.maximum(tok, 0)], buf_ref.at[slot, :, r, :], sem.at[slot])
        if skip_pads:
            @pl.when(tok >= 0)
            def _():
                copy.start()
        else:
            copy.start()
        return carry

    lax.fori_loop(0, GATHER_ROWS, body, 0, unroll=8)


def _wait_rows(src_ref, buf_ref, slot, sem, count):
    del src_ref
    if count is None:
        pltpu.make_async_copy(buf_ref.at[slot], buf_ref.at[slot], sem.at[slot]).wait()
    else:
        @pl.when(count > 0)
        def _():
            part = buf_ref.at[slot, :, pl.ds(0, count), :]
            pltpu.make_async_copy(part, part, sem.at[slot]).wait()


def _gather_step(idx_ref, idx_next_ref, src_ref, buf_ref, sem, cnt_ref=None):
    i = pl.program_id(0)
    slot = i % 2
    skip_pads = cnt_ref is not None

    @pl.when(i == 0)
    def _():
        if skip_pads:
            buf_ref[...] = jnp.zeros_like(buf_ref)
        _issue_rows(idx_ref, src_ref, buf_ref, 0, sem, skip_pads)

    @pl.when(i + 1 < pl.num_programs(0))
    def _():
        _issue_rows(idx_next_ref, src_ref, buf_ref, 1 - slot, sem, skip_pads)

    _wait_rows(src_ref, buf_ref, slot, sem, cnt_ref[i] if skip_pads else None)
    return slot


def _gather_specs(nsteps):
    smem = lambda f: pl.BlockSpec((1, 1, GATHER_ROWS), f, memory_space=pltpu.SMEM)
    return [smem(lambda i: (i, 0, 0)),
            smem(lambda i: (jnp.minimum(i + 1, nsteps - 1), 0, 0)),
            pl.BlockSpec(memory_space=pl.ANY)]


GATHER_SCRATCH = [pltpu.VMEM((2, ROW_TILES, GATHER_ROWS, LANES), F32), pltpu.SemaphoreType.DMA((2,))]


MOE_FC = 512
MOE_J = MOE_FF // MOE_FC
N_ITEMS = (16384 // MOE_BLOCK + N_EXPERTS) // ITEM_BLOCKS + (N_EXPERTS * (ITEM_BLOCKS - 1)) // ITEM_BLOCKS


def _experts_body(ie_ref, io_ref, ins_ref, ifl_ref, nr_ref, wg_ref, wu_ref, wd_ref, h_ref,
                  f0_ref, f1_ref, f2_ref, f3_ref, n0_ref, n1_ref, n2_ref, n3_ref, ys_ref,
                  xst_ref, xb_ref, acc_ref, sem_in, sem_out):
    i = pl.program_id(0)
    j = pl.program_id(1)
    n_items = pl.num_programs(0)
    nsub = ins_ref[i]
    nfill = ifl_ref[i]
    slot = i % 2
    blk = MOE_BLOCK

    def row0(item):
        return pl.multiple_of(io_ref[item] * blk, blk)

    def start_rows(idx_refs, count, sl):
        def block(s):
            def body(r, carry):
                pltpu.make_async_copy(h_ref.at[idx_refs[s][0, 0, r]], xst_ref.at[sl, :, s * blk + r, :],
                                      sem_in.at[sl]).start()
                return carry

            lax.fori_loop(0, blk, body, 0, unroll=8)

        for_blocks(count, block)

    def wait_rows(count, sl):
        @pl.when(count > 0)
        def _():
            part = xst_ref.at[sl, :, pl.ds(0, count * blk), :]
            pltpu.make_async_copy(part, part, sem_in.at[sl]).wait()

    def start_y(item, s):
        for c in range(ROW_TILES):
            pltpu.make_async_copy(acc_ref.at[item % 2, s, :, pl.ds(c * LANES, LANES)],
                                  ys_ref.at[pl.ds(row0(item) + s * blk, blk), c, :], sem_out.at[s]).start()

    def wait_y(s):
        pltpu.make_async_copy(acc_ref.at[0, s], acc_ref.at[0, s], sem_out.at[s]).wait()

    def for_blocks(count, fn):
        for s in range(ITEM_BLOCKS):
            @pl.when(s < count)
            def _():
                fn(s)

    @pl.when(j == 0)
    def _():
        @pl.when(i == 0)
        def _():
            start_rows((f0_ref, f1_ref, f2_ref, f3_ref), nsub, 0)

        wait_rows(nsub, slot)

        def to_bf16(s):
            for c in range(ROW_TILES):
                xb_ref[s, :, c * LANES:(c + 1) * LANES] = xst_ref[slot, c, s * blk:(s + 1) * blk, :].astype(BF16)

        for_blocks(nsub, to_bf16)

    @pl.when((j == MOE_J - 1) & (i + 1 < n_items))
    def _():
        start_rows((n0_ref, n1_ref, n2_ref, n3_ref), ins_ref[jnp.minimum(i + 1, n_items - 1)], 1 - slot)

    for n in range(1, ITEM_BLOCKS + 1):
        @pl.when(nsub == n)
        def _():
            x = xb_ref[0:n].reshape(n * blk, D_MODEL)
            gate = _dot(x, wg_ref[...].astype(BF16))
            up = _dot(x, wu_ref[...].astype(BF16))
            hid = (_silu(gate) * up).astype(BF16)
            y = _dot(hid, wd_ref[...].astype(BF16)).reshape(n, blk, D_MODEL)

            @pl.when(j == 0)
            def _():
                acc_ref[slot, 0:n] = y

            @pl.when(j > 0)
            def _():
                acc_ref[slot, 0:n] = acc_ref[slot, 0:n] + y

    def wait_prev_y():
        @pl.when(i > 0)
        def _():
            for_blocks(ins_ref[jnp.maximum(i - 1, 0)], wait_y)

    @pl.when((nsub > 0) & (j == MOE_J - 1))
    def _():
        wait_prev_y()
        for_blocks(nsub, lambda s: start_y(i, s))

        @pl.when(i == n_items - 1)
        def _():
            for_blocks(nsub, wait_y)

    @pl.when((nsub == 0) & (j == 0))
    def _():
        wait_prev_y()

        @pl.when(nfill > 0)
        def _():
            acc_ref[slot] = jnp.zeros(acc_ref.shape[1:], F32)
            for_blocks(nfill, lambda s: start_y(i, s))
            for_blocks(nfill, wait_y)


def _experts(h_rows, row_tok, w_gate, w_up, w_down, item_e, item_off, item_nsub, item_fill, n_real):
    n_rows = row_tok.shape[0]
    n_blocks = n_rows // MOE_BLOCK
    d = D_MODEL

    def w_in_map(i, j, ie, io, ins, ifl, nr):
        return (ie[i], 0, jnp.where(i < nr[0], j, MOE_J - 1))

    def w_dn_map(i, j, ie, io, ins, ifl, nr):
        return (ie[i], jnp.where(i < nr[0], j, MOE_J - 1), 0)

    def tok_spec(s, next_item):
        def index_map(i, j, ie, io, ins, ifl, nr):
            item = jnp.minimum(i + 1, N_ITEMS - 1) if next_item else 0
            return (jnp.minimum(io[item] + s, n_blocks - 1), 0, 0)

        return pl.BlockSpec((1, 1, MOE_BLOCK), index_map, memory_space=pltpu.SMEM)

    grid_spec = pltpu.PrefetchScalarGridSpec(
        num_scalar_prefetch=5,
        grid=(N_ITEMS, MOE_J),
        in_specs=[pl.BlockSpec((None, d, MOE_FC), w_in_map),
                  pl.BlockSpec((None, d, MOE_FC), w_in_map),
                  pl.BlockSpec((None, MOE_FC, d), w_dn_map),
                  pl.BlockSpec(memory_space=pl.ANY)]
                 + [tok_spec(s, False) for s in range(ITEM_BLOCKS)]
                 + [tok_spec(s, True) for s in range(ITEM_BLOCKS)],
        out_specs=pl.BlockSpec(memory_space=pl.ANY),
        scratch_shapes=[pltpu.VMEM((2, ROW_TILES, ITEM_BLOCKS * MOE_BLOCK, LANES), F32),
                        pltpu.VMEM((ITEM_BLOCKS, MOE_BLOCK, d), BF16),
                        pltpu.VMEM((2, ITEM_BLOCKS, MOE_BLOCK, d), F32),
                        pltpu.SemaphoreType.DMA((2,)),
                        pltpu.SemaphoreType.DMA((ITEM_BLOCKS,))],
    )
    tok3 = row_tok.reshape(n_blocks, 1, MOE_BLOCK)
    return pl.pallas_call(
        _experts_body,
        grid_spec=grid_spec,
        out_shape=jax.ShapeDtypeStruct((n_rows, ROW_TILES, LANES), F32),
        compiler_params=_cparams(("arbitrary", "arbitrary")),
        name="experts",
    )(item_e, item_off, item_nsub, item_fill, n_real, w_gate, w_up, w_down, h_rows, *([tok3] * (2 * ITEM_BLOCKS)))


FINAL_TM = GATHER_ROWS // 2


def _final_body(idx_ref, idx_next_ref, ys_ref, w_ref, x1_ref, mod_ref, g_ref, o_ref, buf_ref, sem):
    slot = _gather_step(idx_ref, idx_next_ref, ys_ref, buf_ref, sem)
    tm = FINAL_TM
    w = w_ref[...]
    w0, w1 = w[:, 0:1], w[:, 1:2]
    ffn = jnp.concatenate([w0 * buf_ref[slot, c, 0:tm, :] + w1 * buf_ref[slot, c, tm:2 * tm, :]
                           for c in range(ROW_TILES)], axis=1)
    o_ref[...] = x1_ref[...] + mod_ref[5:6, :] * (_rms(ffn) * g_ref[...])


def _final(y_sorted, pos, wts, x1, mod8, g_post):
    t, d = x1.shape
    tm = FINAL_TM
    nt = t // tm
    idx3 = pos.reshape(nt, tm, 2).transpose(0, 2, 1).reshape(nt, 1, 2 * tm)
    return pl.pallas_call(
        _final_body,
        grid=(nt,),
        in_specs=_gather_specs(nt) + [pl.BlockSpec((tm, 128), lambda m: (m, 0)),
                                      pl.BlockSpec((tm, d), lambda m: (m, 0)),
                                      pl.BlockSpec((8, d), lambda m: (0, 0)),
                                      pl.BlockSpec((1, d), lambda m: (0, 0))],
        out_specs=pl.BlockSpec((tm, d), lambda m: (m, 0)),
        out_shape=jax.ShapeDtypeStruct((t, d), F32),
        scratch_shapes=GATHER_SCRATCH,
        compiler_params=_cparams(("arbitrary",)),
        name="final",
    )(idx3, idx3, y_sorted, wts, x1, mod8, g_post)


PLAN_TT = 512


def _plan_body(ids_ref, tril_ref, upper_ref, dest_ref, cnt_ref, run_ref):
    p = pl.program_id(0)
    i = pl.program_id(1)

    @pl.when((p == 0) & (i == 0))
    def _():
        run_ref[...] = jnp.zeros_like(run_ref)

    @pl.when((p == 1) & (i == 0))
    def _():
        counts = run_ref[...]
        cnt_ref[...] = counts.astype(I32)
        nblk = jnp.floor((counts + (MOE_BLOCK - 1.0)) * (1.0 / MOE_BLOCK))
        blk_start = _dot(nblk.astype(BF16), upper_ref[...])
        run_ref[...] = blk_start * MOE_BLOCK

    ids = ids_ref[...]
    lane = lax.broadcasted_iota(I32, ids.shape, 1)
    oh0 = lane == ids[:, 0:1]
    oh1 = lane == ids[:, 1:2]
    both = jnp.where(oh0 | oh1, 1.0, 0.0).astype(BF16)

    @pl.when(p == 1)
    def _():
        nxt = _dot(tril_ref[...], both) + run_ref[0:1, :]
        d0 = jnp.sum(jnp.where(oh0, nxt, 0.0), axis=-1, keepdims=True)
        d1 = jnp.sum(jnp.where(oh1, nxt, 0.0), axis=-1, keepdims=True)
        dest_ref[...] = jnp.where(lane == 0, d0, jnp.where(lane == 1, d1, 0.0)).astype(I32)

    run_ref[...] = run_ref[...] + _dot(jnp.ones((8, PLAN_TT), BF16), both)


def _plan(ids):
    t = ids.shape[0]
    r = np.arange(PLAN_TT)
    e = np.arange(128)
    strict_tril = jnp.asarray(r[:, None] > r[None, :], BF16)
    strict_upper = jnp.asarray(e[:, None] < e[None, :], BF16)
    return pl.pallas_call(
        _plan_body,
        grid=(2, t // PLAN_TT),
        in_specs=[pl.BlockSpec((PLAN_TT, 128), lambda p, i: (i, 0)),
                  pl.BlockSpec((PLAN_TT, PLAN_TT), lambda p, i: (0, 0)),
                  pl.BlockSpec((128, 128), lambda p, i: (0, 0))],
        out_specs=[pl.BlockSpec((PLAN_TT, 128), lambda p, i: (i * p, 0)),
                   pl.BlockSpec((8, 128), lambda p, i: (0, 0))],
        out_shape=[jax.ShapeDtypeStruct((t, 128), I32), jax.ShapeDtypeStruct((8, 128), I32)],
        scratch_shapes=[pltpu.VMEM((8, 128), F32)],
        compiler_params=_cparams(("arbitrary", "arbitrary")),
        name="plan",
    )(ids, strict_tril, strict_upper)


def _routing_tables(ids128, n_tok):
    n_assign = n_tok * 2
    n_blocks = n_assign // MOE_BLOCK + N_EXPERTS
    dest128, cnt = _plan(ids128)
    dest = dest128[:, :2].reshape(n_assign)
    counts = cnt[0, :N_EXPERTS]
    nb = (counts + MOE_BLOCK - 1) // MOE_BLOCK
    blk_start = jnp.cumsum(nb) - nb
    n_rows = n_blocks * MOE_BLOCK
    row_tok = (jnp.arange(n_rows, dtype=I32) % n_tok).at[dest].set(jnp.arange(n_assign, dtype=I32) // 2)
    pos = dest.reshape(n_tok, 2)

    n_it = (nb + ITEM_BLOCKS - 1) // ITEM_BLOCKS
    it_end = jnp.cumsum(n_it)
    it_start = it_end - n_it
    n_real = it_end[-1]
    i = jnp.arange(N_ITEMS, dtype=I32)
    e_i = jnp.minimum(jnp.searchsorted(it_end, i, side='right').astype(I32), N_EXPERTS - 1)
    k_i = i - it_start[e_i]
    valid = i < n_real
    last_e = e_i[jnp.maximum(n_real - 1, 0)]
    item_e = jnp.where(valid, e_i, last_e).astype(I32)
    fill_off = jnp.sum(nb) + ITEM_BLOCKS * (i - n_real)
    item_fill = jnp.where(valid, 0, jnp.clip(n_blocks - fill_off, 0, ITEM_BLOCKS)).astype(I32)
    item_off = jnp.where(valid, blk_start[e_i] + ITEM_BLOCKS * k_i, jnp.minimum(fill_off, n_blocks - 1)).astype(I32)
    item_nsub = jnp.where(valid, jnp.clip(nb[e_i] - ITEM_BLOCKS * k_i, 0, ITEM_BLOCKS), 0).astype(I32)
    return row_tok, pos, item_e, item_off, item_nsub, item_fill, n_real.reshape(1).astype(I32)


def _pad_lanes(v, start, total=SMALL_COLS):
    return jnp.zeros((1, total), F32).at[0, start:start + v.shape[0]].set(v)


def _layer(x2, c, w_ada, b_ada, norm_pre_mix, norm_post_mix, norm_pre_ffn, norm_post_ffn,
           w_in, gla_w_gate_up, gla_b_gate, gla_norm, ssm_conv_w, ssm_conv_b, ssm_dt_bias,
           ssm_a_log, ssm_d, ssm_norm, w_branch_gla, w_branch_ssm, w_out,
           router_group, router_expert, moe_w_gate, moe_w_up, moe_w_down):
    t, d = x2.shape
    row = lambda v: v.reshape(1, -1)

    mod = _ada(c, w_ada, b_ada)
    mod8 = jnp.concatenate([mod.reshape(6, d), jnp.zeros((2, d), F32)], axis=0)

    w_big, w_small = _repack(w_in.T)
    p, small = _inproj(x2, mod8, row(norm_pre_mix), w_big, w_small)

    o_gla = _gla(p, small, gla_w_gate_up, row(gla_b_gate), row(gla_norm))
    y_ssm = _ssd(p, small, ssm_conv_w, row(ssm_conv_b),
                 _pad_lanes(ssm_dt_bias, SMALL_DT0), _pad_lanes(ssm_a_log, SMALL_DT0),
                 _pad_lanes(ssm_d, SMALL_DT0), row(ssm_norm))
    merged = _merge(o_gla, y_ssm, w_branch_gla, w_branch_ssm, p)

    w_router = jnp.concatenate([router_group, router_expert,
                                jnp.zeros((d, 128 - MOE_GROUPS - N_EXPERTS), F32)], axis=1)
    x1, h2f, logits = _outproj(merged, w_out.astype(BF16), x2, mod8,
                               row(norm_post_mix), row(norm_pre_ffn), w_router)
    ids, wts = _route(logits)

    row_tok, pos, item_e, item_off, item_nsub, item_fill, n_real = _routing_tables(ids, t)
    y_sorted = _experts(h2f, row_tok, moe_w_gate, moe_w_up, moe_w_down,
                        item_e, item_off, item_nsub, item_fill, n_real)
    return _final(y_sorted, pos, wts, x1, mod8, row(norm_post_ffn))


def kernel(x, c, w_ada, b_ada, norm_pre_mix, norm_post_mix, norm_pre_ffn, norm_post_ffn, w_in, gla_w_gate_up, gla_b_gate, gla_norm, ssm_conv_w, ssm_conv_b, ssm_dt_bias, ssm_a_log, ssm_d, ssm_norm, w_branch_gla, w_branch_ssm, w_out, router_group, router_expert, moe_w_gate, moe_w_up, moe_w_down):
    bsz, seq, d = x.shape
    assert bsz == 1 and d == D_MODEL
    x2 = x.reshape(seq, d)
    params = (w_ada, b_ada, norm_pre_mix, norm_post_mix, norm_pre_ffn, norm_post_ffn, w_in, gla_w_gate_up,
              gla_b_gate, gla_norm, ssm_conv_w, ssm_conv_b, ssm_dt_bias, ssm_a_log, ssm_d, ssm_norm,
              w_branch_gla, w_branch_ssm, w_out, router_group, router_expert, moe_w_gate, moe_w_up, moe_w_down)
    for layer in range(w_ada.shape[0]):
        x2 = _layer(x2, c, *(prm[layer] for prm in params))
    return x2.reshape(bsz, seq, d)
```

```python
import functools

import jax
import jax.numpy as jnp
import numpy as np
from jax import lax
from jax.experimental import pallas as pl
from jax.experimental.pallas import tpu as pltpu

F32 = jnp.float32
BF16 = jnp.bfloat16
I32 = jnp.int32

D_MODEL = 2048
EPS = 1e-6
LANES = 128
ROW_TILES = D_MODEL // LANES

GLA_HEADS = 4
GLA_HEAD_K = 256
GLA_HEAD_V = 512
GLA_RANK = 16
GLA_NORMALIZER = 16.0
CHUNK = 64

SSM_GROUPS = 8
SSM_HEADS = 64
SSM_HEAD_DIM = 64
SSM_STATE = 128
SSM_CONV = 4
SSM_GROUP_W = 512
SSM_INNER = 4096

N_EXPERTS = 64
EXPERTS_PER_GROUP = 8
MOE_GROUPS = 8
MOE_FF = 1024
MOE_BLOCK = 128
ITEM_BLOCKS = 4

COL_Q, COL_K, COL_V, COL_OG, COL_Z, COL_XS, COL_B, COL_C, COL_GG, COL_GS = (
    0, 1024, 2048, 4096, 6144, 10240, 14336, 15360, 16384, 18432)
P_COLS = 20480
SMALL_COLS = 128
SMALL_DT0 = GLA_RANK

VMEM_LIMIT = 56 * 1024 * 1024


def _cparams(sem, vmem=VMEM_LIMIT):
    return pltpu.CompilerParams(dimension_semantics=sem, vmem_limit_bytes=vmem)


def _dot(a, b):
    return jnp.dot(a, b, preferred_element_type=F32)


def _dot_nt(a, b):
    return lax.dot_general(a, b, (((1,), (1,)), ((), ())), preferred_element_type=F32)


def _dot_tn(a, b):
    return lax.dot_general(a, b, (((0,), (0,)), ((), ())), preferred_element_type=F32)


def _split3(a):
    hi = a.astype(BF16)
    r1 = a - hi.astype(F32)
    mid = r1.astype(BF16)
    lo = (r1 - mid.astype(F32)).astype(BF16)
    return hi, mid, lo


def _dot_sel_r(a, sel):
    hi, mid, lo = _split3(a)
    return _dot(hi, sel) + _dot(mid, sel) + _dot(lo, sel)


def _dot_sel_l(sel, a):
    hi, mid, lo = _split3(a)
    return _dot(sel, hi) + _dot(sel, mid) + _dot(sel, lo)


def _sigmoid(x):
    return 1.0 / (1.0 + jnp.exp(-x))


def _silu(x):
    return x * _sigmoid(x)


def _softplus(x):
    return jnp.maximum(x, 0.0) + jnp.log1p(jnp.exp(-jnp.abs(x)))


def _log_sigmoid(x):
    return jnp.minimum(x, 0.0) - jnp.log1p(jnp.exp(-jnp.abs(x)))


def _rms(x):
    return x * lax.rsqrt(jnp.mean(x * x, axis=-1, keepdims=True) + EPS)


def _ada_body(c_ref, w_ref, b_ref, o_ref):
    c = c_ref[...]
    s = jnp.broadcast_to(_silu(c), (8, c.shape[1])).astype(BF16)
    o_ref[...] = _dot(s, w_ref[...].astype(BF16))[0:1] + b_ref[...]


def _ada(c, w_ada, b_ada):
    d, n = w_ada.shape
    tn = 1024
    return pl.pallas_call(
        _ada_body,
        grid=(n // tn,),
        in_specs=[pl.BlockSpec((1, d), lambda j: (0, 0)),
                  pl.BlockSpec((d, tn), lambda j: (0, j)),
                  pl.BlockSpec((1, tn), lambda j: (0, j))],
        out_specs=pl.BlockSpec((1, tn), lambda j: (0, j)),
        out_shape=jax.ShapeDtypeStruct((1, n), F32),
        compiler_params=_cparams(("parallel",)),
        name="ada",
    )(c, w_ada, b_ada.reshape(1, n))


W_GLR, W_MID, W_DT, W_GATES, W_END = 4096, 4112, 16400, 16464, 20560
REPACK_ROWS = 1024
REPACK_TAIL = 128


def _repack_body(a_ref, b_ref, big_ref, small_ref):
    j = pl.program_id(0)
    n = REPACK_ROWS

    def emit(shift):
        if shift == 0:
            big_ref[...] = a_ref[...].astype(BF16)
        else:
            big_ref[0:n - shift, :] = a_ref[shift:n, :].astype(BF16)
            big_ref[n - shift:n, :] = b_ref[0:shift, :].astype(BF16)

    first_mid, first_gates = COL_OG // n, COL_GG // n

    @pl.when(j < first_mid)
    def _():
        emit(0)

    @pl.when((j >= first_mid) & (j < first_gates))
    def _():
        emit(W_MID - COL_OG)

    @pl.when(j >= first_gates)
    def _():
        emit(W_GATES - COL_GG)

    @pl.when(j == 0)
    def _():
        small_ref[...] = jnp.zeros_like(small_ref)

    @pl.when(j == W_GLR // n)
    def _():
        small_ref[0:GLA_RANK, :] = a_ref[0:GLA_RANK, :].astype(BF16)

    @pl.when(j == W_DT // n)
    def _():
        small_ref[SMALL_DT0:SMALL_DT0 + SSM_HEADS, :] = a_ref[SMALL_DT0:SMALL_DT0 + SSM_HEADS, :].astype(BF16)


def _repack(w_t):
    n_in, d = w_t.shape
    n = REPACK_ROWS
    assert n_in == W_END and W_GLR % n == 0 and W_GLR + GLA_RANK == W_MID and W_DT % n == SMALL_DT0
    assert COL_OG % n == 0 and COL_GG % n == 0 and W_GATES - COL_GG <= REPACK_TAIL
    return pl.pallas_call(
        _repack_body,
        grid=(P_COLS // n,),
        in_specs=[pl.BlockSpec((n, d), lambda j: (j, 0)),
                  pl.BlockSpec((REPACK_TAIL, d), lambda j: ((j + 1) * (n // REPACK_TAIL), 0))],
        out_specs=[pl.BlockSpec((n, d), lambda j: (j, 0)),
                   pl.BlockSpec((SMALL_COLS, d), lambda j: (0, 0))],
        out_shape=[jax.ShapeDtypeStruct((P_COLS, d), BF16), jax.ShapeDtypeStruct((SMALL_COLS, d), BF16)],
        compiler_params=_cparams(("arbitrary",)),
        name="repack",
    )(w_t, w_t)


def _inproj_body(x_ref, mod_ref, g_ref, w_ref, ws_ref, p_ref, s_ref, h_ref):
    @pl.when(pl.program_id(1) == 0)
    def _():
        h = _rms(x_ref[...]) * g_ref[...] * (1.0 + mod_ref[1:2, :]) + mod_ref[0:1, :]
        hb = h.astype(BF16)
        h_ref[...] = hb
        s_ref[...] = _dot_nt(hb, ws_ref[...])

    p_ref[...] = _dot_nt(h_ref[...], w_ref[...]).astype(BF16)


def _inproj(x2, mod8, gain, w_big, w_small):
    t, d = x2.shape
    n = w_big.shape[0]
    tm, tn = 1024, 1024
    return pl.pallas_call(
        _inproj_body,
        grid=(t // tm, n // tn),
        in_specs=[pl.BlockSpec((tm, d), lambda m, j: (m, 0)),
                  pl.BlockSpec((8, d), lambda m, j: (0, 0)),
                  pl.BlockSpec((1, d), lambda m, j: (0, 0)),
                  pl.BlockSpec((tn, d), lambda m, j: (j, 0)),
                  pl.BlockSpec((SMALL_COLS, d), lambda m, j: (0, 0))],
        out_specs=[pl.BlockSpec((tm, tn), lambda m, j: (m, j)),
                   pl.BlockSpec((tm, SMALL_COLS), lambda m, j: (m, 0))],
        out_shape=[jax.ShapeDtypeStruct((t, n), BF16),
                   jax.ShapeDtypeStruct((t, SMALL_COLS), F32)],
        scratch_shapes=[pltpu.VMEM((tm, d), BF16)],
        compiler_params=_cparams(("parallel", "arbitrary")),
        name="inproj",
    )(x2, mod8, gain, w_big, w_small)


GLA_TB = 512


def _gla_body(q_ref, k_ref, v_ref, og_ref, sm_ref, wup_ref, bg_ref, gn_ref, o_ref, st_ref):
    @pl.when(pl.program_id(1) == 0)
    def _():
        st_ref[...] = jnp.zeros_like(st_ref)

    tb, C = GLA_TB, CHUNK
    nch = tb // C
    r = lax.broadcasted_iota(I32, (tb, tb), 0)
    c = lax.broadcasted_iota(I32, (tb, tb), 1)
    causal = (r // C == c // C) & (r >= c)
    r2 = lax.broadcasted_iota(I32, (2 * C, 2 * C), 0)
    c2 = lax.broadcasted_iota(I32, (2 * C, 2 * C), 1)
    tril2 = ((r2 // C == c2 // C) & (r2 >= c2)).astype(BF16)

    glr = sm_ref[:, 0:GLA_RANK].astype(BF16)
    pre = _dot(glr, wup_ref[...].astype(BF16)) + bg_ref[...]
    log_a = _log_sigmoid(pre) / GLA_NORMALIZER
    b = jnp.concatenate([_dot_sel_l(tril2, log_a[i * 2 * C:(i + 1) * 2 * C]) for i in range(nch // 2)],
                        axis=0)
    b_last = [b[(i + 1) * C - 1:(i + 1) * C, :] for i in range(nch)]
    b_end = jnp.concatenate([jnp.broadcast_to(bl, (C, bl.shape[1])) for bl in b_last], axis=0)

    q = q_ref[...].astype(F32) * (GLA_HEAD_K ** -0.5)
    k = k_ref[...].astype(F32)
    v = v_ref[...]
    q_dec = (q * jnp.exp(b)).astype(BF16)
    k_inv = (k * jnp.exp(-b)).astype(BF16)
    k_end = (k * jnp.exp(b_end - b)).astype(BF16)
    scores = jnp.where(causal, _dot_nt(q_dec, k_inv), 0.0)
    o_intra = _dot(scores.astype(BF16), v)

    st = st_ref[...]
    o_inter = []
    for i in range(nch):
        rows = slice(i * C, (i + 1) * C)
        o_inter.append(_dot_nt(q_dec[rows], st.astype(BF16)))
        st = st * jnp.exp(b_last[i]) + _dot_tn(v[rows], k_end[rows])
    st_ref[...] = st

    o = o_intra + jnp.concatenate(o_inter, axis=0)
    og = og_ref[...].astype(F32)
    o_ref[...] = (_rms(o) * gn_ref[...] * _silu(og)).astype(BF16)


def _gla(p, small, wup, bg, gn):
    t = p.shape[0]
    tb = GLA_TB
    kb, vb = GLA_HEAD_K, GLA_HEAD_V
    return pl.pallas_call(
        _gla_body,
        grid=(GLA_HEADS, t // tb),
        in_specs=[pl.BlockSpec((tb, kb), lambda h, i: (i, COL_Q // kb + h)),
                  pl.BlockSpec((tb, kb), lambda h, i: (i, COL_K // kb + h)),
                  pl.BlockSpec((tb, vb), lambda h, i: (i, COL_V // vb + h)),
                  pl.BlockSpec((tb, vb), lambda h, i: (i, COL_OG // vb + h)),
                  pl.BlockSpec((tb, SMALL_COLS), lambda h, i: (i, 0)),
                  pl.BlockSpec((GLA_RANK, kb), lambda h, i: (0, h)),
                  pl.BlockSpec((1, kb), lambda h, i: (0, h)),
                  pl.BlockSpec((1, vb), lambda h, i: (0, h))],
        out_specs=pl.BlockSpec((tb, vb), lambda h, i: (i, h)),
        out_shape=jax.ShapeDtypeStruct((t, GLA_HEADS * vb), BF16),
        scratch_shapes=[pltpu.VMEM((vb, kb), F32)],
        compiler_params=_cparams(("parallel", "arbitrary")),
        name="gla",
    )(p, p, p, p, small, wup, bg, gn)


SSD_TB = 512
HALO = 8


def _ssd_body(xs_ref, b_ref, c_ref, z_ref, sm_ref, cwx_ref, cwb_ref, cwc_ref, cbx_ref, cbb_ref, cbc_ref,
              dtb_ref, alog_ref, dsk_ref, ng_ref, o_ref,
              ex_ref, eb_ref, ec_ref, at_ref, ht_ref):
    g = pl.program_id(0)
    t = pl.program_id(1)
    tb = SSD_TB
    gw = SSM_GROUP_W

    @pl.when(t == 0)
    def _():
        ht_ref[...] = jnp.zeros_like(ht_ref)
        ex_ref[0:HALO, :] = jnp.zeros((HALO, gw), F32)
        eb_ref[0:HALO, :] = jnp.zeros((HALO, SSM_STATE), F32)
        ec_ref[0:HALO, :] = jnp.zeros((HALO, SSM_STATE), F32)

    @pl.when(t > 0)
    def _():
        ex_ref[0:HALO, :] = ex_ref[tb:tb + HALO, :]
        eb_ref[0:HALO, :] = eb_ref[tb:tb + HALO, :]
        ec_ref[0:HALO, :] = ec_ref[tb:tb + HALO, :]

    def conv_silu(u_ref, e_ref, w_ref, bias_ref):
        e_ref[HALO:HALO + tb, :] = u_ref[...].astype(F32)
        acc = bias_ref[...] + w_ref[0:1, :] * e_ref[HALO - 3:HALO - 3 + tb, :]
        for kk in range(1, SSM_CONV):
            acc = acc + w_ref[kk:kk + 1, :] * e_ref[HALO - 3 + kk:HALO - 3 + kk + tb, :]
        return _silu(acc)

    xa = conv_silu(xs_ref, ex_ref, cwx_ref, cbx_ref)
    ba = conv_silu(b_ref, eb_ref, cwb_ref, cbb_ref).astype(BF16)
    ca = conv_silu(c_ref, ec_ref, cwc_ref, cbc_ref).astype(BF16)

    L = CHUNK
    nch = tb // L
    rep = gw // L
    hpg = gw // SSM_HEAD_DIM
    head0 = pl.multiple_of(SMALL_DT0 + g * hpg, hpg)

    e_row = lax.broadcasted_iota(I32, (SMALL_COLS, gw), 0)
    e_col = lax.broadcasted_iota(I32, (SMALL_COLS, gw), 1)
    expand = (e_row == head0 + e_col // SSM_HEAD_DIM).astype(BF16)

    dt_small = _softplus(sm_ref[...] + dtb_ref[...])
    adt_small = dt_small * (-jnp.exp(alog_ref[...]))
    rb = lax.broadcasted_iota(I32, (2 * L, 2 * L), 0)
    cb = lax.broadcasted_iota(I32, (2 * L, 2 * L), 1)
    tril2 = ((rb // L == cb // L) & (rb >= cb)).astype(BF16)
    acum_small = jnp.concatenate([_dot_sel_l(tril2, adt_small[k * 2 * L:(k + 1) * 2 * L])
                                  for k in range(nch // 2)], axis=0)
    at_ref[...] = acum_small.T
    heads = at_ref[pl.ds(head0, hpg), :]
    dt_exp = _dot_sel_r(dt_small, expand)
    acum = _dot_sel_r(acum_small, expand)
    d_exp = _dot_sel_r(jnp.broadcast_to(dsk_ref[...], (8, SMALL_COLS)), expand)[0:1]
    xdt = xa * dt_exp
    xdt_b = xdt.astype(BF16)
    e_acum = jnp.exp(acum)

    li = lax.broadcasted_iota(I32, (L, gw), 0)
    lj = lax.broadcasted_iota(I32, (L, gw), 1) % L
    causal_t = li >= lj
    hi = lax.broadcasted_iota(I32, (hpg, gw), 0)
    hj = lax.broadcasted_iota(I32, (hpg, gw), 1) // L
    headmask = hi == hj
    ones_h = jnp.ones((L, hpg), BF16)
    bi = lax.broadcasted_iota(I32, (gw, gw), 0) // L
    bj = lax.broadcasted_iota(I32, (gw, gw), 1) // SSM_HEAD_DIM
    blockmask = bi == bj
    masked_out = -1e30

    ht = ht_ref[...]
    for c in range(nch):
        rows = slice(c * L, (c + 1) * L)
        acum_c = acum[rows]
        a_rows = jnp.concatenate([heads[:, c * L:(c + 1) * L]] * rep, axis=1)
        rterm = _dot_sel_l(ones_h, jnp.where(headmask, a_rows, 0.0))
        decay = jnp.exp(jnp.where(causal_t, acum_c - rterm, masked_out))
        cc = ca[rows]
        bc = ba[rows]
        cb_t = _dot_nt(cc, jnp.concatenate([bc] * rep, axis=0))
        m = (cb_t * decay).astype(BF16)
        bd = jnp.where(blockmask, jnp.concatenate([xdt_b[rows]] * rep, axis=0), jnp.zeros((), BF16))
        y_diag = _dot(m, bd)
        y_off = _dot(cc, ht.astype(BF16)) * e_acum[rows]
        a_last = acum_c[L - 1:L, :]
        xd = (xdt[rows] * jnp.exp(a_last - acum_c)).astype(BF16)
        ht = ht * jnp.exp(a_last) + _dot_tn(bc, xd)
        y = y_diag + y_off + d_exp * xa[rows]
        y = y * _silu(z_ref[rows, :].astype(F32))
        o_ref[rows, :] = (_rms(y) * ng_ref[...]).astype(BF16)
    ht_ref[...] = ht


def _ssd(p, small, conv_w, conv_b, dt_bias_s, a_log_s, d_skip_s, norm_g):
    t = p.shape[0]
    tb = SSD_TB
    gw, ns = SSM_GROUP_W, SSM_STATE
    xs0, b0, c0 = 0, SSM_INNER // ns, (SSM_INNER + SSM_GROUPS * ns) // ns
    row = lambda w, off: pl.BlockSpec((1, w), lambda g, i: (0, off + g))
    return pl.pallas_call(
        _ssd_body,
        grid=(SSM_GROUPS, t // tb),
        in_specs=[pl.BlockSpec((tb, gw), lambda g, i: (i, COL_XS // gw + g)),
                  pl.BlockSpec((tb, ns), lambda g, i: (i, COL_B // ns + g)),
                  pl.BlockSpec((tb, ns), lambda g, i: (i, COL_C // ns + g)),
                  pl.BlockSpec((tb, gw), lambda g, i: (i, COL_Z // gw + g)),
                  pl.BlockSpec((tb, SMALL_COLS), lambda g, i: (i, 0)),
                  pl.BlockSpec((SSM_CONV, gw), lambda g, i: (0, xs0 + g)),
                  pl.BlockSpec((SSM_CONV, ns), lambda g, i: (0, b0 + g)),
                  pl.BlockSpec((SSM_CONV, ns), lambda g, i: (0, c0 + g)),
                  row(gw, xs0), row(ns, b0), row(ns, c0),
                  pl.BlockSpec((1, SMALL_COLS), lambda g, i: (0, 0)),
                  pl.BlockSpec((1, SMALL_COLS), lambda g, i: (0, 0)),
                  pl.BlockSpec((1, SMALL_COLS), lambda g, i: (0, 0)),
                  pl.BlockSpec((1, gw), lambda g, i: (0, g))],
        out_specs=pl.BlockSpec((tb, gw), lambda g, i: (i, g)),
        out_shape=jax.ShapeDtypeStruct((t, SSM_INNER), BF16),
        scratch_shapes=[pltpu.VMEM((tb + HALO, gw), F32),
                        pltpu.VMEM((tb + HALO, ns), F32),
                        pltpu.VMEM((tb + HALO, ns), F32),
                        pltpu.VMEM((SMALL_COLS, tb), F32),
                        pltpu.VMEM((ns, gw), F32)],
        compiler_params=_cparams(("parallel", "arbitrary")),
        name="ssd",
    )(p, p, p, p, small, conv_w, conv_w, conv_w, conv_b, conv_b, conv_b,
      dt_bias_s, a_log_s, d_skip_s, norm_g)


def _merge_body(a1_ref, a2_ref, w1_ref, w2_ref, gg_ref, gs_ref, o_ref, w1b_ref, w2b_ref):
    @pl.when(pl.program_id(1) == 0)
    def _():
        w1b_ref[...] = w1_ref[...].astype(BF16)
        w2b_ref[...] = w2_ref[...].astype(BF16)

    y1 = _dot(a1_ref[...], w1b_ref[...])
    y2 = _dot(a2_ref[...], w2b_ref[...])
    o_ref[...] = (_sigmoid(gg_ref[...].astype(F32)) * y1 + _sigmoid(gs_ref[...].astype(F32)) * y2).astype(BF16)


def _merge(o_gla, y_ssm, w1, w2, p):
    t, k1 = o_gla.shape
    k2 = y_ssm.shape[1]
    n = w1.shape[1]
    tm, tn = 512, 512
    return pl.pallas_call(
        _merge_body,
        grid=(n // tn, t // tm),
        in_specs=[pl.BlockSpec((tm, k1), lambda j, m: (m, 0)),
                  pl.BlockSpec((tm, k2), lambda j, m: (m, 0)),
                  pl.BlockSpec((k1, tn), lambda j, m: (0, j)),
                  pl.BlockSpec((k2, tn), lambda j, m: (0, j)),
                  pl.BlockSpec((tm, tn), lambda j, m: (m, COL_GG // tn + j)),
                  pl.BlockSpec((tm, tn), lambda j, m: (m, COL_GS // tn + j))],
        out_specs=pl.BlockSpec((tm, tn), lambda j, m: (m, j)),
        out_shape=jax.ShapeDtypeStruct((t, n), BF16),
        scratch_shapes=[pltpu.VMEM((k1, tn), BF16), pltpu.VMEM((k2, tn), BF16)],
        compiler_params=_cparams(("parallel", "arbitrary")),
        name="merge",
    )(o_gla, y_ssm, w1, w2, p, p)


def _outproj_body(m_ref, w_ref, x_ref, mod_ref, gpost_ref, gpre_ref, wr_ref, x1_ref, hf_ref, lg_ref):
    mix = _dot(m_ref[...], w_ref[...])
    x1 = x_ref[...] + mod_ref[2:3, :] * (_rms(mix) * gpost_ref[...])
    x1_ref[...] = x1
    h = _rms(x1) * gpre_ref[...] * (1.0 + mod_ref[4:5, :]) + mod_ref[3:4, :]
    for s in range(ROW_TILES):
        hf_ref[:, s, :] = h[:, s * LANES:(s + 1) * LANES]
    h_hi = h.astype(BF16)
    h_lo = (h - h_hi.astype(F32)).astype(BF16)
    wr = wr_ref[...]
    w_hi = wr.astype(BF16)
    w_lo = (wr - w_hi.astype(F32)).astype(BF16)
    lg_ref[...] = _dot(h_hi, w_hi) + _dot(h_hi, w_lo) + _dot(h_lo, w_hi)


def _outproj(merged, w_out, x2, mod8, g_post, g_pre, w_router):
    t, d = x2.shape
    tm = 256
    full = lambda r, c: pl.BlockSpec((r, c), lambda m: (0, 0))
    tile = lambda c: pl.BlockSpec((tm, c), lambda m: (m, 0))
    return pl.pallas_call(
        _outproj_body,
        grid=(t // tm,),
        in_specs=[tile(d), full(d, d), tile(d), full(8, d), full(1, d), full(1, d), full(d, 128)],
        out_specs=[tile(d), pl.BlockSpec((tm, ROW_TILES, LANES), lambda m: (m, 0, 0)), tile(128)],
        out_shape=[jax.ShapeDtypeStruct((t, d), F32),
                   jax.ShapeDtypeStruct((t, ROW_TILES, LANES), F32),
                   jax.ShapeDtypeStruct((t, 128), F32)],
        compiler_params=_cparams(("parallel",)),
        name="outproj",
    )(merged, w_out, x2, mod8, g_post, g_pre, w_router)


def _route_body(lg_ref, id_ref, w_ref):
    lg = lg_ref[...]
    lane = lax.broadcasted_iota(I32, lg.shape, 1)
    lane_f = lane.astype(F32)
    neg = jnp.float32(-jnp.inf)

    def first_argmax(vals, mx):
        return jnp.min(jnp.where(vals == mx, lane_f, 1e9), axis=-1, keepdims=True).astype(I32)

    gmask = lane < MOE_GROUPS
    gl = jnp.where(gmask, lg, neg)
    gmax = jnp.max(gl, axis=-1, keepdims=True)
    gsum = jnp.sum(jnp.where(gmask, jnp.exp(gl - gmax), 0.0), axis=-1, keepdims=True)
    g_w = 1.0 / gsum
    g_idx = first_argmax(gl, gmax)
    lo = MOE_GROUPS + g_idx * EXPERTS_PER_GROUP
    emask = (lane >= lo) & (lane < lo + EXPERTS_PER_GROUP)
    el = jnp.where(emask, lg, neg)
    m1 = jnp.max(el, axis=-1, keepdims=True)
    i1 = first_argmax(el, m1)
    el2 = jnp.where(lane == i1, neg, el)
    m2 = jnp.max(el2, axis=-1, keepdims=True)
    i2 = first_argmax(el2, m2)
    r = jnp.exp(m2 - m1)
    w1 = g_w / (1.0 + r)
    w2 = g_w * r / (1.0 + r)
    id_ref[...] = jnp.where(lane == 0, i1 - MOE_GROUPS, jnp.where(lane == 1, i2 - MOE_GROUPS, 0))
    w_ref[...] = jnp.where(lane == 0, w1, jnp.where(lane == 1, w2, 0.0))


def _route(logits):
    t = logits.shape[0]
    tm = 1024
    spec = pl.BlockSpec((tm, 128), lambda m: (m, 0))
    return pl.pallas_call(
        _route_body,
        grid=(t // tm,),
        in_specs=[spec],
        out_specs=[spec, spec],
        out_shape=[jax.ShapeDtypeStruct((t, 128), I32), jax.ShapeDtypeStruct((t, 128), F32)],
        compiler_params=_cparams(("parallel",)),
        name="route",
    )(logits)


GATHER_ROWS = 512


def _issue_rows(idx_ref, src_ref, buf_ref, slot, sem, skip_pads):
    def body(r, carry):
        tok = idx_ref[0, 0, r]
        copy = pltpu.make_async_copy(src_ref.at[jnp.maximum(tok, 0)], buf_ref.at[slot, :, r, :], sem.at[slot])
        if skip_pads:
            @pl.when(tok >= 0)
            def _():
                copy.start()
        else:
            copy.start()
        return carry

    lax.fori_loop(0, GATHER_ROWS, body, 0, unroll=8)


def _wait_rows(src_ref, buf_ref, slot, sem, count):
    del src_ref
    if count is None:
        pltpu.make_async_copy(buf_ref.at[slot], buf_ref.at[slot], sem.at[slot]).wait()
    else:
        @pl.when(count > 0)
        def _():
            part = buf_ref.at[slot, :, pl.ds(0, count), :]
            pltpu.make_async_copy(part, part, sem.at[slot]).wait()


def _gather_step(idx_ref, idx_next_ref, src_ref, buf_ref, sem, cnt_ref=None):
    i = pl.program_id(0)
    slot = i % 2
    skip_pads = cnt_ref is not None

    @pl.when(i == 0)
    def _():
        if skip_pads:
            buf_ref[...] = jnp.zeros_like(buf_ref)
        _issue_rows(idx_ref, src_ref, buf_ref, 0, sem, skip_pads)

    @pl.when(i + 1 < pl.num_programs(0))
    def _():
        _issue_rows(idx_next_ref, src_ref, buf_ref, 1 - slot, sem, skip_pads)

    _wait_rows(src_ref, buf_ref, slot, sem, cnt_ref[i] if skip_pads else None)
    return slot


def _gather_specs(nsteps):
    smem = lambda f: pl.BlockSpec((1, 1, GATHER_ROWS), f, memory_space=pltpu.SMEM)
    return [smem(lambda i: (i, 0, 0)),
            smem(lambda i: (jnp.minimum(i + 1, nsteps - 1), 0, 0)),
            pl.BlockSpec(memory_space=pl.ANY)]


GATHER_SCRATCH = [pltpu.VMEM((2, ROW_TILES, GATHER_ROWS, LANES), F32), pltpu.SemaphoreType.DMA((2,))]


MOE_FC = 512
MOE_J = MOE_FF // MOE_FC
N_ITEMS = (16384 // MOE_BLOCK + N_EXPERTS) // ITEM_BLOCKS + (N_EXPERTS * (ITEM_BLOCKS - 1)) // ITEM_BLOCKS


def _experts_body(ie_ref, io_ref, ins_ref, ifl_ref, nr_ref, wg_ref, wu_ref, wd_ref, h_ref,
                  f0_ref, f1_ref, f2_ref, f3_ref, n0_ref, n1_ref, n2_ref, n3_ref, ys_ref,
                  xst_ref, xb_ref, acc_ref, sem_in, sem_out):
    i = pl.program_id(0)
    j = pl.program_id(1)
    n_items = pl.num_programs(0)
    nsub = ins_ref[i]
    nfill = ifl_ref[i]
    slot = i % 2
    blk = MOE_BLOCK

    def row0(item):
        return pl.multiple_of(io_ref[item] * blk, blk)

    def start_rows(idx_refs, count, sl):
        def block(s):
            def body(r, carry):
                pltpu.make_async_copy(h_ref.at[idx_refs[s][0, 0, r]], xst_ref.at[sl, :, s * blk + r, :],
                                      sem_in.at[sl]).start()
                return carry

            lax.fori_loop(0, blk, body, 0, unroll=8)

        for_blocks(count, block)

    def wait_rows(count, sl):
        @pl.when(count > 0)
        def _():
            part = xst_ref.at[sl, :, pl.ds(0, count * blk), :]
            pltpu.make_async_copy(part, part, sem_in.at[sl]).wait()

    def start_y(item, s):
        for c in range(ROW_TILES):
            pltpu.make_async_copy(acc_ref.at[item % 2, s, :, pl.ds(c * LANES, LANES)],
                                  ys_ref.at[pl.ds(row0(item) + s * blk, blk), c, :], sem_out.at[s]).start()

    def wait_y(s):
        pltpu.make_async_copy(acc_ref.at[0, s], acc_ref.at[0, s], sem_out.at[s]).wait()

    def for_blocks(count, fn):
        for s in range(ITEM_BLOCKS):
            @pl.when(s < count)
            def _():
                fn(s)

    @pl.when(j == 0)
    def _():
        @pl.when(i == 0)
        def _():
            start_rows((f0_ref, f1_ref, f2_ref, f3_ref), nsub, 0)

        wait_rows(nsub, slot)

        def to_bf16(s):
            for c in range(ROW_TILES):
                xb_ref[s, :, c * LANES:(c + 1) * LANES] = xst_ref[slot, c, s * blk:(s + 1) * blk, :].astype(BF16)

        for_blocks(nsub, to_bf16)

    @pl.when((j == MOE_J - 1) & (i + 1 < n_items))
    def _():
        start_rows((n0_ref, n1_ref, n2_ref, n3_ref), ins_ref[jnp.minimum(i + 1, n_items - 1)], 1 - slot)

    for n in range(1, ITEM_BLOCKS + 1):
        @pl.when(nsub == n)
        def _():
            x = xb_ref[0:n].reshape(n * blk, D_MODEL)
            gate = _dot(x, wg_ref[...].astype(BF16))
            up = _dot(x, wu_ref[...].astype(BF16))
            hid = (_silu(gate) * up).astype(BF16)
            y = _dot(hid, wd_ref[...].astype(BF16)).reshape(n, blk, D_MODEL)

            @pl.when(j == 0)
            def _():
                acc_ref[slot, 0:n] = y

            @pl.when(j > 0)
            def _():
                acc_ref[slot, 0:n] = acc_ref[slot, 0:n] + y

    def wait_prev_y():
        @pl.when(i > 0)
        def _():
            for_blocks(ins_ref[jnp.maximum(i - 1, 0)], wait_y)

    @pl.when((nsub > 0) & (j == MOE_J - 1))
    def _():
        wait_prev_y()
        for_blocks(nsub, lambda s: start_y(i, s))

        @pl.when(i == n_items - 1)
        def _():
            for_blocks(nsub, wait_y)

    @pl.when((nsub == 0) & (j == 0))
    def _():
        wait_prev_y()

        @pl.when(nfill > 0)
        def _():
            acc_ref[slot] = jnp.zeros(acc_ref.shape[1:], F32)
            for_blocks(nfill, lambda s: start_y(i, s))
            for_blocks(nfill, wait_y)


def _experts(h_rows, row_tok, w_gate, w_up, w_down, item_e, item_off, item_nsub, item_fill, n_real):
    n_rows = row_tok.shape[0]
    n_blocks = n_rows // MOE_BLOCK
    d = D_MODEL

    def w_in_map(i, j, ie, io, ins, ifl, nr):
        return (ie[i], 0, jnp.where(i < nr[0], j, MOE_J - 1))

    def w_dn_map(i, j, ie, io, ins, ifl, nr):
        return (ie[i], jnp.where(i < nr[0], j, MOE_J - 1), 0)

    def tok_spec(s, next_item):
        def index_map(i, j, ie, io, ins, ifl, nr):
            item = jnp.minimum(i + 1, N_ITEMS - 1) if next_item else 0
            return (jnp.minimum(io[item] + s, n_blocks - 1), 0, 0)

        return pl.BlockSpec((1, 1, MOE_BLOCK), index_map, memory_space=pltpu.SMEM)

    grid_spec = pltpu.PrefetchScalarGridSpec(
        num_scalar_prefetch=5,
        grid=(N_ITEMS, MOE_J),
        in_specs=[pl.BlockSpec((None, d, MOE_FC), w_in_map),
                  pl.BlockSpec((None, d, MOE_FC), w_in_map),
                  pl.BlockSpec((None, MOE_FC, d), w_dn_map),
                  pl.BlockSpec(memory_space=pl.ANY)]
                 + [tok_spec(s, False) for s in range(ITEM_BLOCKS)]
                 + [tok_spec(s, True) for s in range(ITEM_BLOCKS)],
        out_specs=pl.BlockSpec(memory_space=pl.ANY),
        scratch_shapes=[pltpu.VMEM((2, ROW_TILES, ITEM_BLOCKS * MOE_BLOCK, LANES), F32),
                        pltpu.VMEM((ITEM_BLOCKS, MOE_BLOCK, d), BF16),
                        pltpu.VMEM((2, ITEM_BLOCKS, MOE_BLOCK, d), F32),
                        pltpu.SemaphoreType.DMA((2,)),
                        pltpu.SemaphoreType.DMA((ITEM_BLOCKS,))],
    )
    tok3 = row_tok.reshape(n_blocks, 1, MOE_BLOCK)
    return pl.pallas_call(
        _experts_body,
        grid_spec=grid_spec,
        out_shape=jax.ShapeDtypeStruct((n_rows, ROW_TILES, LANES), F32),
        compiler_params=_cparams(("arbitrary", "arbitrary")),
        name="experts",
    )(item_e, item_off, item_nsub, item_fill, n_real, w_gate, w_up, w_down, h_rows, *([tok3] * (2 * ITEM_BLOCKS)))


FINAL_TM = GATHER_ROWS // 2


def _final_body(idx_ref, idx_next_ref, ys_ref, w_ref, x1_ref, mod_ref, g_ref, o_ref, buf_ref, sem):
    slot = _gather_step(idx_ref, idx_next_ref, ys_ref, buf_ref, sem)
    tm = FINAL_TM
    w = w_ref[...]
    w0, w1 = w[:, 0:1], w[:, 1:2]
    ffn = jnp.concatenate([w0 * buf_ref[slot, c, 0:tm, :] + w1 * buf_ref[slot, c, tm:2 * tm, :]
                           for c in range(ROW_TILES)], axis=1)
    o_ref[...] = x1_ref[...] + mod_ref[5:6, :] * (_rms(ffn) * g_ref[...])


def _final(y_sorted, pos, wts, x1, mod8, g_post):
    t, d = x1.shape
    tm = FINAL_TM
    nt = t // tm
    idx3 = pos.reshape(nt, tm, 2).transpose(0, 2, 1).reshape(nt, 1, 2 * tm)
    return pl.pallas_call(
        _final_body,
        grid=(nt,),
        in_specs=_gather_specs(nt) + [pl.BlockSpec((tm, 128), lambda m: (m, 0)),
                                      pl.BlockSpec((tm, d), lambda m: (m, 0)),
                                      pl.BlockSpec((8, d), lambda m: (0, 0)),
                                      pl.BlockSpec((1, d), lambda m: (0, 0))],
        out_specs=pl.BlockSpec((tm, d), lambda m: (m, 0)),
        out_shape=jax.ShapeDtypeStruct((t, d), F32),
        scratch_shapes=GATHER_SCRATCH,
        compiler_params=_cparams(("arbitrary",)),
        name="final",
    )(idx3, idx3, y_sorted, wts, x1, mod8, g_post)


PLAN_TT = 512


def _plan_body(ids_ref, tril_ref, upper_ref, dest_ref, cnt_ref, run_ref):
    p = pl.program_id(0)
    i = pl.program_id(1)

    @pl.when((p == 0) & (i == 0))
    def _():
        run_ref[...] = jnp.zeros_like(run_ref)

    @pl.when((p == 1) & (i == 0))
    def _():
        counts = run_ref[...]
        cnt_ref[...] = counts.astype(I32)
        nblk = jnp.floor((counts + (MOE_BLOCK - 1.0)) * (1.0 / MOE_BLOCK))
        blk_start = _dot(nblk.astype(BF16), upper_ref[...])
        run_ref[...] = blk_start * MOE_BLOCK

    ids = ids_ref[...]
    lane = lax.broadcasted_iota(I32, ids.shape, 1)
    oh0 = lane == ids[:, 0:1]
    oh1 = lane == ids[:, 1:2]
    both = jnp.where(oh0 | oh1, 1.0, 0.0).astype(BF16)

    @pl.when(p == 1)
    def _():
        nxt = _dot(tril_ref[...], both) + run_ref[0:1, :]
        d0 = jnp.sum(jnp.where(oh0, nxt, 0.0), axis=-1, keepdims=True)
        d1 = jnp.sum(jnp.where(oh1, nxt, 0.0), axis=-1, keepdims=True)
        dest_ref[...] = jnp.where(lane == 0, d0, jnp.where(lane == 1, d1, 0.0)).astype(I32)

    run_ref[...] = run_ref[...] + _dot(jnp.ones((8, PLAN_TT), BF16), both)


def _plan(ids):
    t = ids.shape[0]
    r = np.arange(PLAN_TT)
    e = np.arange(128)
    strict_tril = jnp.asarray(r[:, None] > r[None, :], BF16)
    strict_upper = jnp.asarray(e[:, None] < e[None, :], BF16)
    return pl.pallas_call(
        _plan_body,
        grid=(2, t // PLAN_TT),
        in_specs=[pl.BlockSpec((PLAN_TT, 128), lambda p, i: (i, 0)),
                  pl.BlockSpec((PLAN_TT, PLAN_TT), lambda p, i: (0, 0)),
                  pl.BlockSpec((128, 128), lambda p, i: (0, 0))],
        out_specs=[pl.BlockSpec((PLAN_TT, 128), lambda p, i: (i * p, 0)),
                   pl.BlockSpec((8, 128), lambda p, i: (0, 0))],
        out_shape=[jax.ShapeDtypeStruct((t, 128), I32), jax.ShapeDtypeStruct((8, 128), I32)],
        scratch_shapes=[pltpu.VMEM((8, 128), F32)],
        compiler_params=_cparams(("arbitrary", "arbitrary")),
        name="plan",
    )(ids, strict_tril, strict_upper)


def _routing_tables(ids128, n_tok):
    n_assign = n_tok * 2
    n_blocks = n_assign // MOE_BLOCK + N_EXPERTS
    dest128, cnt = _plan(ids128)
    dest = dest128[:, :2].reshape(n_assign)
    counts = cnt[0, :N_EXPERTS]
    nb = (counts + MOE_BLOCK - 1) // MOE_BLOCK
    blk_start = jnp.cumsum(nb) - nb
    n_rows = n_blocks * MOE_BLOCK
    row_tok = (jnp.arange(n_rows, dtype=I32) % n_tok).at[dest].set(jnp.arange(n_assign, dtype=I32) // 2)
    pos = dest.reshape(n_tok, 2)

    n_it = (nb + ITEM_BLOCKS - 1) // ITEM_BLOCKS
    it_end = jnp.cumsum(n_it)
    it_start = it_end - n_it
    n_real = it_end[-1]
    i = jnp.arange(N_ITEMS, dtype=I32)
    e_i = jnp.minimum(jnp.searchsorted(it_end, i, side='right').astype(I32), N_EXPERTS - 1)
    k_i = i - it_start[e_i]
    valid = i < n_real
    last_e = e_i[jnp.maximum(n_real - 1, 0)]
    item_e = jnp.where(valid, e_i, last_e).astype(I32)
    fill_off = jnp.sum(nb) + ITEM_BLOCKS * (i - n_real)
    item_fill = jnp.where(valid, 0, jnp.clip(n_blocks - fill_off, 0, ITEM_BLOCKS)).astype(I32)
    item_off = jnp.where(valid, blk_start[e_i] + ITEM_BLOCKS * k_i, jnp.minimum(fill_off, n_blocks - 1)).astype(I32)
    item_nsub = jnp.where(valid, jnp.clip(nb[e_i] - ITEM_BLOCKS * k_i, 0, ITEM_BLOCKS), 0).astype(I32)
    return row_tok, pos, item_e, item_off, item_nsub, item_fill, n_real.reshape(1).astype(I32)


def _pad_lanes(v, start, total=SMALL_COLS):
    return jnp.zeros((1, total), F32).at[0, start:start + v.shape[0]].set(v)


def _layer(x2, c, w_ada, b_ada, norm_pre_mix, norm_post_mix, norm_pre_ffn, norm_post_ffn,
           w_in, gla_w_gate_up, gla_b_gate, gla_norm, ssm_conv_w, ssm_conv_b, ssm_dt_bias,
           ssm_a_log, ssm_d, ssm_norm, w_branch_gla, w_branch_ssm, w_out,
           router_group, router_expert, moe_w_gate, moe_w_up, moe_w_down):
    t, d = x2.shape
    row = lambda v: v.reshape(1, -1)

    mod = _ada(c, w_ada, b_ada)
    mod8 = jnp.concatenate([mod.reshape(6, d), jnp.zeros((2, d), F32)], axis=0)

    w_big, w_small = _repack(w_in.T)
    p, small = _inproj(x2, mod8, row(norm_pre_mix), w_big, w_small)

    o_gla = _gla(p, small, gla_w_gate_up, row(gla_b_gate), row(gla_norm))
    y_ssm = _ssd(p, small, ssm_conv_w, row(ssm_conv_b),
                 _pad_lanes(ssm_dt_bias, SMALL_DT0), _pad_lanes(ssm_a_log, SMALL_DT0),
                 _pad_lanes(ssm_d, SMALL_DT0), row(ssm_norm))
    merged = _merge(o_gla, y_ssm, w_branch_gla, w_branch_ssm, p)

    w_router = jnp.concatenate([router_group, router_expert,
                                jnp.zeros((d, 128 - MOE_GROUPS - N_EXPERTS), F32)], axis=1)
    x1, h2f, logits = _outproj(merged, w_out.astype(BF16), x2, mod8,
                               row(norm_post_mix), row(norm_pre_ffn), w_router)
    ids, wts = _route(logits)

    row_tok, pos, item_e, item_off, item_nsub, item_fill, n_real = _routing_tables(ids, t)
    y_sorted = _experts(h2f, row_tok, moe_w_gate, moe_w_up, moe_w_down,
                        item_e, item_off, item_nsub, item_fill, n_real)
    return _final(y_sorted, pos, wts, x1, mod8, row(norm_post_ffn))


def kernel(x, c, w_ada, b_ada, norm_pre_mix, norm_post_mix, norm_pre_ffn, norm_post_ffn, w_in, gla_w_gate_up, gla_b_gate, gla_norm, ssm_conv_w, ssm_conv_b, ssm_dt_bias, ssm_a_log, ssm_d, ssm_norm, w_branch_gla, w_branch_ssm, w_out, router_group, router_expert, moe_w_gate, moe_w_up, moe_w_down):
    bsz, seq, d = x.shape
    assert bsz == 1 and d == D_MODEL
    x2 = x.reshape(seq, d)
    params = (w_ada, b_ada, norm_pre_mix, norm_post_mix, norm_pre_ffn, norm_post_ffn, w_in, gla_w_gate_up,
              gla_b_gate, gla_norm, ssm_conv_w, ssm_conv_b, ssm_dt_bias, ssm_a_log, ssm_d, ssm_norm,
              w_branch_gla, w_branch_ssm, w_out, router_group, router_expert, moe_w_gate, moe_w_up, moe_w_down)
    for layer in range(w_ada.shape[0]):
        x2 = _layer(x2, c, *(prm[layer] for prm in params))
    return x2.reshape(bsz, seq, d)
```

```python
import functools

import jax
import jax.numpy as jnp
import numpy as np
from jax import lax
from jax.experimental import pallas as pl
from jax.experimental.pallas import tpu as pltpu

F32 = jnp.float32
BF16 = jnp.bfloat16
I32 = jnp.int32

D_MODEL = 2048
EPS = 1e-6
LANES = 128
ROW_TILES = D_MODEL // LANES

GLA_HEADS = 4
GLA_HEAD_K = 256
GLA_HEAD_V = 512
GLA_RANK = 16
GLA_NORMALIZER = 16.0
CHUNK = 64

SSM_GROUPS = 8
SSM_HEADS = 64
SSM_HEAD_DIM = 64
SSM_STATE = 128
SSM_CONV = 4
SSM_GROUP_W = 512
SSM_INNER = 4096

N_EXPERTS = 64
EXPERTS_PER_GROUP = 8
MOE_GROUPS = 8
MOE_FF = 1024
MOE_BLOCK = 128
ITEM_BLOCKS = 4

COL_Q, COL_K, COL_V, COL_OG, COL_Z, COL_XS, COL_B, COL_C, COL_GG, COL_GS = (
    0, 1024, 2048, 4096, 6144, 10240, 14336, 15360, 16384, 18432)
P_COLS = 20480
SMALL_COLS = 128
SMALL_DT0 = GLA_RANK

VMEM_LIMIT = 56 * 1024 * 1024


def _cparams(sem, vmem=VMEM_LIMIT):
    return pltpu.CompilerParams(dimension_semantics=sem, vmem_limit_bytes=vmem)


def _dot(a, b):
    return jnp.dot(a, b, preferred_element_type=F32)


def _dot_nt(a, b):
    return lax.dot_general(a, b, (((1,), (1,)), ((), ())), preferred_element_type=F32)


def _dot_tn(a, b):
    return lax.dot_general(a, b, (((0,), (0,)), ((), ())), preferred_element_type=F32)


def _split3(a):
    hi = a.astype(BF16)
    r1 = a - hi.astype(F32)
    mid = r1.astype(BF16)
    lo = (r1 - mid.astype(F32)).astype(BF16)
    return hi, mid, lo


def _dot_sel_r(a, sel):
    hi, mid, lo = _split3(a)
    return _dot(hi, sel) + _dot(mid, sel) + _dot(lo, sel)


def _dot_sel_l(sel, a):
    hi, mid, lo = _split3(a)
    return _dot(sel, hi) + _dot(sel, mid) + _dot(sel, lo)


def _sigmoid(x):
    return 1.0 / (1.0 + jnp.exp(-x))


def _silu(x):
    return x * _sigmoid(x)


def _softplus(x):
    return jnp.maximum(x, 0.0) + jnp.log1p(jnp.exp(-jnp.abs(x)))


def _log_sigmoid(x):
    return jnp.minimum(x, 0.0) - jnp.log1p(jnp.exp(-jnp.abs(x)))


def _rms(x):
    return x * lax.rsqrt(jnp.mean(x * x, axis=-1, keepdims=True) + EPS)


def _ada_body(c_ref, w_ref, b_ref, o_ref):
    c = c_ref[...]
    s = jnp.broadcast_to(_silu(c), (8, c.shape[1])).astype(BF16)
    o_ref[...] = _dot(s, w_ref[...].astype(BF16))[0:1] + b_ref[...]


def _ada(c, w_ada, b_ada):
    d, n = w_ada.shape
    tn = 1024
    return pl.pallas_call(
        _ada_body,
        grid=(n // tn,),
        in_specs=[pl.BlockSpec((1, d), lambda j: (0, 0)),
                  pl.BlockSpec((d, tn), lambda j: (0, j)),
                  pl.BlockSpec((1, tn), lambda j: (0, j))],
        out_specs=pl.BlockSpec((1, tn), lambda j: (0, j)),
        out_shape=jax.ShapeDtypeStruct((1, n), F32),
        compiler_params=_cparams(("parallel",)),
        name="ada",
    )(c, w_ada, b_ada.reshape(1, n))


W_GLR, W_MID, W_DT, W_GATES, W_END = 4096, 4112, 16400, 16464, 20560
REPACK_ROWS = 1024
REPACK_TAIL = 128


def _repack_body(a_ref, b_ref, big_ref, small_ref):
    j = pl.program_id(0)
    n = REPACK_ROWS

    def emit(shift):
        if shift == 0:
            big_ref[...] = a_ref[...].astype(BF16)
        else:
            big_ref[0:n - shift, :] = a_ref[shift:n, :].astype(BF16)
            big_ref[n - shift:n, :] = b_ref[0:shift, :].astype(BF16)

    first_mid, first_gates = COL_OG // n, COL_GG // n

    @pl.when(j < first_mid)
    def _():
        emit(0)

    @pl.when((j >= first_mid) & (j < first_gates))
    def _():
        emit(W_MID - COL_OG)

    @pl.when(j >= first_gates)
    def _():
        emit(W_GATES - COL_GG)

    @pl.when(j == 0)
    def _():
        small_ref[...] = jnp.zeros_like(small_ref)

    @pl.when(j == W_GLR // n)
    def _():
        small_ref[0:GLA_RANK, :] = a_ref[0:GLA_RANK, :].astype(BF16)

    @pl.when(j == W_DT // n)
    def _():
        small_ref[SMALL_DT0:SMALL_DT0 + SSM_HEADS, :] = a_ref[SMALL_DT0:SMALL_DT0 + SSM_HEADS, :].astype(BF16)


def _repack(w_t):
    n_in, d = w_t.shape
    n = REPACK_ROWS
    assert n_in == W_END and W_GLR % n == 0 and W_GLR + GLA_RANK == W_MID and W_DT % n == SMALL_DT0
    assert COL_OG % n == 0 and COL_GG % n == 0 and W_GATES - COL_GG <= REPACK_TAIL
    return pl.pallas_call(
        _repack_body,
        grid=(P_COLS // n,),
        in_specs=[pl.BlockSpec((n, d), lambda j: (j, 0)),
                  pl.BlockSpec((REPACK_TAIL, d), lambda j: ((j + 1) * (n // REPACK_TAIL), 0))],
        out_specs=[pl.BlockSpec((n, d), lambda j: (j, 0)),
                   pl.BlockSpec((SMALL_COLS, d), lambda j: (0, 0))],
        out_shape=[jax.ShapeDtypeStruct((P_COLS, d), BF16), jax.ShapeDtypeStruct((SMALL_COLS, d), BF16)],
        compiler_params=_cparams(("arbitrary",)),
        name="repack",
    )(w_t, w_t)


def _inproj_body(x_ref, mod_ref, g_ref, w_ref, ws_ref, p_ref, s_ref, h_ref):
    @pl.when(pl.program_id(1) == 0)
    def _():
        h = _rms(x_ref[...]) * g_ref[...] * (1.0 + mod_ref[1:2, :]) + mod_ref[0:1, :]
        hb = h.astype(BF16)
        h_ref[...] = hb
        s_ref[...] = _dot_nt(hb, ws_ref[...])

    p_ref[...] = _dot_nt(h_ref[...], w_ref[...]).astype(BF16)


def _inproj(x2, mod8, gain, w_big, w_small):
    t, d = x2.shape
    n = w_big.shape[0]
    tm, tn = 1024, 2048
    return pl.pallas_call(
        _inproj_body,
        grid=(t // tm, n // tn),
        in_specs=[pl.BlockSpec((tm, d), lambda m, j: (m, 0)),
                  pl.BlockSpec((8, d), lambda m, j: (0, 0)),
                  pl.BlockSpec((1, d), lambda m, j: (0, 0)),
                  pl.BlockSpec((tn, d), lambda m, j: (j, 0)),
                  pl.BlockSpec((SMALL_COLS, d), lambda m, j: (0, 0))],
        out_specs=[pl.BlockSpec((tm, tn), lambda m, j: (m, j)),
                   pl.BlockSpec((tm, SMALL_COLS), lambda m, j: (m, 0))],
        out_shape=[jax.ShapeDtypeStruct((t, n), BF16),
                   jax.ShapeDtypeStruct((t, SMALL_COLS), F32)],
        scratch_shapes=[pltpu.VMEM((tm, d), BF16)],
        compiler_params=_cparams(("parallel", "arbitrary")),
        name="inproj",
    )(x2, mod8, gain, w_big, w_small)


GLA_TB = 512
GLA_SCORE_ROWS = 256


def _gla_body(q_ref, k_ref, v_ref, og_ref, sm_ref, wup_ref, bg_ref, gn_ref, o_ref, st_ref):
    @pl.when(pl.program_id(1) == 0)
    def _():
        st_ref[...] = jnp.zeros_like(st_ref)

    tb, C = GLA_TB, CHUNK
    nch = tb // C
    sb = GLA_SCORE_ROWS
    r = lax.broadcasted_iota(I32, (sb, sb), 0)
    c = lax.broadcasted_iota(I32, (sb, sb), 1)
    causal = (r // C == c // C) & (r >= c)
    r2 = lax.broadcasted_iota(I32, (2 * C, 2 * C), 0)
    c2 = lax.broadcasted_iota(I32, (2 * C, 2 * C), 1)
    tril2 = ((r2 // C == c2 // C) & (r2 >= c2)).astype(BF16)

    glr = sm_ref[:, 0:GLA_RANK].astype(BF16)
    pre = _dot(glr, wup_ref[...].astype(BF16)) + bg_ref[...]
    log_a = _log_sigmoid(pre) / GLA_NORMALIZER
    b = jnp.concatenate([_dot_sel_l(tril2, log_a[i * 2 * C:(i + 1) * 2 * C]) for i in range(nch // 2)],
                        axis=0)
    b_last = [b[(i + 1) * C - 1:(i + 1) * C, :] for i in range(nch)]
    b_end = jnp.concatenate([jnp.broadcast_to(bl, (C, bl.shape[1])) for bl in b_last], axis=0)

    q = q_ref[...].astype(F32) * (GLA_HEAD_K ** -0.5)
    k = k_ref[...].astype(F32)
    v = v_ref[...]
    q_dec = (q * jnp.exp(b)).astype(BF16)
    k_inv = (k * jnp.exp(-b)).astype(BF16)
    k_end = (k * jnp.exp(b_end - b)).astype(BF16)
    o_intra = []
    for i in range(tb // sb):
        rows = slice(i * sb, (i + 1) * sb)
        scores = jnp.where(causal, _dot_nt(q_dec[rows], k_inv[rows]), 0.0)
        o_intra.append(_dot(scores.astype(BF16), v[rows]))
    o_intra = jnp.concatenate(o_intra, axis=0)

    st = st_ref[...]
    o_inter = []
    for i in range(nch):
        rows = slice(i * C, (i + 1) * C)
        o_inter.append(_dot_nt(q_dec[rows], st.astype(BF16)))
        st = st * jnp.exp(b_last[i]) + _dot_tn(v[rows], k_end[rows])
    st_ref[...] = st

    o = o_intra + jnp.concatenate(o_inter, axis=0)
    og = og_ref[...].astype(F32)
    o_ref[...] = (_rms(o) * gn_ref[...] * _silu(og)).astype(BF16)


def _gla(p, small, wup, bg, gn):
    t = p.shape[0]
    tb = GLA_TB
    kb, vb = GLA_HEAD_K, GLA_HEAD_V
    return pl.pallas_call(
        _gla_body,
        grid=(GLA_HEADS, t // tb),
        in_specs=[pl.BlockSpec((tb, kb), lambda h, i: (i, COL_Q // kb + h)),
                  pl.BlockSpec((tb, kb), lambda h, i: (i, COL_K // kb + h)),
                  pl.BlockSpec((tb, vb), lambda h, i: (i, COL_V // vb + h)),
                  pl.BlockSpec((tb, vb), lambda h, i: (i, COL_OG // vb + h)),
                  pl.BlockSpec((tb, SMALL_COLS), lambda h, i: (i, 0)),
                  pl.BlockSpec((GLA_RANK, kb), lambda h, i: (0, h)),
                  pl.BlockSpec((1, kb), lambda h, i: (0, h)),
                  pl.BlockSpec((1, vb), lambda h, i: (0, h))],
        out_specs=pl.BlockSpec((tb, vb), lambda h, i: (i, h)),
        out_shape=jax.ShapeDtypeStruct((t, GLA_HEADS * vb), BF16),
        scratch_shapes=[pltpu.VMEM((vb, kb), F32)],
        compiler_params=_cparams(("parallel", "arbitrary")),
        name="gla",
    )(p, p, p, p, small, wup, bg, gn)


SSD_TB = 512
HALO = 8


def _ssd_body(xs_ref, b_ref, c_ref, z_ref, sm_ref, cwx_ref, cwb_ref, cwc_ref, cbx_ref, cbb_ref, cbc_ref,
              dtb_ref, alog_ref, dsk_ref, ng_ref, o_ref,
              ex_ref, eb_ref, ec_ref, at_ref, ht_ref):
    g = pl.program_id(0)
    t = pl.program_id(1)
    tb = SSD_TB
    gw = SSM_GROUP_W

    @pl.when(t == 0)
    def _():
        ht_ref[...] = jnp.zeros_like(ht_ref)
        ex_ref[0:HALO, :] = jnp.zeros((HALO, gw), F32)
        eb_ref[0:HALO, :] = jnp.zeros((HALO, SSM_STATE), F32)
        ec_ref[0:HALO, :] = jnp.zeros((HALO, SSM_STATE), F32)

    @pl.when(t > 0)
    def _():
        ex_ref[0:HALO, :] = ex_ref[tb:tb + HALO, :]
        eb_ref[0:HALO, :] = eb_ref[tb:tb + HALO, :]
        ec_ref[0:HALO, :] = ec_ref[tb:tb + HALO, :]

    def conv_silu(u_ref, e_ref, w_ref, bias_ref):
        e_ref[HALO:HALO + tb, :] = u_ref[...].astype(F32)
        acc = bias_ref[...] + w_ref[0:1, :] * e_ref[HALO - 3:HALO - 3 + tb, :]
        for kk in range(1, SSM_CONV):
            acc = acc + w_ref[kk:kk + 1, :] * e_ref[HALO - 3 + kk:HALO - 3 + kk + tb, :]
        return _silu(acc)

    xa = conv_silu(xs_ref, ex_ref, cwx_ref, cbx_ref)
    ba = conv_silu(b_ref, eb_ref, cwb_ref, cbb_ref).astype(BF16)
    ca = conv_silu(c_ref, ec_ref, cwc_ref, cbc_ref).astype(BF16)

    L = CHUNK
    nch = tb // L
    rep = gw // L
    hpg = gw // SSM_HEAD_DIM
    head0 = pl.multiple_of(SMALL_DT0 + g * hpg, hpg)

    e_row = lax.broadcasted_iota(I32, (SMALL_COLS, gw), 0)
    e_col = lax.broadcasted_iota(I32, (SMALL_COLS, gw), 1)
    expand = (e_row == head0 + e_col // SSM_HEAD_DIM).astype(BF16)

    dt_small = _softplus(sm_ref[...] + dtb_ref[...])
    adt_small = dt_small * (-jnp.exp(alog_ref[...]))
    rb = lax.broadcasted_iota(I32, (2 * L, 2 * L), 0)
    cb = lax.broadcasted_iota(I32, (2 * L, 2 * L), 1)
    tril2 = ((rb // L == cb // L) & (rb >= cb)).astype(BF16)
    acum_small = jnp.concatenate([_dot_sel_l(tril2, adt_small[k * 2 * L:(k + 1) * 2 * L])
                                  for k in range(nch // 2)], axis=0)
    at_ref[...] = acum_small.T
    heads = at_ref[pl.ds(head0, hpg), :]
    dt_exp = _dot_sel_r(dt_small, expand)
    acum = _dot_sel_r(acum_small, expand)
    d_exp = _dot_sel_r(jnp.broadcast_to(dsk_ref[...], (8, SMALL_COLS)), expand)[0:1]
    xdt = xa * dt_exp
    xdt_b = xdt.astype(BF16)
    e_acum = jnp.exp(acum)

    li = lax.broadcasted_iota(I32, (L, gw), 0)
    lj = lax.broadcasted_iota(I32, (L, gw), 1) % L
    causal_t = li >= lj
    hi = lax.broadcasted_iota(I32, (hpg, gw), 0)
    hj = lax.broadcasted_iota(I32, (hpg, gw), 1) // L
    headmask = hi == hj
    ones_h = jnp.ones((L, hpg), BF16)
    bi = lax.broadcasted_iota(I32, (gw, gw), 0) // L
    bj = lax.broadcasted_iota(I32, (gw, gw), 1) // SSM_HEAD_DIM
    blockmask = bi == bj
    masked_out = -1e30

    ht = ht_ref[...]
    for c in range(nch):
        rows = slice(c * L, (c + 1) * L)
        acum_c = acum[rows]
        a_rows = jnp.concatenate([heads[:, c * L:(c + 1) * L]] * rep, axis=1)
        rterm = _dot_sel_l(ones_h, jnp.where(headmask, a_rows, 0.0))
        decay = jnp.exp(jnp.where(causal_t, acum_c - rterm, masked_out))
        cc = ca[rows]
        bc = ba[rows]
        cb_t = _dot_nt(cc, jnp.concatenate([bc] * rep, axis=0))
        m = (cb_t * decay).astype(BF16)
        bd = jnp.where(blockmask, jnp.concatenate([xdt_b[rows]] * rep, axis=0), jnp.zeros((), BF16))
        y_diag = _dot(m, bd)
        y_off = _dot(cc, ht.astype(BF16)) * e_acum[rows]
        a_last = acum_c[L - 1:L, :]
        xd = (xdt[rows] * jnp.exp(a_last - acum_c)).astype(BF16)
        ht = ht * jnp.exp(a_last) + _dot_tn(bc, xd)
        y = y_diag + y_off + d_exp * xa[rows]
        y = y * _silu(z_ref[rows, :].astype(F32))
        o_ref[rows, :] = (_rms(y) * ng_ref[...]).astype(BF16)
    ht_ref[...] = ht


def _ssd(p, small, conv_w, conv_b, dt_bias_s, a_log_s, d_skip_s, norm_g):
    t = p.shape[0]
    tb = SSD_TB
    gw, ns = SSM_GROUP_W, SSM_STATE
    xs0, b0, c0 = 0, SSM_INNER // ns, (SSM_INNER + SSM_GROUPS * ns) // ns
    row = lambda w, off: pl.BlockSpec((1, w), lambda g, i: (0, off + g))
    return pl.pallas_call(
        _ssd_body,
        grid=(SSM_GROUPS, t // tb),
        in_specs=[pl.BlockSpec((tb, gw), lambda g, i: (i, COL_XS // gw + g)),
                  pl.BlockSpec((tb, ns), lambda g, i: (i, COL_B // ns + g)),
                  pl.BlockSpec((tb, ns), lambda g, i: (i, COL_C // ns + g)),
                  pl.BlockSpec((tb, gw), lambda g, i: (i, COL_Z // gw + g)),
                  pl.BlockSpec((tb, SMALL_COLS), lambda g, i: (i, 0)),
                  pl.BlockSpec((SSM_CONV, gw), lambda g, i: (0, xs0 + g)),
                  pl.BlockSpec((SSM_CONV, ns), lambda g, i: (0, b0 + g)),
                  pl.BlockSpec((SSM_CONV, ns), lambda g, i: (0, c0 + g)),
                  row(gw, xs0), row(ns, b0), row(ns, c0),
                  pl.BlockSpec((1, SMALL_COLS), lambda g, i: (0, 0)),
                  pl.BlockSpec((1, SMALL_COLS), lambda g, i: (0, 0)),
                  pl.BlockSpec((1, SMALL_COLS), lambda g, i: (0, 0)),
                  pl.BlockSpec((1, gw), lambda g, i: (0, g))],
        out_specs=pl.BlockSpec((tb, gw), lambda g, i: (i, g)),
        out_shape=jax.ShapeDtypeStruct((t, SSM_INNER), BF16),
        scratch_shapes=[pltpu.VMEM((tb + HALO, gw), F32),
                        pltpu.VMEM((tb + HALO, ns), F32),
                        pltpu.VMEM((tb + HALO, ns), F32),
                        pltpu.VMEM((SMALL_COLS, tb), F32),
                        pltpu.VMEM((ns, gw), F32)],
        compiler_params=_cparams(("parallel", "arbitrary")),
        name="ssd",
    )(p, p, p, p, small, conv_w, conv_w, conv_w, conv_b, conv_b, conv_b,
      dt_bias_s, a_log_s, d_skip_s, norm_g)


def _merge_body(a1_ref, a2_ref, w1_ref, w2_ref, gg_ref, gs_ref, o_ref, w1b_ref, w2b_ref):
    @pl.when(pl.program_id(1) == 0)
    def _():
        w1b_ref[...] = w1_ref[...].astype(BF16)
        w2b_ref[...] = w2_ref[...].astype(BF16)

    y1 = _dot(a1_ref[...], w1b_ref[...])
    y2 = _dot(a2_ref[...], w2b_ref[...])
    o_ref[...] = (_sigmoid(gg_ref[...].astype(F32)) * y1 + _sigmoid(gs_ref[...].astype(F32)) * y2).astype(BF16)


def _merge(o_gla, y_ssm, w1, w2, p):
    t, k1 = o_gla.shape
    k2 = y_ssm.shape[1]
    n = w1.shape[1]
    tm, tn = 512, 512
    return pl.pallas_call(
        _merge_body,
        grid=(n // tn, t // tm),
        in_specs=[pl.BlockSpec((tm, k1), lambda j, m: (m, 0)),
                  pl.BlockSpec((tm, k2), lambda j, m: (m, 0)),
                  pl.BlockSpec((k1, tn), lambda j, m: (0, j)),
                  pl.BlockSpec((k2, tn), lambda j, m: (0, j)),
                  pl.BlockSpec((tm, tn), lambda j, m: (m, COL_GG // tn + j)),
                  pl.BlockSpec((tm, tn), lambda j, m: (m, COL_GS // tn + j))],
        out_specs=pl.BlockSpec((tm, tn), lambda j, m: (m, j)),
        out_shape=jax.ShapeDtypeStruct((t, n), BF16),
        scratch_shapes=[pltpu.VMEM((k1, tn), BF16), pltpu.VMEM((k2, tn), BF16)],
        compiler_params=_cparams(("parallel", "arbitrary")),
        name="merge",
    )(o_gla, y_ssm, w1, w2, p, p)


def _outproj_body(m_ref, w_ref, x_ref, mod_ref, gpost_ref, gpre_ref, wr_ref, x1_ref, hf_ref, lg_ref):
    mix = _dot(m_ref[...], w_ref[...])
    x1 = x_ref[...] + mod_ref[2:3, :] * (_rms(mix) * gpost_ref[...])
    x1_ref[...] = x1
    h = _rms(x1) * gpre_ref[...] * (1.0 + mod_ref[4:5, :]) + mod_ref[3:4, :]
    for s in range(ROW_TILES):
        hf_ref[:, s, :] = h[:, s * LANES:(s + 1) * LANES]
    h_hi = h.astype(BF16)
    h_lo = (h - h_hi.astype(F32)).astype(BF16)
    wr = wr_ref[...]
    w_hi = wr.astype(BF16)
    w_lo = (wr - w_hi.astype(F32)).astype(BF16)
    lg_ref[...] = _dot(h_hi, w_hi) + _dot(h_hi, w_lo) + _dot(h_lo, w_hi)


def _outproj(merged, w_out, x2, mod8, g_post, g_pre, w_router):
    t, d = x2.shape
    tm = 256
    full = lambda r, c: pl.BlockSpec((r, c), lambda m: (0, 0))
    tile = lambda c: pl.BlockSpec((tm, c), lambda m: (m, 0))
    return pl.pallas_call(
        _outproj_body,
        grid=(t // tm,),
        in_specs=[tile(d), full(d, d), tile(d), full(8, d), full(1, d), full(1, d), full(d, 128)],
        out_specs=[tile(d), pl.BlockSpec((tm, ROW_TILES, LANES), lambda m: (m, 0, 0)), tile(128)],
        out_shape=[jax.ShapeDtypeStruct((t, d), F32),
                   jax.ShapeDtypeStruct((t, ROW_TILES, LANES), F32),
                   jax.ShapeDtypeStruct((t, 128), F32)],
        compiler_params=_cparams(("parallel",)),
        name="outproj",
    )(merged, w_out, x2, mod8, g_post, g_pre, w_router)


def _route_body(lg_ref, id_ref, w_ref):
    lg = lg_ref[...]
    lane = lax.broadcasted_iota(I32, lg.shape, 1)
    lane_f = lane.astype(F32)
    neg = jnp.float32(-jnp.inf)

    def first_argmax(vals, mx):
        return jnp.min(jnp.where(vals == mx, lane_f, 1e9), axis=-1, keepdims=True).astype(I32)

    gmask = lane < MOE_GROUPS
    gl = jnp.where(gmask, lg, neg)
    gmax = jnp.max(gl, axis=-1, keepdims=True)
    gsum = jnp.sum(jnp.where(gmask, jnp.exp(gl - gmax), 0.0), axis=-1, keepdims=True)
    g_w = 1.0 / gsum
    g_idx = first_argmax(gl, gmax)
    lo = MOE_GROUPS + g_idx * EXPERTS_PER_GROUP
    emask = (lane >= lo) & (lane < lo + EXPERTS_PER_GROUP)
    el = jnp.where(emask, lg, neg)
    m1 = jnp.max(el, axis=-1, keepdims=True)
    i1 = first_argmax(el, m1)
    el2 = jnp.where(lane == i1, neg, el)
    m2 = jnp.max(el2, axis=-1, keepdims=True)
    i2 = first_argmax(el2, m2)
    r = jnp.exp(m2 - m1)
    w1 = g_w / (1.0 + r)
    w2 = g_w * r / (1.0 + r)
    id_ref[...] = jnp.where(lane == 0, i1 - MOE_GROUPS, jnp.where(lane == 1, i2 - MOE_GROUPS, 0))
    w_ref[...] = jnp.where(lane == 0, w1, jnp.where(lane == 1, w2, 0.0))


def _route(logits):
    t = logits.shape[0]
    tm = 1024
    spec = pl.BlockSpec((tm, 128), lambda m: (m, 0))
    return pl.pallas_call(
        _route_body,
        grid=(t // tm,),
        in_specs=[spec],
        out_specs=[spec, spec],
        out_shape=[jax.ShapeDtypeStruct((t, 128), I32), jax.ShapeDtypeStruct((t, 128), F32)],
        compiler_params=_cparams(("parallel",)),
        name="route",
    )(logits)


GATHER_ROWS = 512


def _issue_rows(idx_ref, src_ref, buf_ref, slot, sem, skip_pads):
    def body(r, carry):
        tok = idx_ref[0, 0, r]
        copy = pltpu.make_async_copy(src_ref.at[jnp.maximum(tok, 0)], buf_ref.at[slot, :, r, :], sem.at[slot])
        if skip_pads:
            @pl.when(tok >= 0)
            def _():
                copy.start()
        else:
            copy.start()
        return carry

    lax.fori_loop(0, GATHER_ROWS, body, 0, unroll=8)


def _wait_rows(src_ref, buf_ref, slot, sem, count):
    del src_ref
    if count is None:
        pltpu.make_async_copy(buf_ref.at[slot], buf_ref.at[slot], sem.at[slot]).wait()
    else:
        @pl.when(count > 0)
        def _():
            part = buf_ref.at[slot, :, pl.ds(0, count), :]
            pltpu.make_async_copy(part, part, sem.at[slot]).wait()


def _gather_step(idx_ref, idx_next_ref, src_ref, buf_ref, sem, cnt_ref=None):
    i = pl.program_id(0)
    slot = i % 2
    skip_pads = cnt_ref is not None

    @pl.when(i == 0)
    def _():
        if skip_pads:
            buf_ref[...] = jnp.zeros_like(buf_ref)
        _issue_rows(idx_ref, src_ref, buf_ref, 0, sem, skip_pads)

    @pl.when(i + 1 < pl.num_programs(0))
    def _():
        _issue_rows(idx_next_ref, src_ref, buf_ref, 1 - slot, sem, skip_pads)

    _wait_rows(src_ref, buf_ref, slot, sem, cnt_ref[i] if skip_pads else None)
    return slot


def _gather_specs(nsteps):
    smem = lambda f: pl.BlockSpec((1, 1, GATHER_ROWS), f, memory_space=pltpu.SMEM)
    return [smem(lambda i: (i, 0, 0)),
            smem(lambda i: (jnp.minimum(i + 1, nsteps - 1), 0, 0)),
            pl.BlockSpec(memory_space=pl.ANY)]


GATHER_SCRATCH = [pltpu.VMEM((2, ROW_TILES, GATHER_ROWS, LANES), F32), pltpu.SemaphoreType.DMA((2,))]


MOE_FC = 512
MOE_J = MOE_FF // MOE_FC
N_ITEMS = (16384 // MOE_BLOCK + N_EXPERTS) // ITEM_BLOCKS + (N_EXPERTS * (ITEM_BLOCKS - 1)) // ITEM_BLOCKS


def _experts_body(ie_ref, io_ref, ins_ref, ifl_ref, nr_ref, wg_ref, wu_ref, wd_ref, h_ref,
                  f0_ref, f1_ref, f2_ref, f3_ref, n0_ref, n1_ref, n2_ref, n3_ref, ys_ref,
                  xst_ref, xb_ref, acc_ref, sem_in, sem_out):
    i = pl.program_id(0)
    j = pl.program_id(1)
    n_items = pl.num_programs(0)
    nsub = ins_ref[i]
    nfill = ifl_ref[i]
    slot = i % 2
    blk = MOE_BLOCK

    def row0(item):
        return pl.multiple_of(io_ref[item] * blk, blk)

    def start_rows(idx_refs, count, sl):
        def block(s):
            def body(r, carry):
                pltpu.make_async_copy(h_ref.at[idx_refs[s][0, 0, r]], xst_ref.at[sl, :, s * blk + r, :],
                                      sem_in.at[sl]).start()
                return carry

            lax.fori_loop(0, blk, body, 0, unroll=8)

        for_blocks(count, block)

    def wait_rows(count, sl):
        @pl.when(count > 0)
        def _():
            part = xst_ref.at[sl, :, pl.ds(0, count * blk), :]
            pltpu.make_async_copy(part, part, sem_in.at[sl]).wait()

    def start_y(item, s):
        for c in range(ROW_TILES):
            pltpu.make_async_copy(acc_ref.at[item % 2, s, :, pl.ds(c * LANES, LANES)],
                                  ys_ref.at[pl.ds(row0(item) + s * blk, blk), c, :], sem_out.at[s]).start()

    def wait_y(s):
        pltpu.make_async_copy(acc_ref.at[0, s], acc_ref.at[0, s], sem_out.at[s]).wait()

    def for_blocks(count, fn):
        for s in range(ITEM_BLOCKS):
            @pl.when(s < count)
            def _():
                fn(s)

    @pl.when(j == 0)
    def _():
        @pl.when(i == 0)
        def _():
            start_rows((f0_ref, f1_ref, f2_ref, f3_ref), nsub, 0)

        wait_rows(nsub, slot)

        def to_bf16(s):
            for c in range(ROW_TILES):
                xb_ref[s, :, c * LANES:(c + 1) * LANES] = xst_ref[slot, c, s * blk:(s + 1) * blk, :].astype(BF16)

        for_blocks(nsub, to_bf16)

    @pl.when((j == MOE_J - 1) & (i + 1 < n_items))
    def _():
        start_rows((n0_ref, n1_ref, n2_ref, n3_ref), ins_ref[jnp.minimum(i + 1, n_items - 1)], 1 - slot)

    for n in range(1, ITEM_BLOCKS + 1):
        @pl.when(nsub == n)
        def _():
            x = xb_ref[0:n].reshape(n * blk, D_MODEL)
            gate = _dot(x, wg_ref[...].astype(BF16))
            up = _dot(x, wu_ref[...].astype(BF16))
            hid = (_silu(gate) * up).astype(BF16)
            y = _dot(hid, wd_ref[...].astype(BF16)).reshape(n, blk, D_MODEL)

            @pl.when(j == 0)
            def _():
                acc_ref[slot, 0:n] = y

            @pl.when(j > 0)
            def _():
                acc_ref[slot, 0:n] = acc_ref[slot, 0:n] + y

    def wait_prev_y():
        @pl.when(i > 0)
        def _():
            for_blocks(ins_ref[jnp.maximum(i - 1, 0)], wait_y)

    @pl.when((nsub > 0) & (j == MOE_J - 1))
    def _():
        wait_prev_y()
        for_blocks(nsub, lambda s: start_y(i, s))

        @pl.when(i == n_items - 1)
        def _():
            for_blocks(nsub, wait_y)

    @pl.when((nsub == 0) & (j == 0))
    def _():
        wait_prev_y()

        @pl.when(nfill > 0)
        def _():
            acc_ref[slot] = jnp.zeros(acc_ref.shape[1:], F32)
            for_blocks(nfill, lambda s: start_y(i, s))
            for_blocks(nfill, wait_y)


def _experts(h_rows, row_tok, w_gate, w_up, w_down, item_e, item_off, item_nsub, item_fill, n_real):
    n_rows = row_tok.shape[0]
    n_blocks = n_rows // MOE_BLOCK
    d = D_MODEL

    def w_in_map(i, j, ie, io, ins, ifl, nr):
        return (ie[i], 0, jnp.where(i < nr[0], j, MOE_J - 1))

    def w_dn_map(i, j, ie, io, ins, ifl, nr):
        return (ie[i], jnp.where(i < nr[0], j, MOE_J - 1), 0)

    def tok_spec(s, next_item):
        def index_map(i, j, ie, io, ins, ifl, nr):
            item = jnp.minimum(i + 1, N_ITEMS - 1) if next_item else 0
            return (jnp.minimum(io[item] + s, n_blocks - 1), 0, 0)

        return pl.BlockSpec((1, 1, MOE_BLOCK), index_map, memory_space=pltpu.SMEM)

    grid_spec = pltpu.PrefetchScalarGridSpec(
        num_scalar_prefetch=5,
        grid=(N_ITEMS, MOE_J),
        in_specs=[pl.BlockSpec((None, d, MOE_FC), w_in_map),
                  pl.BlockSpec((None, d, MOE_FC), w_in_map),
                  pl.BlockSpec((None, MOE_FC, d), w_dn_map),
                  pl.BlockSpec(memory_space=pl.ANY)]
                 + [tok_spec(s, False) for s in range(ITEM_BLOCKS)]
                 + [tok_spec(s, True) for s in range(ITEM_BLOCKS)],
        out_specs=pl.BlockSpec(memory_space=pl.ANY),
        scratch_shapes=[pltpu.VMEM((2, ROW_TILES, ITEM_BLOCKS * MOE_BLOCK, LANES), F32),
                        pltpu.VMEM((ITEM_BLOCKS, MOE_BLOCK, d), BF16),
                        pltpu.VMEM((2, ITEM_BLOCKS, MOE_BLOCK, d), F32),
                        pltpu.SemaphoreType.DMA((2,)),
                        pltpu.SemaphoreType.DMA((ITEM_BLOCKS,))],
    )
    tok3 = row_tok.reshape(n_blocks, 1, MOE_BLOCK)
    return pl.pallas_call(
        _experts_body,
        grid_spec=grid_spec,
        out_shape=jax.ShapeDtypeStruct((n_rows, ROW_TILES, LANES), F32),
        compiler_params=_cparams(("arbitrary", "arbitrary")),
        name="experts",
    )(item_e, item_off, item_nsub, item_fill, n_real, w_gate, w_up, w_down, h_rows, *([tok3] * (2 * ITEM_BLOCKS)))


FINAL_TM = GATHER_ROWS // 2


def _final_body(idx_ref, idx_next_ref, ys_ref, w_ref, x1_ref, mod_ref, g_ref, o_ref, buf_ref, sem):
    slot = _gather_step(idx_ref, idx_next_ref, ys_ref, buf_ref, sem)
    tm = FINAL_TM
    w = w_ref[...]
    w0, w1 = w[:, 0:1], w[:, 1:2]
    ffn = jnp.concatenate([w0 * buf_ref[slot, c, 0:tm, :] + w1 * buf_ref[slot, c, tm:2 * tm, :]
                           for c in range(ROW_TILES)], axis=1)
    o_ref[...] = x1_ref[...] + mod_ref[5:6, :] * (_rms(ffn) * g_ref[...])


def _final(y_sorted, pos, wts, x1, mod8, g_post):
    t, d = x1.shape
    tm = FINAL_TM
    nt = t // tm
    idx3 = pos.reshape(nt, tm, 2).transpose(0, 2, 1).reshape(nt, 1, 2 * tm)
    return pl.pallas_call(
        _final_body,
        grid=(nt,),
        in_specs=_gather_specs(nt) + [pl.BlockSpec((tm, 128), lambda m: (m, 0)),
                                      pl.BlockSpec((tm, d), lambda m: (m, 0)),
                                      pl.BlockSpec((8, d), lambda m: (0, 0)),
                                      pl.BlockSpec((1, d), lambda m: (0, 0))],
        out_specs=pl.BlockSpec((tm, d), lambda m: (m, 0)),
        out_shape=jax.ShapeDtypeStruct((t, d), F32),
        scratch_shapes=GATHER_SCRATCH,
        compiler_params=_cparams(("arbitrary",)),
        name="final",
    )(idx3, idx3, y_sorted, wts, x1, mod8, g_post)


PLAN_TT = 512


def _plan_body(ids_ref, tril_ref, upper_ref, dest_ref, cnt_ref, run_ref):
    p = pl.program_id(0)
    i = pl.program_id(1)

    @pl.when((p == 0) & (i == 0))
    def _():
        run_ref[...] = jnp.zeros_like(run_ref)

    @pl.when((p == 1) & (i == 0))
    def _():
        counts = run_ref[...]
        cnt_ref[...] = counts.astype(I32)
        nblk = jnp.floor((counts + (MOE_BLOCK - 1.0)) * (1.0 / MOE_BLOCK))
        blk_start = _dot(nblk.astype(BF16), upper_ref[...])
        run_ref[...] = blk_start * MOE_BLOCK

    ids = ids_ref[...]
    lane = lax.broadcasted_iota(I32, ids.shape, 1)
    oh0 = lane == ids[:, 0:1]
    oh1 = lane == ids[:, 1:2]
    both = jnp.where(oh0 | oh1, 1.0, 0.0).astype(BF16)

    @pl.when(p == 1)
    def _():
        nxt = _dot(tril_ref[...], both) + run_ref[0:1, :]
        d0 = jnp.sum(jnp.where(oh0, nxt, 0.0), axis=-1, keepdims=True)
        d1 = jnp.sum(jnp.where(oh1, nxt, 0.0), axis=-1, keepdims=True)
        dest_ref[...] = jnp.where(lane == 0, d0, jnp.where(lane == 1, d1, 0.0)).astype(I32)

    run_ref[...] = run_ref[...] + _dot(jnp.ones((8, PLAN_TT), BF16), both)


def _plan(ids):
    t = ids.shape[0]
    r = np.arange(PLAN_TT)
    e = np.arange(128)
    strict_tril = jnp.asarray(r[:, None] > r[None, :], BF16)
    strict_upper = jnp.asarray(e[:, None] < e[None, :], BF16)
    return pl.pallas_call(
        _plan_body,
        grid=(2, t // PLAN_TT),
        in_specs=[pl.BlockSpec((PLAN_TT, 128), lambda p, i: (i, 0)),
                  pl.BlockSpec((PLAN_TT, PLAN_TT), lambda p, i: (0, 0)),
                  pl.BlockSpec((128, 128), lambda p, i: (0, 0))],
        out_specs=[pl.BlockSpec((PLAN_TT, 128), lambda p, i: (i * p, 0)),
                   pl.BlockSpec((8, 128), lambda p, i: (0, 0))],
        out_shape=[jax.ShapeDtypeStruct((t, 128), I32), jax.ShapeDtypeStruct((8, 128), I32)],
        scratch_shapes=[pltpu.VMEM((8, 128), F32)],
        compiler_params=_cparams(("arbitrary", "arbitrary")),
        name="plan",
    )(ids, strict_tril, strict_upper)


def _routing_tables(ids128, n_tok):
    n_assign = n_tok * 2
    n_blocks = n_assign // MOE_BLOCK + N_EXPERTS
    dest128, cnt = _plan(ids128)
    dest = dest128[:, :2].reshape(n_assign)
    counts = cnt[0, :N_EXPERTS]
    nb = (counts + MOE_BLOCK - 1) // MOE_BLOCK
    blk_start = jnp.cumsum(nb) - nb
    n_rows = n_blocks * MOE_BLOCK
    row_tok = (jnp.arange(n_rows, dtype=I32) % n_tok).at[dest].set(jnp.arange(n_assign, dtype=I32) // 2)
    pos = dest.reshape(n_tok, 2)

    n_it = (nb + ITEM_BLOCKS - 1) // ITEM_BLOCKS
    it_end = jnp.cumsum(n_it)
    it_start = it_end - n_it
    n_real = it_end[-1]
    i = jnp.arange(N_ITEMS, dtype=I32)
    e_i = jnp.minimum(jnp.searchsorted(it_end, i, side='right').astype(I32), N_EXPERTS - 1)
    k_i = i - it_start[e_i]
    valid = i < n_real
    last_e = e_i[jnp.maximum(n_real - 1, 0)]
    item_e = jnp.where(valid, e_i, last_e).astype(I32)
    fill_off = jnp.sum(nb) + ITEM_BLOCKS * (i - n_real)
    item_fill = jnp.where(valid, 0, jnp.clip(n_blocks - fill_off, 0, ITEM_BLOCKS)).astype(I32)
    item_off = jnp.where(valid, blk_start[e_i] + ITEM_BLOCKS * k_i, jnp.minimum(fill_off, n_blocks - 1)).astype(I32)
    item_nsub = jnp.where(valid, jnp.clip(nb[e_i] - ITEM_BLOCKS * k_i, 0, ITEM_BLOCKS), 0).astype(I32)
    return row_tok, pos, item_e, item_off, item_nsub, item_fill, n_real.reshape(1).astype(I32)


def _pad_lanes(v, start, total=SMALL_COLS):
    return jnp.zeros((1, total), F32).at[0, start:start + v.shape[0]].set(v)


def _layer(x2, c, w_ada, b_ada, norm_pre_mix, norm_post_mix, norm_pre_ffn, norm_post_ffn,
           w_in, gla_w_gate_up, gla_b_gate, gla_norm, ssm_conv_w, ssm_conv_b, ssm_dt_bias,
           ssm_a_log, ssm_d, ssm_norm, w_branch_gla, w_branch_ssm, w_out,
           router_group, router_expert, moe_w_gate, moe_w_up, moe_w_down):
    t, d = x2.shape
    row = lambda v: v.reshape(1, -1)

    mod = _ada(c, w_ada, b_ada)
    mod8 = jnp.concatenate([mod.reshape(6, d), jnp.zeros((2, d), F32)], axis=0)

    w_big, w_small = _repack(w_in.T)
    p, small = _inproj(x2, mod8, row(norm_pre_mix), w_big, w_small)

    o_gla = _gla(p, small, gla_w_gate_up, row(gla_b_gate), row(gla_norm))
    y_ssm = _ssd(p, small, ssm_conv_w, row(ssm_conv_b),
                 _pad_lanes(ssm_dt_bias, SMALL_DT0), _pad_lanes(ssm_a_log, SMALL_DT0),
                 _pad_lanes(ssm_d, SMALL_DT0), row(ssm_norm))
    merged = _merge(o_gla, y_ssm, w_branch_gla, w_branch_ssm, p)

    w_router = jnp.concatenate([router_group, router_expert,
                                jnp.zeros((d, 128 - MOE_GROUPS - N_EXPERTS), F32)], axis=1)
    x1, h2f, logits = _outproj(merged, w_out.astype(BF16), x2, mod8,
                               row(norm_post_mix), row(norm_pre_ffn), w_router)
    ids, wts = _route(logits)

    row_tok, pos, item_e, item_off, item_nsub, item_fill, n_real = _routing_tables(ids, t)
    y_sorted = _experts(h2f, row_tok, moe_w_gate, moe_w_up, moe_w_down,
                        item_e, item_off, item_nsub, item_fill, n_real)
    return _final(y_sorted, pos, wts, x1, mod8, row(norm_post_ffn))


def kernel(x, c, w_ada, b_ada, norm_pre_mix, norm_post_mix, norm_pre_ffn, norm_post_ffn, w_in, gla_w_gate_up, gla_b_gate, gla_norm, ssm_conv_w, ssm_conv_b, ssm_dt_bias, ssm_a_log, ssm_d, ssm_norm, w_branch_gla, w_branch_ssm, w_out, router_group, router_expert, moe_w_gate, moe_w_up, moe_w_down):
    bsz, seq, d = x.shape
    assert bsz == 1 and d == D_MODEL
    x2 = x.reshape(seq, d)
    params = (w_ada, b_ada, norm_pre_mix, norm_post_mix, norm_pre_ffn, norm_post_ffn, w_in, gla_w_gate_up,
              gla_b_gate, gla_norm, ssm_conv_w, ssm_conv_b, ssm_dt_bias, ssm_a_log, ssm_d, ssm_norm,
              w_branch_gla, w_branch_ssm, w_out, router_group, router_expert, moe_w_gate, moe_w_up, moe_w_down)
    for layer in range(w_ada.shape[0]):
        x2 = _layer(x2, c, *(prm[layer] for prm in params))
    return x2.reshape(bsz, seq, d)
```

```python
import jax
import jax.numpy as jnp
import numpy as np
from jax import lax
from jax.experimental import pallas as pl
from jax.experimental.pallas import tpu as pltpu

F32 = jnp.float32
BF16 = jnp.bfloat16
I32 = jnp.int32

D_MODEL = 2048
EPS = 1e-6
LANES = 128
ROW_TILES = D_MODEL // LANES

GLA_HEADS = 4
GLA_HEAD_K = 256
GLA_HEAD_V = 512
GLA_RANK = 16
GLA_NORMALIZER = 16.0
CHUNK = 64

SSM_GROUPS = 8
SSM_HEADS = 64
SSM_HEAD_DIM = 64
SSM_STATE = 128
SSM_CONV = 4
SSM_GROUP_W = 512
SSM_INNER = 4096

N_EXPERTS = 64
EXPERTS_PER_GROUP = 8
MOE_GROUPS = 8
MOE_FF = 1024
MOE_BLOCK = 128
ITEM_BLOCKS = 4

COL_Q, COL_K, COL_V, COL_OG, COL_Z, COL_XS, COL_B, COL_C, COL_GG, COL_GS = (
    0, 1024, 2048, 4096, 6144, 10240, 14336, 15360, 16384, 18432)
P_COLS = 20480
SMALL_COLS = 128
SMALL_DT0 = GLA_RANK

VMEM_LIMIT = 56 * 1024 * 1024


def _cparams(sem, vmem=VMEM_LIMIT):
    return pltpu.CompilerParams(dimension_semantics=sem, vmem_limit_bytes=vmem)


def _dot(a, b):
    return jnp.dot(a, b, preferred_element_type=F32)


def _dot_nt(a, b):
    return lax.dot_general(a, b, (((1,), (1,)), ((), ())), preferred_element_type=F32)


def _dot_tn(a, b):
    return lax.dot_general(a, b, (((0,), (0,)), ((), ())), preferred_element_type=F32)


def _split3(a):
    hi = a.astype(BF16)
    r1 = a - hi.astype(F32)
    mid = r1.astype(BF16)
    lo = (r1 - mid.astype(F32)).astype(BF16)
    return hi, mid, lo


def _dot_sel_r(a, sel):
    hi, mid, lo = _split3(a)
    return _dot(hi, sel) + _dot(mid, sel) + _dot(lo, sel)


def _dot_sel_l(sel, a):
    hi, mid, lo = _split3(a)
    return _dot(sel, hi) + _dot(sel, mid) + _dot(sel, lo)


def _sigmoid(x):
    return 1.0 / (1.0 + jnp.exp(-x))


def _silu(x):
    return x * _sigmoid(x)


def _softplus(x):
    return jnp.maximum(x, 0.0) + jnp.log1p(jnp.exp(-jnp.abs(x)))


def _log_sigmoid(x):
    return jnp.minimum(x, 0.0) - jnp.log1p(jnp.exp(-jnp.abs(x)))


def _rms(x):
    return x * lax.rsqrt(jnp.mean(x * x, axis=-1, keepdims=True) + EPS)


def _ada_body(c_ref, w_ref, b_ref, o_ref):
    c = c_ref[...]
    s = jnp.broadcast_to(_silu(c), (8, c.shape[1])).astype(BF16)
    o_ref[...] = _dot(s, w_ref[...].astype(BF16))[0:1] + b_ref[...]


def _ada(c, w_ada, b_ada):
    d, n = w_ada.shape
    tn = 1024
    return pl.pallas_call(
        _ada_body,
        grid=(n // tn,),
        in_specs=[pl.BlockSpec((1, d), lambda j: (0, 0)),
                  pl.BlockSpec((d, tn), lambda j: (0, j)),
                  pl.BlockSpec((1, tn), lambda j: (0, j))],
        out_specs=pl.BlockSpec((1, tn), lambda j: (0, j)),
        out_shape=jax.ShapeDtypeStruct((1, n), F32),
        compiler_params=_cparams(("parallel",)),
        name="ada",
    )(c, w_ada, b_ada.reshape(1, n))


W_GLR, W_MID, W_DT, W_GATES, W_END = 4096, 4112, 16400, 16464, 20560
REPACK_ROWS = 1024
REPACK_TAIL = 128


def _repack_body(a_ref, b_ref, big_ref, small_ref):
    j = pl.program_id(0)
    n = REPACK_ROWS

    def emit(shift):
        if shift == 0:
            big_ref[...] = a_ref[...].astype(BF16)
        else:
            big_ref[0:n - shift, :] = a_ref[shift:n, :].astype(BF16)
            big_ref[n - shift:n, :] = b_ref[0:shift, :].astype(BF16)

    first_mid, first_gates = COL_OG // n, COL_GG // n

    @pl.when(j < first_mid)
    def _():
        emit(0)

    @pl.when((j >= first_mid) & (j < first_gates))
    def _():
        emit(W_MID - COL_OG)

    @pl.when(j >= first_gates)
    def _():
        emit(W_GATES - COL_GG)

    @pl.when(j == 0)
    def _():
        small_ref[...] = jnp.zeros_like(small_ref)

    @pl.when(j == W_GLR // n)
    def _():
        small_ref[0:GLA_RANK, :] = a_ref[0:GLA_RANK, :].astype(BF16)

    @pl.when(j == W_DT // n)
    def _():
        small_ref[SMALL_DT0:SMALL_DT0 + SSM_HEADS, :] = a_ref[SMALL_DT0:SMALL_DT0 + SSM_HEADS, :].astype(BF16)


def _repack(w_t):
    n_in, d = w_t.shape
    n = REPACK_ROWS
    assert n_in == W_END and W_GLR % n == 0 and W_GLR + GLA_RANK == W_MID and W_DT % n == SMALL_DT0
    assert COL_OG % n == 0 and COL_GG % n == 0 and W_GATES - COL_GG <= REPACK_TAIL
    return pl.pallas_call(
        _repack_body,
        grid=(P_COLS // n,),
        in_specs=[pl.BlockSpec((n, d), lambda j: (j, 0)),
                  pl.BlockSpec((REPACK_TAIL, d), lambda j: ((j + 1) * (n // REPACK_TAIL), 0))],
        out_specs=[pl.BlockSpec((n, d), lambda j: (j, 0)),
                   pl.BlockSpec((SMALL_COLS, d), lambda j: (0, 0))],
        out_shape=[jax.ShapeDtypeStruct((P_COLS, d), BF16), jax.ShapeDtypeStruct((SMALL_COLS, d), BF16)],
        compiler_params=_cparams(("arbitrary",)),
        name="repack",
    )(w_t, w_t)


def _inproj_body(x_ref, mod_ref, g_ref, w_ref, ws_ref, p_ref, s_ref, h_ref):
    @pl.when(pl.program_id(1) == 0)
    def _():
        h = _rms(x_ref[...]) * g_ref[...] * (1.0 + mod_ref[1:2, :]) + mod_ref[0:1, :]
        hb = h.astype(BF16)
        h_ref[...] = hb
        s_ref[...] = _dot_nt(hb, ws_ref[...])

    p_ref[...] = _dot_nt(h_ref[...], w_ref[...]).astype(BF16)


def _inproj(x2, mod8, gain, w_big, w_small):
    t, d = x2.shape
    n = w_big.shape[0]
    tm, tn = 1024, 2048
    return pl.pallas_call(
        _inproj_body,
        grid=(t // tm, n // tn),
        in_specs=[pl.BlockSpec((tm, d), lambda m, j: (m, 0)),
                  pl.BlockSpec((8, d), lambda m, j: (0, 0)),
                  pl.BlockSpec((1, d), lambda m, j: (0, 0)),
                  pl.BlockSpec((tn, d), lambda m, j: (j, 0)),
                  pl.BlockSpec((SMALL_COLS, d), lambda m, j: (0, 0))],
        out_specs=[pl.BlockSpec((tm, tn), lambda m, j: (m, j)),
                   pl.BlockSpec((tm, SMALL_COLS), lambda m, j: (m, 0))],
        out_shape=[jax.ShapeDtypeStruct((t, n), BF16),
                   jax.ShapeDtypeStruct((t, SMALL_COLS), F32)],
        scratch_shapes=[pltpu.VMEM((tm, d), BF16)],
        compiler_params=_cparams(("parallel", "arbitrary")),
        name="inproj",
    )(x2, mod8, gain, w_big, w_small)


GLA_TB = 512
GLA_SCORE_ROWS = 256


def _gla_body(q_ref, k_ref, v_ref, og_ref, sm_ref, wup_ref, bg_ref, gn_ref, o_ref, st_ref):
    @pl.when(pl.program_id(1) == 0)
    def _():
        st_ref[...] = jnp.zeros_like(st_ref)

    tb, C = GLA_TB, CHUNK
    nch = tb // C
    sb = GLA_SCORE_ROWS
    r = lax.broadcasted_iota(I32, (sb, sb), 0)
    c = lax.broadcasted_iota(I32, (sb, sb), 1)
    causal = (r // C == c // C) & (r >= c)
    r2 = lax.broadcasted_iota(I32, (2 * C, 2 * C), 0)
    c2 = lax.broadcasted_iota(I32, (2 * C, 2 * C), 1)
    tril2 = ((r2 // C == c2 // C) & (r2 >= c2)).astype(BF16)

    glr = sm_ref[:, 0:GLA_RANK].astype(BF16)
    pre = _dot(glr, wup_ref[...].astype(BF16)) + bg_ref[...]
    log_a = _log_sigmoid(pre) / GLA_NORMALIZER
    b = jnp.concatenate([_dot_sel_l(tril2, log_a[i * 2 * C:(i + 1) * 2 * C]) for i in range(nch // 2)],
                        axis=0)
    b_last = [b[(i + 1) * C - 1:(i + 1) * C, :] for i in range(nch)]
    b_end = jnp.concatenate([jnp.broadcast_to(bl, (C, bl.shape[1])) for bl in b_last], axis=0)

    q = q_ref[...].astype(F32) * (GLA_HEAD_K ** -0.5)
    k = k_ref[...].astype(F32)
    v = v_ref[...]
    q_dec = (q * jnp.exp(b)).astype(BF16)
    k_inv = (k * jnp.exp(-b)).astype(BF16)
    k_end = (k * jnp.exp(b_end - b)).astype(BF16)
    o_intra = []
    for i in range(tb // sb):
        rows = slice(i * sb, (i + 1) * sb)
        scores = jnp.where(causal, _dot_nt(q_dec[rows], k_inv[rows]), 0.0)
        o_intra.append(_dot(scores.astype(BF16), v[rows]))
    o_intra = jnp.concatenate(o_intra, axis=0)

    st = st_ref[...]
    o_inter = []
    for i in range(nch):
        rows = slice(i * C, (i + 1) * C)
        o_inter.append(_dot_nt(q_dec[rows], st.astype(BF16)))
        st = st * jnp.exp(b_last[i]) + _dot_tn(v[rows], k_end[rows])
    st_ref[...] = st

    o = o_intra + jnp.concatenate(o_inter, axis=0)
    og = og_ref[...].astype(F32)
    o_ref[...] = (_rms(o) * gn_ref[...] * _silu(og)).astype(BF16)


def _gla(p, small, wup, bg, gn):
    t = p.shape[0]
    tb = GLA_TB
    kb, vb = GLA_HEAD_K, GLA_HEAD_V
    return pl.pallas_call(
        _gla_body,
        grid=(GLA_HEADS, t // tb),
        in_specs=[pl.BlockSpec((tb, kb), lambda h, i: (i, COL_Q // kb + h)),
                  pl.BlockSpec((tb, kb), lambda h, i: (i, COL_K // kb + h)),
                  pl.BlockSpec((tb, vb), lambda h, i: (i, COL_V // vb + h)),
                  pl.BlockSpec((tb, vb), lambda h, i: (i, COL_OG // vb + h)),
                  pl.BlockSpec((tb, SMALL_COLS), lambda h, i: (i, 0)),
                  pl.BlockSpec((GLA_RANK, kb), lambda h, i: (0, h)),
                  pl.BlockSpec((1, kb), lambda h, i: (0, h)),
                  pl.BlockSpec((1, vb), lambda h, i: (0, h))],
        out_specs=pl.BlockSpec((tb, vb), lambda h, i: (i, h)),
        out_shape=jax.ShapeDtypeStruct((t, GLA_HEADS * vb), BF16),
        scratch_shapes=[pltpu.VMEM((vb, kb), F32)],
        compiler_params=_cparams(("parallel", "arbitrary")),
        name="gla",
    )(p, p, p, p, small, wup, bg, gn)


SSD_TB = 512
HALO = 8


def _ssd_body(xs_ref, b_ref, c_ref, z_ref, sm_ref, cwx_ref, cwb_ref, cwc_ref, cbx_ref, cbb_ref, cbc_ref,
              dtb_ref, alog_ref, dsk_ref, ng_ref, o_ref,
              ex_ref, eb_ref, ec_ref, at_ref, ht_ref):
    g = pl.program_id(0)
    t = pl.program_id(1)
    tb = SSD_TB
    gw = SSM_GROUP_W

    @pl.when(t == 0)
    def _():
        ht_ref[...] = jnp.zeros_like(ht_ref)
        ex_ref[0:HALO, :] = jnp.zeros((HALO, gw), F32)
        eb_ref[0:HALO, :] = jnp.zeros((HALO, SSM_STATE), F32)
        ec_ref[0:HALO, :] = jnp.zeros((HALO, SSM_STATE), F32)

    @pl.when(t > 0)
    def _():
        ex_ref[0:HALO, :] = ex_ref[tb:tb + HALO, :]
        eb_ref[0:HALO, :] = eb_ref[tb:tb + HALO, :]
        ec_ref[0:HALO, :] = ec_ref[tb:tb + HALO, :]

    def conv_silu(u_ref, e_ref, w_ref, bias_ref):
        e_ref[HALO:HALO + tb, :] = u_ref[...].astype(F32)
        acc = bias_ref[...] + w_ref[0:1, :] * e_ref[HALO - 3:HALO - 3 + tb, :]
        for kk in range(1, SSM_CONV):
            acc = acc + w_ref[kk:kk + 1, :] * e_ref[HALO - 3 + kk:HALO - 3 + kk + tb, :]
        return _silu(acc)

    xa = conv_silu(xs_ref, ex_ref, cwx_ref, cbx_ref)
    ba = conv_silu(b_ref, eb_ref, cwb_ref, cbb_ref).astype(BF16)
    ca = conv_silu(c_ref, ec_ref, cwc_ref, cbc_ref).astype(BF16)

    L = CHUNK
    nch = tb // L
    rep = gw // L
    hpg = gw // SSM_HEAD_DIM
    head0 = pl.multiple_of(SMALL_DT0 + g * hpg, hpg)

    e_row = lax.broadcasted_iota(I32, (SMALL_COLS, gw), 0)
    e_col = lax.broadcasted_iota(I32, (SMALL_COLS, gw), 1)
    expand = (e_row == head0 + e_col // SSM_HEAD_DIM).astype(BF16)

    dt_small = _softplus(sm_ref[...] + dtb_ref[...])
    adt_small = dt_small * (-jnp.exp(alog_ref[...]))
    rb = lax.broadcasted_iota(I32, (2 * L, 2 * L), 0)
    cb = lax.broadcasted_iota(I32, (2 * L, 2 * L), 1)
    tril2 = ((rb // L == cb // L) & (rb >= cb)).astype(BF16)
    acum_small = jnp.concatenate([_dot_sel_l(tril2, adt_small[k * 2 * L:(k + 1) * 2 * L])
                                  for k in range(nch // 2)], axis=0)
    at_ref[...] = acum_small.T
    heads = at_ref[pl.ds(head0, hpg), :]
    dt_exp = _dot_sel_r(dt_small, expand)
    acum = _dot_sel_r(acum_small, expand)
    d_exp = _dot_sel_r(jnp.broadcast_to(dsk_ref[...], (8, SMALL_COLS)), expand)[0:1]
    xdt = xa * dt_exp
    xdt_b = xdt.astype(BF16)
    e_acum = jnp.exp(acum)

    li = lax.broadcasted_iota(I32, (L, gw), 0)
    lj = lax.broadcasted_iota(I32, (L, gw), 1) % L
    causal_t = li >= lj
    hi = lax.broadcasted_iota(I32, (hpg, gw), 0)
    hj = lax.broadcasted_iota(I32, (hpg, gw), 1) // L
    headmask = hi == hj
    ones_h = jnp.ones((L, hpg), BF16)
    bi = lax.broadcasted_iota(I32, (gw, gw), 0) // L
    bj = lax.broadcasted_iota(I32, (gw, gw), 1) // SSM_HEAD_DIM
    blockmask = bi == bj
    masked_out = -1e30

    ht = ht_ref[...]
    for c in range(nch):
        rows = slice(c * L, (c + 1) * L)
        acum_c = acum[rows]
        a_rows = jnp.concatenate([heads[:, c * L:(c + 1) * L]] * rep, axis=1)
        rterm = _dot_sel_l(ones_h, jnp.where(headmask, a_rows, 0.0))
        decay = jnp.exp(jnp.where(causal_t, acum_c - rterm, masked_out))
        cc = ca[rows]
        bc = ba[rows]
        cb_t = _dot_nt(cc, jnp.concatenate([bc] * rep, axis=0))
        m = (cb_t * decay).astype(BF16)
        bd = jnp.where(blockmask, jnp.concatenate([xdt_b[rows]] * rep, axis=0), jnp.zeros((), BF16))
        y_diag = _dot(m, bd)
        y_off = _dot(cc, ht.astype(BF16)) * e_acum[rows]
        a_last = acum_c[L - 1:L, :]
        xd = (xdt[rows] * jnp.exp(a_last - acum_c)).astype(BF16)
        ht = ht * jnp.exp(a_last) + _dot_tn(bc, xd)
        y = y_diag + y_off + d_exp * xa[rows]
        y = y * _silu(z_ref[rows, :].astype(F32))
        o_ref[rows, :] = (_rms(y) * ng_ref[...]).astype(BF16)
    ht_ref[...] = ht


def _ssd(p, small, conv_w, conv_b, dt_bias_s, a_log_s, d_skip_s, norm_g):
    t = p.shape[0]
    tb = SSD_TB
    gw, ns = SSM_GROUP_W, SSM_STATE
    xs0, b0, c0 = 0, SSM_INNER // ns, (SSM_INNER + SSM_GROUPS * ns) // ns
    row = lambda w, off: pl.BlockSpec((1, w), lambda g, i: (0, off + g))
    return pl.pallas_call(
        _ssd_body,
        grid=(SSM_GROUPS, t // tb),
        in_specs=[pl.BlockSpec((tb, gw), lambda g, i: (i, COL_XS // gw + g)),
                  pl.BlockSpec((tb, ns), lambda g, i: (i, COL_B // ns + g)),
                  pl.BlockSpec((tb, ns), lambda g, i: (i, COL_C // ns + g)),
                  pl.BlockSpec((tb, gw), lambda g, i: (i, COL_Z // gw + g)),
                  pl.BlockSpec((tb, SMALL_COLS), lambda g, i: (i, 0)),
                  pl.BlockSpec((SSM_CONV, gw), lambda g, i: (0, xs0 + g)),
                  pl.BlockSpec((SSM_CONV, ns), lambda g, i: (0, b0 + g)),
                  pl.BlockSpec((SSM_CONV, ns), lambda g, i: (0, c0 + g)),
                  row(gw, xs0), row(ns, b0), row(ns, c0),
                  pl.BlockSpec((1, SMALL_COLS), lambda g, i: (0, 0)),
                  pl.BlockSpec((1, SMALL_COLS), lambda g, i: (0, 0)),
                  pl.BlockSpec((1, SMALL_COLS), lambda g, i: (0, 0)),
                  pl.BlockSpec((1, gw), lambda g, i: (0, g))],
        out_specs=pl.BlockSpec((tb, gw), lambda g, i: (i, g)),
        out_shape=jax.ShapeDtypeStruct((t, SSM_INNER), BF16),
        scratch_shapes=[pltpu.VMEM((tb + HALO, gw), F32),
                        pltpu.VMEM((tb + HALO, ns), F32),
                        pltpu.VMEM((tb + HALO, ns), F32),
                        pltpu.VMEM((SMALL_COLS, tb), F32),
                        pltpu.VMEM((ns, gw), F32)],
        compiler_params=_cparams(("parallel", "arbitrary")),
        name="ssd",
    )(p, p, p, p, small, conv_w, conv_w, conv_w, conv_b, conv_b, conv_b,
      dt_bias_s, a_log_s, d_skip_s, norm_g)


def _merge_body(a1_ref, a2_ref, w1_ref, w2_ref, gg_ref, gs_ref, o_ref, w1b_ref, w2b_ref):
    @pl.when(pl.program_id(1) == 0)
    def _():
        w1b_ref[...] = w1_ref[...].astype(BF16)
        w2b_ref[...] = w2_ref[...].astype(BF16)

    y1 = _dot(a1_ref[...], w1b_ref[...])
    y2 = _dot(a2_ref[...], w2b_ref[...])
    o_ref[...] = (_sigmoid(gg_ref[...].astype(F32)) * y1 + _sigmoid(gs_ref[...].astype(F32)) * y2).astype(BF16)


def _merge(o_gla, y_ssm, w1, w2, p):
    t, k1 = o_gla.shape
    k2 = y_ssm.shape[1]
    n = w1.shape[1]
    tm, tn = 512, 512
    return pl.pallas_call(
        _merge_body,
        grid=(n // tn, t // tm),
        in_specs=[pl.BlockSpec((tm, k1), lambda j, m: (m, 0)),
                  pl.BlockSpec((tm, k2), lambda j, m: (m, 0)),
                  pl.BlockSpec((k1, tn), lambda j, m: (0, j)),
                  pl.BlockSpec((k2, tn), lambda j, m: (0, j)),
                  pl.BlockSpec((tm, tn), lambda j, m: (m, COL_GG // tn + j)),
                  pl.BlockSpec((tm, tn), lambda j, m: (m, COL_GS // tn + j))],
        out_specs=pl.BlockSpec((tm, tn), lambda j, m: (m, j)),
        out_shape=jax.ShapeDtypeStruct((t, n), BF16),
        scratch_shapes=[pltpu.VMEM((k1, tn), BF16), pltpu.VMEM((k2, tn), BF16)],
        compiler_params=_cparams(("parallel", "arbitrary")),
        name="merge",
    )(o_gla, y_ssm, w1, w2, p, p)


def _outproj_body(m_ref, w_ref, x_ref, mod_ref, gpost_ref, gpre_ref, wr_ref, x1_ref, hf_ref, lg_ref):
    mix = _dot(m_ref[...], w_ref[...])
    x1 = x_ref[...] + mod_ref[2:3, :] * (_rms(mix) * gpost_ref[...])
    x1_ref[...] = x1
    h = _rms(x1) * gpre_ref[...] * (1.0 + mod_ref[4:5, :]) + mod_ref[3:4, :]
    for s in range(ROW_TILES):
        hf_ref[:, s, :] = h[:, s * LANES:(s + 1) * LANES]
    h_hi = h.astype(BF16)
    h_lo = (h - h_hi.astype(F32)).astype(BF16)
    wr = wr_ref[...]
    w_hi = wr.astype(BF16)
    w_lo = (wr - w_hi.astype(F32)).astype(BF16)
    lg_ref[...] = _dot(h_hi, w_hi) + _dot(h_hi, w_lo) + _dot(h_lo, w_hi)


def _outproj(merged, w_out, x2, mod8, g_post, g_pre, w_router):
    t, d = x2.shape
    tm = 256
    full = lambda r, c: pl.BlockSpec((r, c), lambda m: (0, 0))
    tile = lambda c: pl.BlockSpec((tm, c), lambda m: (m, 0))
    return pl.pallas_call(
        _outproj_body,
        grid=(t // tm,),
        in_specs=[tile(d), full(d, d), tile(d), full(8, d), full(1, d), full(1, d), full(d, 128)],
        out_specs=[tile(d), pl.BlockSpec((tm, ROW_TILES, LANES), lambda m: (m, 0, 0)), tile(128)],
        out_shape=[jax.ShapeDtypeStruct((t, d), F32),
                   jax.ShapeDtypeStruct((t, ROW_TILES, LANES), F32),
                   jax.ShapeDtypeStruct((t, 128), F32)],
        compiler_params=_cparams(("parallel",)),
        name="outproj",
    )(merged, w_out, x2, mod8, g_post, g_pre, w_router)


def _route_body(lg_ref, id_ref, w_ref):
    lg = lg_ref[...]
    lane = lax.broadcasted_iota(I32, lg.shape, 1)
    lane_f = lane.astype(F32)
    neg = jnp.float32(-jnp.inf)

    def first_argmax(vals, mx):
        return jnp.min(jnp.where(vals == mx, lane_f, 1e9), axis=-1, keepdims=True).astype(I32)

    gmask = lane < MOE_GROUPS
    gl = jnp.where(gmask, lg, neg)
    gmax = jnp.max(gl, axis=-1, keepdims=True)
    gsum = jnp.sum(jnp.where(gmask, jnp.exp(gl - gmax), 0.0), axis=-1, keepdims=True)
    g_w = 1.0 / gsum
    g_idx = first_argmax(gl, gmax)
    lo = MOE_GROUPS + g_idx * EXPERTS_PER_GROUP
    emask = (lane >= lo) & (lane < lo + EXPERTS_PER_GROUP)
    el = jnp.where(emask, lg, neg)
    m1 = jnp.max(el, axis=-1, keepdims=True)
    i1 = first_argmax(el, m1)
    el2 = jnp.where(lane == i1, neg, el)
    m2 = jnp.max(el2, axis=-1, keepdims=True)
    i2 = first_argmax(el2, m2)
    r = jnp.exp(m2 - m1)
    w1 = g_w / (1.0 + r)
    w2 = g_w * r / (1.0 + r)
    id_ref[...] = jnp.where(lane == 0, i1 - MOE_GROUPS, jnp.where(lane == 1, i2 - MOE_GROUPS, 0))
    w_ref[...] = jnp.where(lane == 0, w1, jnp.where(lane == 1, w2, 0.0))


def _route(logits):
    t = logits.shape[0]
    tm = 1024
    spec = pl.BlockSpec((tm, 128), lambda m: (m, 0))
    return pl.pallas_call(
        _route_body,
        grid=(t // tm,),
        in_specs=[spec],
        out_specs=[spec, spec],
        out_shape=[jax.ShapeDtypeStruct((t, 128), I32), jax.ShapeDtypeStruct((t, 128), F32)],
        compiler_params=_cparams(("parallel",)),
        name="route",
    )(logits)


GATHER_ROWS = 512


def _issue_rows(idx_ref, src_ref, buf_ref, slot, sem):
    def body(r, carry):
        pltpu.make_async_copy(src_ref.at[idx_ref[0, 0, r]], buf_ref.at[slot, :, r, :], sem.at[slot]).start()
        return carry

    lax.fori_loop(0, GATHER_ROWS, body, 0, unroll=8)


def _wait_rows(buf_ref, slot, sem):
    pltpu.make_async_copy(buf_ref.at[slot], buf_ref.at[slot], sem.at[slot]).wait()


def _gather_step(idx_ref, idx_next_ref, src_ref, buf_ref, sem):
    i = pl.program_id(0)
    slot = i % 2

    @pl.when(i == 0)
    def _():
        _issue_rows(idx_ref, src_ref, buf_ref, 0, sem)

    @pl.when(i + 1 < pl.num_programs(0))
    def _():
        _issue_rows(idx_next_ref, src_ref, buf_ref, 1 - slot, sem)

    _wait_rows(buf_ref, slot, sem)
    return slot


def _gather_specs(nsteps):
    smem = lambda f: pl.BlockSpec((1, 1, GATHER_ROWS), f, memory_space=pltpu.SMEM)
    return [smem(lambda i: (i, 0, 0)),
            smem(lambda i: (jnp.minimum(i + 1, nsteps - 1), 0, 0)),
            pl.BlockSpec(memory_space=pl.ANY)]


GATHER_SCRATCH = [pltpu.VMEM((2, ROW_TILES, GATHER_ROWS, LANES), F32), pltpu.SemaphoreType.DMA((2,))]


MOE_FC = 512
MOE_J = MOE_FF // MOE_FC
N_ITEMS = (16384 // MOE_BLOCK + N_EXPERTS) // ITEM_BLOCKS + (N_EXPERTS * (ITEM_BLOCKS - 1)) // ITEM_BLOCKS


def _experts_body(ie_ref, io_ref, ins_ref, ifl_ref, nr_ref, wg_ref, wu_ref, wd_ref, h_ref,
                  f0_ref, f1_ref, f2_ref, f3_ref, n0_ref, n1_ref, n2_ref, n3_ref, ys_ref,
                  xst_ref, xb_ref, acc_ref, sem_in, sem_out):
    i = pl.program_id(0)
    j = pl.program_id(1)
    n_items = pl.num_programs(0)
    nsub = ins_ref[i]
    nfill = ifl_ref[i]
    slot = i % 2
    blk = MOE_BLOCK

    def row0(item):
        return pl.multiple_of(io_ref[item] * blk, blk)

    def start_rows(idx_refs, count, sl):
        def block(s):
            def body(r, carry):
                pltpu.make_async_copy(h_ref.at[idx_refs[s][0, 0, r]], xst_ref.at[sl, :, s * blk + r, :],
                                      sem_in.at[sl]).start()
                return carry

            lax.fori_loop(0, blk, body, 0, unroll=8)

        for_blocks(count, block)

    def wait_rows(count, sl):
        @pl.when(count > 0)
        def _():
            part = xst_ref.at[sl, :, pl.ds(0, count * blk), :]
            pltpu.make_async_copy(part, part, sem_in.at[sl]).wait()

    def start_y(item, s):
        for c in range(ROW_TILES):
            pltpu.make_async_copy(acc_ref.at[item % 2, s, :, pl.ds(c * LANES, LANES)],
                                  ys_ref.at[pl.ds(row0(item) + s * blk, blk), c, :], sem_out.at[s]).start()

    def wait_y(s):
        pltpu.make_async_copy(acc_ref.at[0, s], acc_ref.at[0, s], sem_out.at[s]).wait()

    def for_blocks(count, fn):
        for s in range(ITEM_BLOCKS):
            @pl.when(s < count)
            def _():
                fn(s)

    @pl.when(j == 0)
    def _():
        @pl.when(i == 0)
        def _():
            start_rows((f0_ref, f1_ref, f2_ref, f3_ref), nsub, 0)

        wait_rows(nsub, slot)

        def to_bf16(s):
            for c in range(ROW_TILES):
                xb_ref[s, :, c * LANES:(c + 1) * LANES] = xst_ref[slot, c, s * blk:(s + 1) * blk, :].astype(BF16)

        for_blocks(nsub, to_bf16)

    @pl.when((j == MOE_J - 1) & (i + 1 < n_items))
    def _():
        start_rows((n0_ref, n1_ref, n2_ref, n3_ref), ins_ref[jnp.minimum(i + 1, n_items - 1)], 1 - slot)

    for n in range(1, ITEM_BLOCKS + 1):
        @pl.when(nsub == n)
        def _():
            x = xb_ref[0:n].reshape(n * blk, D_MODEL)
            gate = _dot(x, wg_ref[...].astype(BF16))
            up = _dot(x, wu_ref[...].astype(BF16))
            hid = (_silu(gate) * up).astype(BF16)
            y = _dot(hid, wd_ref[...].astype(BF16)).reshape(n, blk, D_MODEL)

            @pl.when(j == 0)
            def _():
                acc_ref[slot, 0:n] = y

            @pl.when(j > 0)
            def _():
                acc_ref[slot, 0:n] = acc_ref[slot, 0:n] + y

    def wait_prev_y():
        @pl.when(i > 0)
        def _():
            for_blocks(ins_ref[jnp.maximum(i - 1, 0)], wait_y)

    @pl.when((nsub > 0) & (j == MOE_J - 1))
    def _():
        wait_prev_y()
        for_blocks(nsub, lambda s: start_y(i, s))

        @pl.when(i == n_items - 1)
        def _():
            for_blocks(nsub, wait_y)

    @pl.when((nsub == 0) & (j == 0))
    def _():
        wait_prev_y()

        @pl.when(nfill > 0)
        def _():
            acc_ref[slot] = jnp.zeros(acc_ref.shape[1:], F32)
            for_blocks(nfill, lambda s: start_y(i, s))
            for_blocks(nfill, wait_y)


def _experts(h_rows, row_tok, w_gate, w_up, w_down, item_e, item_off, item_nsub, item_fill, n_real):
    n_rows = row_tok.shape[0]
    n_blocks = n_rows // MOE_BLOCK
    d = D_MODEL

    def w_in_map(i, j, ie, io, ins, ifl, nr):
        return (ie[i], 0, jnp.where(i < nr[0], j, MOE_J - 1))

    def w_dn_map(i, j, ie, io, ins, ifl, nr):
        return (ie[i], jnp.where(i < nr[0], j, MOE_J - 1), 0)

    def tok_spec(s, next_item):
        def index_map(i, j, ie, io, ins, ifl, nr):
            item = jnp.minimum(i + 1, N_ITEMS - 1) if next_item else 0
            return (jnp.minimum(io[item] + s, n_blocks - 1), 0, 0)

        return pl.BlockSpec((1, 1, MOE_BLOCK), index_map, memory_space=pltpu.SMEM)

    grid_spec = pltpu.PrefetchScalarGridSpec(
        num_scalar_prefetch=5,
        grid=(N_ITEMS, MOE_J),
        in_specs=[pl.BlockSpec((None, d, MOE_FC), w_in_map),
                  pl.BlockSpec((None, d, MOE_FC), w_in_map),
                  pl.BlockSpec((None, MOE_FC, d), w_dn_map),
                  pl.BlockSpec(memory_space=pl.ANY)]
                 + [tok_spec(s, False) for s in range(ITEM_BLOCKS)]
                 + [tok_spec(s, True) for s in range(ITEM_BLOCKS)],
        out_specs=pl.BlockSpec(memory_space=pl.ANY),
        scratch_shapes=[pltpu.VMEM((2, ROW_TILES, ITEM_BLOCKS * MOE_BLOCK, LANES), F32),
                        pltpu.VMEM((ITEM_BLOCKS, MOE_BLOCK, d), BF16),
                        pltpu.VMEM((2, ITEM_BLOCKS, MOE_BLOCK, d), F32),
                        pltpu.SemaphoreType.DMA((2,)),
                        pltpu.SemaphoreType.DMA((ITEM_BLOCKS,))],
    )
    tok3 = row_tok.reshape(n_blocks, 1, MOE_BLOCK)
    return pl.pallas_call(
        _experts_body,
        grid_spec=grid_spec,
        out_shape=jax.ShapeDtypeStruct((n_rows, ROW_TILES, LANES), F32),
        compiler_params=_cparams(("arbitrary", "arbitrary")),
        name="experts",
    )(item_e, item_off, item_nsub, item_fill, n_real, w_gate, w_up, w_down, h_rows, *([tok3] * (2 * ITEM_BLOCKS)))


FINAL_TM = GATHER_ROWS // 2


def _final_body(idx_ref, idx_next_ref, ys_ref, w_ref, x1_ref, mod_ref, g_ref, o_ref, buf_ref, sem):
    slot = _gather_step(idx_ref, idx_next_ref, ys_ref, buf_ref, sem)
    tm = FINAL_TM
    w = w_ref[...]
    w0, w1 = w[:, 0:1], w[:, 1:2]
    ffn = jnp.concatenate([w0 * buf_ref[slot, c, 0:tm, :] + w1 * buf_ref[slot, c, tm:2 * tm, :]
                           for c in range(ROW_TILES)], axis=1)
    o_ref[...] = x1_ref[...] + mod_ref[5:6, :] * (_rms(ffn) * g_ref[...])


def _final(y_sorted, pos, wts, x1, mod8, g_post):
    t, d = x1.shape
    tm = FINAL_TM
    nt = t // tm
    idx3 = pos.reshape(nt, tm, 2).transpose(0, 2, 1).reshape(nt, 1, 2 * tm)
    return pl.pallas_call(
        _final_body,
        grid=(nt,),
        in_specs=_gather_specs(nt) + [pl.BlockSpec((tm, 128), lambda m: (m, 0)),
                                      pl.BlockSpec((tm, d), lambda m: (m, 0)),
                                      pl.BlockSpec((8, d), lambda m: (0, 0)),
                                      pl.BlockSpec((1, d), lambda m: (0, 0))],
        out_specs=pl.BlockSpec((tm, d), lambda m: (m, 0)),
        out_shape=jax.ShapeDtypeStruct((t, d), F32),
        scratch_shapes=GATHER_SCRATCH,
        compiler_params=_cparams(("arbitrary",)),
        name="final",
    )(idx3, idx3, y_sorted, wts, x1, mod8, g_post)


PLAN_TT = 512


def _plan_body(ids_ref, tril_ref, upper_ref, dest_ref, cnt_ref, run_ref):
    p = pl.program_id(0)
    i = pl.program_id(1)

    @pl.when((p == 0) & (i == 0))
    def _():
        run_ref[...] = jnp.zeros_like(run_ref)

    @pl.when((p == 1) & (i == 0))
    def _():
        counts = run_ref[...]
        cnt_ref[...] = counts.astype(I32)
        nblk = jnp.floor((counts + (MOE_BLOCK - 1.0)) * (1.0 / MOE_BLOCK))
        blk_start = _dot(nblk.astype(BF16), upper_ref[...])
        run_ref[...] = blk_start * MOE_BLOCK

    ids = ids_ref[...]
    lane = lax.broadcasted_iota(I32, ids.shape, 1)
    oh0 = lane == ids[:, 0:1]
    oh1 = lane == ids[:, 1:2]
    both = jnp.where(oh0 | oh1, 1.0, 0.0).astype(BF16)

    @pl.when(p == 1)
    def _():
        nxt = _dot(tril_ref[...], both) + run_ref[0:1, :]
        d0 = jnp.sum(jnp.where(oh0, nxt, 0.0), axis=-1, keepdims=True)
        d1 = jnp.sum(jnp.where(oh1, nxt, 0.0), axis=-1, keepdims=True)
        dest_ref[...] = jnp.where(lane == 0, d0, jnp.where(lane == 1, d1, 0.0)).astype(I32)

    run_ref[...] = run_ref[...] + _dot(jnp.ones((8, PLAN_TT), BF16), both)


def _plan(ids):
    t = ids.shape[0]
    r = np.arange(PLAN_TT)
    e = np.arange(128)
    strict_tril = jnp.asarray(r[:, None] > r[None, :], BF16)
    strict_upper = jnp.asarray(e[:, None] < e[None, :], BF16)
    return pl.pallas_call(
        _plan_body,
        grid=(2, t // PLAN_TT),
        in_specs=[pl.BlockSpec((PLAN_TT, 128), lambda p, i: (i, 0)),
                  pl.BlockSpec((PLAN_TT, PLAN_TT), lambda p, i: (0, 0)),
                  pl.BlockSpec((128, 128), lambda p, i: (0, 0))],
        out_specs=[pl.BlockSpec((PLAN_TT, 128), lambda p, i: (i * p, 0)),
                   pl.BlockSpec((8, 128), lambda p, i: (0, 0))],
        out_shape=[jax.ShapeDtypeStruct((t, 128), I32), jax.ShapeDtypeStruct((8, 128), I32)],
        scratch_shapes=[pltpu.VMEM((8, 128), F32)],
        compiler_params=_cparams(("arbitrary", "arbitrary")),
        name="plan",
    )(ids, strict_tril, strict_upper)


def _routing_tables(ids128, n_tok):
    n_assign = n_tok * 2
    n_blocks = n_assign // MOE_BLOCK + N_EXPERTS
    dest128, cnt = _plan(ids128)
    dest = dest128[:, :2].reshape(n_assign)
    counts = cnt[0, :N_EXPERTS]
    nb = (counts + MOE_BLOCK - 1) // MOE_BLOCK
    blk_start = jnp.cumsum(nb) - nb
    n_rows = n_blocks * MOE_BLOCK
    row_tok = (jnp.arange(n_rows, dtype=I32) % n_tok).at[dest].set(jnp.arange(n_assign, dtype=I32) // 2)
    pos = dest.reshape(n_tok, 2)

    n_it = (nb + ITEM_BLOCKS - 1) // ITEM_BLOCKS
    it_end = jnp.cumsum(n_it)
    it_start = it_end - n_it
    n_real = it_end[-1]
    i = jnp.arange(N_ITEMS, dtype=I32)
    e_i = jnp.minimum(jnp.searchsorted(it_end, i, side='right').astype(I32), N_EXPERTS - 1)
    k_i = i - it_start[e_i]
    valid = i < n_real
    last_e = e_i[jnp.maximum(n_real - 1, 0)]
    item_e = jnp.where(valid, e_i, last_e).astype(I32)
    fill_off = jnp.sum(nb) + ITEM_BLOCKS * (i - n_real)
    item_fill = jnp.where(valid, 0, jnp.clip(n_blocks - fill_off, 0, ITEM_BLOCKS)).astype(I32)
    item_off = jnp.where(valid, blk_start[e_i] + ITEM_BLOCKS * k_i, jnp.minimum(fill_off, n_blocks - 1)).astype(I32)
    item_nsub = jnp.where(valid, jnp.clip(nb[e_i] - ITEM_BLOCKS * k_i, 0, ITEM_BLOCKS), 0).astype(I32)
    return row_tok, pos, item_e, item_off, item_nsub, item_fill, n_real.reshape(1).astype(I32)


def _pad_lanes(v, start, total=SMALL_COLS):
    return jnp.zeros((1, total), F32).at[0, start:start + v.shape[0]].set(v)


def _layer(x2, c, w_ada, b_ada, norm_pre_mix, norm_post_mix, norm_pre_ffn, norm_post_ffn,
           w_in, gla_w_gate_up, gla_b_gate, gla_norm, ssm_conv_w, ssm_conv_b, ssm_dt_bias,
           ssm_a_log, ssm_d, ssm_norm, w_branch_gla, w_branch_ssm, w_out,
           router_group, router_expert, moe_w_gate, moe_w_up, moe_w_down):
    t, d = x2.shape
    row = lambda v: v.reshape(1, -1)

    mod = _ada(c, w_ada, b_ada)
    mod8 = jnp.concatenate([mod.reshape(6, d), jnp.zeros((2, d), F32)], axis=0)

    w_big, w_small = _repack(w_in.T)
    p, small = _inproj(x2, mod8, row(norm_pre_mix), w_big, w_small)

    o_gla = _gla(p, small, gla_w_gate_up, row(gla_b_gate), row(gla_norm))
    y_ssm = _ssd(p, small, ssm_conv_w, row(ssm_conv_b),
                 _pad_lanes(ssm_dt_bias, SMALL_DT0), _pad_lanes(ssm_a_log, SMALL_DT0),
                 _pad_lanes(ssm_d, SMALL_DT0), row(ssm_norm))
    merged = _merge(o_gla, y_ssm, w_branch_gla, w_branch_ssm, p)

    w_router = jnp.concatenate([router_group, router_expert,
                                jnp.zeros((d, 128 - MOE_GROUPS - N_EXPERTS), F32)], axis=1)
    x1, h2f, logits = _outproj(merged, w_out.astype(BF16), x2, mod8,
                               row(norm_post_mix), row(norm_pre_ffn), w_router)
    ids, wts = _route(logits)

    row_tok, pos, item_e, item_off, item_nsub, item_fill, n_real = _routing_tables(ids, t)
    y_sorted = _experts(h2f, row_tok, moe_w_gate, moe_w_up, moe_w_down,
                        item_e, item_off, item_nsub, item_fill, n_real)
    return _final(y_sorted, pos, wts, x1, mod8, row(norm_post_ffn))


def kernel(x, c, w_ada, b_ada, norm_pre_mix, norm_post_mix, norm_pre_ffn, norm_post_ffn, w_in, gla_w_gate_up, gla_b_gate, gla_norm, ssm_conv_w, ssm_conv_b, ssm_dt_bias, ssm_a_log, ssm_d, ssm_norm, w_branch_gla, w_branch_ssm, w_out, router_group, router_expert, moe_w_gate, moe_w_up, moe_w_down):
    bsz, seq, d = x.shape
    assert bsz == 1 and d == D_MODEL
    x2 = x.reshape(seq, d)
    params = (w_ada, b_ada, norm_pre_mix, norm_post_mix, norm_pre_ffn, norm_post_ffn, w_in, gla_w_gate_up,
              gla_b_gate, gla_norm, ssm_conv_w, ssm_conv_b, ssm_dt_bias, ssm_a_log, ssm_d, ssm_norm,
              w_branch_gla, w_branch_ssm, w_out, router_group, router_expert, moe_w_gate, moe_w_up, moe_w_down)
    for layer in range(w_ada.shape[0]):
        x2 = _layer(x2, c, *(prm[layer] for prm in params))
    return x2.reshape(bsz, seq, d)
```

```python
import jax
import jax.numpy as jnp
import numpy as np
from jax import lax
from jax.experimental import pallas as pl
from jax.experimental.pallas import tpu as pltpu

F32 = jnp.float32
BF16 = jnp.bfloat16
I32 = jnp.int32

D_MODEL = 2048
EPS = 1e-6
LANES = 128
ROW_TILES = D_MODEL // LANES

GLA_HEADS = 4
GLA_HEAD_K = 256
GLA_HEAD_V = 512
GLA_RANK = 16
GLA_NORMALIZER = 16.0
CHUNK = 64

SSM_GROUPS = 8
SSM_HEADS = 64
SSM_HEAD_DIM = 64
SSM_STATE = 128
SSM_CONV = 4
SSM_GROUP_W = 512
SSM_INNER = 4096

N_EXPERTS = 64
EXPERTS_PER_GROUP = 8
MOE_GROUPS = 8
MOE_FF = 1024
MOE_BLOCK = 128
ITEM_BLOCKS = 4

COL_Q, COL_K, COL_V, COL_OG, COL_Z, COL_XS, COL_B, COL_C, COL_GG, COL_GS = (
    0, 1024, 2048, 4096, 6144, 10240, 14336, 15360, 16384, 18432)
P_COLS = 20480
SMALL_COLS = 128
SMALL_DT0 = GLA_RANK

VMEM_LIMIT = 56 * 1024 * 1024


def _cparams(sem, vmem=VMEM_LIMIT):
    return pltpu.CompilerParams(dimension_semantics=sem, vmem_limit_bytes=vmem)


def _dot(a, b):
    return jnp.dot(a, b, preferred_element_type=F32)


def _dot_nt(a, b):
    return lax.dot_general(a, b, (((1,), (1,)), ((), ())), preferred_element_type=F32)


def _dot_tn(a, b):
    return lax.dot_general(a, b, (((0,), (0,)), ((), ())), preferred_element_type=F32)


def _split3(a):
    hi = a.astype(BF16)
    r1 = a - hi.astype(F32)
    mid = r1.astype(BF16)
    lo = (r1 - mid.astype(F32)).astype(BF16)
    return hi, mid, lo


def _dot_sel_r(a, sel):
    hi, mid, lo = _split3(a)
    return _dot(hi, sel) + _dot(mid, sel) + _dot(lo, sel)


def _dot_sel_l(sel, a):
    hi, mid, lo = _split3(a)
    return _dot(sel, hi) + _dot(sel, mid) + _dot(sel, lo)


def _sigmoid(x):
    return 1.0 / (1.0 + jnp.exp(-x))


def _silu(x):
    return x * _sigmoid(x)


def _softplus(x):
    return jnp.maximum(x, 0.0) + jnp.log1p(jnp.exp(-jnp.abs(x)))


def _log_sigmoid(x):
    return jnp.minimum(x, 0.0) - jnp.log1p(jnp.exp(-jnp.abs(x)))


def _rms(x):
    return x * lax.rsqrt(jnp.mean(x * x, axis=-1, keepdims=True) + EPS)


def _ada_body(c_ref, w_ref, b_ref, o_ref):
    c = c_ref[...]
    s = jnp.broadcast_to(_silu(c), (8, c.shape[1])).astype(BF16)
    o_ref[...] = _dot(s, w_ref[...].astype(BF16))[0:1] + b_ref[...]


def _ada(c, w_ada, b_ada):
    d, n = w_ada.shape
    tn = 1024
    return pl.pallas_call(
        _ada_body,
        grid=(n // tn,),
        in_specs=[pl.BlockSpec((1, d), lambda j: (0, 0)),
                  pl.BlockSpec((d, tn), lambda j: (0, j)),
                  pl.BlockSpec((1, tn), lambda j: (0, j))],
        out_specs=pl.BlockSpec((1, tn), lambda j: (0, j)),
        out_shape=jax.ShapeDtypeStruct((1, n), F32),
        compiler_params=_cparams(("parallel",)),
        name="ada",
    )(c, w_ada, b_ada.reshape(1, n))


W_GLR, W_MID, W_DT, W_GATES, W_END = 4096, 4112, 16400, 16464, 20560
REPACK_ROWS = 1024
REPACK_TAIL = 128


def _repack_body(a_ref, b_ref, big_ref, small_ref):
    j = pl.program_id(0)
    n = REPACK_ROWS

    def emit(shift):
        if shift == 0:
            big_ref[...] = a_ref[...].astype(BF16)
        else:
            big_ref[0:n - shift, :] = a_ref[shift:n, :].astype(BF16)
            big_ref[n - shift:n, :] = b_ref[0:shift, :].astype(BF16)

    first_mid, first_gates = COL_OG // n, COL_GG // n

    @pl.when(j < first_mid)
    def _():
        emit(0)

    @pl.when((j >= first_mid) & (j < first_gates))
    def _():
        emit(W_MID - COL_OG)

    @pl.when(j >= first_gates)
    def _():
        emit(W_GATES - COL_GG)

    @pl.when(j == 0)
    def _():
        small_ref[...] = jnp.zeros_like(small_ref)

    @pl.when(j == W_GLR // n)
    def _():
        small_ref[0:GLA_RANK, :] = a_ref[0:GLA_RANK, :].astype(BF16)

    @pl.when(j == W_DT // n)
    def _():
        small_ref[SMALL_DT0:SMALL_DT0 + SSM_HEADS, :] = a_ref[SMALL_DT0:SMALL_DT0 + SSM_HEADS, :].astype(BF16)


def _repack(w_t):
    n_in, d = w_t.shape
    n = REPACK_ROWS
    assert n_in == W_END and W_GLR % n == 0 and W_GLR + GLA_RANK == W_MID and W_DT % n == SMALL_DT0
    assert COL_OG % n == 0 and COL_GG % n == 0 and W_GATES - COL_GG <= REPACK_TAIL
    return pl.pallas_call(
        _repack_body,
        grid=(P_COLS // n,),
        in_specs=[pl.BlockSpec((n, d), lambda j: (j, 0)),
                  pl.BlockSpec((REPACK_TAIL, d), lambda j: ((j + 1) * (n // REPACK_TAIL), 0))],
        out_specs=[pl.BlockSpec((n, d), lambda j: (j, 0)),
                   pl.BlockSpec((SMALL_COLS, d), lambda j: (0, 0))],
        out_shape=[jax.ShapeDtypeStruct((P_COLS, d), BF16), jax.ShapeDtypeStruct((SMALL_COLS, d), BF16)],
        compiler_params=_cparams(("arbitrary",)),
        name="repack",
    )(w_t, w_t)


def _inproj_body(x_ref, mod_ref, g_ref, w_ref, ws_ref, p_ref, s_ref, h_ref):
    @pl.when(pl.program_id(1) == 0)
    def _():
        h = _rms(x_ref[...]) * g_ref[...] * (1.0 + mod_ref[1:2, :]) + mod_ref[0:1, :]
        hb = h.astype(BF16)
        h_ref[...] = hb
        s_ref[...] = _dot_nt(hb, ws_ref[...])

    p_ref[...] = _dot_nt(h_ref[...], w_ref[...]).astype(BF16)


def _inproj(x2, mod8, gain, w_big, w_small):
    t, d = x2.shape
    n = w_big.shape[0]
    tm, tn = 1024, 2048
    return pl.pallas_call(
        _inproj_body,
        grid=(t // tm, n // tn),
        in_specs=[pl.BlockSpec((tm, d), lambda m, j: (m, 0)),
                  pl.BlockSpec((8, d), lambda m, j: (0, 0)),
                  pl.BlockSpec((1, d), lambda m, j: (0, 0)),
                  pl.BlockSpec((tn, d), lambda m, j: (j, 0)),
                  pl.BlockSpec((SMALL_COLS, d), lambda m, j: (0, 0))],
        out_specs=[pl.BlockSpec((tm, tn), lambda m, j: (m, j)),
                   pl.BlockSpec((tm, SMALL_COLS), lambda m, j: (m, 0))],
        out_shape=[jax.ShapeDtypeStruct((t, n), BF16),
                   jax.ShapeDtypeStruct((t, SMALL_COLS), F32)],
        scratch_shapes=[pltpu.VMEM((tm, d), BF16)],
        compiler_params=_cparams(("parallel", "arbitrary")),
        name="inproj",
    )(x2, mod8, gain, w_big, w_small)


GLA_TB = 512
GLA_SCORE_ROWS = 256


def _gla_body(q_ref, k_ref, v_ref, og_ref, sm_ref, wup_ref, bg_ref, gn_ref, o_ref, st_ref):
    @pl.when(pl.program_id(1) == 0)
    def _():
        st_ref[...] = jnp.zeros_like(st_ref)

    tb, C = GLA_TB, CHUNK
    nch = tb // C
    sb = GLA_SCORE_ROWS
    r = lax.broadcasted_iota(I32, (sb, sb), 0)
    c = lax.broadcasted_iota(I32, (sb, sb), 1)
    causal = (r // C == c // C) & (r >= c)
    r2 = lax.broadcasted_iota(I32, (2 * C, 2 * C), 0)
    c2 = lax.broadcasted_iota(I32, (2 * C, 2 * C), 1)
    tril2 = ((r2 // C == c2 // C) & (r2 >= c2)).astype(BF16)

    glr = sm_ref[:, 0:GLA_RANK].astype(BF16)
    pre = _dot(glr, wup_ref[...].astype(BF16)) + bg_ref[...]
    log_a = _log_sigmoid(pre) / GLA_NORMALIZER
    b = jnp.concatenate([_dot_sel_l(tril2, log_a[i * 2 * C:(i + 1) * 2 * C]) for i in range(nch // 2)],
                        axis=0)
    b_last = [b[(i + 1) * C - 1:(i + 1) * C, :] for i in range(nch)]
    b_end = jnp.concatenate([jnp.broadcast_to(bl, (C, bl.shape[1])) for bl in b_last], axis=0)

    q = q_ref[...].astype(F32) * (GLA_HEAD_K ** -0.5)
    k = k_ref[...].astype(F32)
    v = v_ref[...]
    q_dec = (q * jnp.exp(b)).astype(BF16)
    k_inv = (k * jnp.exp(-b)).astype(BF16)
    k_end = (k * jnp.exp(b_end - b)).astype(BF16)
    o_intra = []
    for i in range(tb // sb):
        rows = slice(i * sb, (i + 1) * sb)
        scores = jnp.where(causal, _dot_nt(q_dec[rows], k_inv[rows]), 0.0)
        o_intra.append(_dot(scores.astype(BF16), v[rows]))
    o_intra = jnp.concatenate(o_intra, axis=0)

    st = st_ref[...]
    o_inter = []
    for i in range(nch):
        rows = slice(i * C, (i + 1) * C)
        o_inter.append(_dot_nt(q_dec[rows], st.astype(BF16)))
        st = st * jnp.exp(b_last[i]) + _dot_tn(v[rows], k_end[rows])
    st_ref[...] = st

    o = o_intra + jnp.concatenate(o_inter, axis=0)
    og = og_ref[...].astype(F32)
    o_ref[...] = (_rms(o) * gn_ref[...] * _silu(og)).astype(BF16)


def _gla(p, small, wup, bg, gn):
    t = p.shape[0]
    tb = GLA_TB
    kb, vb = GLA_HEAD_K, GLA_HEAD_V
    return pl.pallas_call(
        _gla_body,
        grid=(GLA_HEADS, t // tb),
        in_specs=[pl.BlockSpec((tb, kb), lambda h, i: (i, COL_Q // kb + h)),
                  pl.BlockSpec((tb, kb), lambda h, i: (i, COL_K // kb + h)),
                  pl.BlockSpec((tb, vb), lambda h, i: (i, COL_V // vb + h)),
                  pl.BlockSpec((tb, vb), lambda h, i: (i, COL_OG // vb + h)),
                  pl.BlockSpec((tb, SMALL_COLS), lambda h, i: (i, 0)),
                  pl.BlockSpec((GLA_RANK, kb), lambda h, i: (0, h)),
                  pl.BlockSpec((1, kb), lambda h, i: (0, h)),
                  pl.BlockSpec((1, vb), lambda h, i: (0, h))],
        out_specs=pl.BlockSpec((tb, vb), lambda h, i: (i, h)),
        out_shape=jax.ShapeDtypeStruct((t, GLA_HEADS * vb), BF16),
        scratch_shapes=[pltpu.VMEM((vb, kb), F32)],
        compiler_params=_cparams(("parallel", "arbitrary")),
        name="gla",
    )(p, p, p, p, small, wup, bg, gn)


SSD_TB = 512
HALO = 8


def _ssd_body(xs_ref, b_ref, c_ref, z_ref, sm_ref, cwx_ref, cwb_ref, cwc_ref, cbx_ref, cbb_ref, cbc_ref,
              dtb_ref, alog_ref, dsk_ref, ng_ref, o_ref,
              ex_ref, eb_ref, ec_ref, at_ref, ht_ref):
    g = pl.program_id(0)
    t = pl.program_id(1)
    tb = SSD_TB
    gw = SSM_GROUP_W

    @pl.when(t == 0)
    def _():
        ht_ref[...] = jnp.zeros_like(ht_ref)
        ex_ref[0:HALO, :] = jnp.zeros((HALO, gw), F32)
        eb_ref[0:HALO, :] = jnp.zeros((HALO, SSM_STATE), F32)
        ec_ref[0:HALO, :] = jnp.zeros((HALO, SSM_STATE), F32)

    @pl.when(t > 0)
    def _():
        ex_ref[0:HALO, :] = ex_ref[tb:tb + HALO, :]
        eb_ref[0:HALO, :] = eb_ref[tb:tb + HALO, :]
        ec_ref[0:HALO, :] = ec_ref[tb:tb + HALO, :]

    def conv_silu(u_ref, e_ref, w_ref, bias_ref):
        e_ref[HALO:HALO + tb, :] = u_ref[...].astype(F32)
        acc = bias_ref[...] + w_ref[0:1, :] * e_ref[HALO - 3:HALO - 3 + tb, :]
        for kk in range(1, SSM_CONV):
            acc = acc + w_ref[kk:kk + 1, :] * e_ref[HALO - 3 + kk:HALO - 3 + kk + tb, :]
        return _silu(acc)

    xa = conv_silu(xs_ref, ex_ref, cwx_ref, cbx_ref)
    ba = conv_silu(b_ref, eb_ref, cwb_ref, cbb_ref).astype(BF16)
    ca = conv_silu(c_ref, ec_ref, cwc_ref, cbc_ref).astype(BF16)

    L = CHUNK
    nch = tb // L
    rep = gw // L
    hpg = gw // SSM_HEAD_DIM
    head0 = pl.multiple_of(SMALL_DT0 + g * hpg, hpg)

    e_row = lax.broadcasted_iota(I32, (SMALL_COLS, gw), 0)
    e_col = lax.broadcasted_iota(I32, (SMALL_COLS, gw), 1)
    expand = (e_row == head0 + e_col // SSM_HEAD_DIM).astype(BF16)

    dt_small = _softplus(sm_ref[...] + dtb_ref[...])
    adt_small = dt_small * (-jnp.exp(alog_ref[...]))
    rb = lax.broadcasted_iota(I32, (2 * L, 2 * L), 0)
    cb = lax.broadcasted_iota(I32, (2 * L, 2 * L), 1)
    tril2 = ((rb // L == cb // L) & (rb >= cb)).astype(BF16)
    acum_small = jnp.concatenate([_dot_sel_l(tril2, adt_small[k * 2 * L:(k + 1) * 2 * L])
                                  for k in range(nch // 2)], axis=0)
    at_ref[...] = acum_small.T
    heads = at_ref[pl.ds(head0, hpg), :]
    dt_exp = _dot_sel_r(dt_small, expand)
    acum = _dot_sel_r(acum_small, expand)
    d_exp = _dot_sel_r(jnp.broadcast_to(dsk_ref[...], (8, SMALL_COLS)), expand)[0:1]
    xdt = xa * dt_exp
    xdt_b = xdt.astype(BF16)
    e_acum = jnp.exp(acum)

    li = lax.broadcasted_iota(I32, (L, gw), 0)
    lj = lax.broadcasted_iota(I32, (L, gw), 1) % L
    causal_t = li >= lj
    hi = lax.broadcasted_iota(I32, (hpg, gw), 0)
    hj = lax.broadcasted_iota(I32, (hpg, gw), 1) // L
    headmask = hi == hj
    ones_h = jnp.ones((L, hpg), BF16)
    bi = lax.broadcasted_iota(I32, (gw, gw), 0) // L
    bj = lax.broadcasted_iota(I32, (gw, gw), 1) // SSM_HEAD_DIM
    blockmask = bi == bj
    masked_out = -1e30

    ht = ht_ref[...]
    for c in range(nch):
        rows = slice(c * L, (c + 1) * L)
        acum_c = acum[rows]
        a_rows = jnp.concatenate([heads[:, c * L:(c + 1) * L]] * rep, axis=1)
        rterm = _dot_sel_l(ones_h, jnp.where(headmask, a_rows, 0.0))
        decay = jnp.exp(jnp.where(causal_t, acum_c - rterm, masked_out))
        cc = ca[rows]
        bc = ba[rows]
        cb_t = _dot_nt(cc, jnp.concatenate([bc] * rep, axis=0))
        m = (cb_t * decay).astype(BF16)
        bd = jnp.where(blockmask, jnp.concatenate([xdt_b[rows]] * rep, axis=0), jnp.zeros((), BF16))
        y_diag = _dot(m, bd)
        y_off = _dot(cc, ht.astype(BF16)) * e_acum[rows]
        a_last = acum_c[L - 1:L, :]
        xd = (xdt[rows] * jnp.exp(a_last - acum_c)).astype(BF16)
        ht = ht * jnp.exp(a_last) + _dot_tn(bc, xd)
        y = y_diag + y_off + d_exp * xa[rows]
        y = y * _silu(z_ref[rows, :].astype(F32))
        o_ref[rows, :] = (_rms(y) * ng_ref[...]).astype(BF16)
    ht_ref[...] = ht


def _ssd(p, small, conv_w, conv_b, dt_bias_s, a_log_s, d_skip_s, norm_g):
    t = p.shape[0]
    tb = SSD_TB
    gw, ns = SSM_GROUP_W, SSM_STATE
    xs0, b0, c0 = 0, SSM_INNER // ns, (SSM_INNER + SSM_GROUPS * ns) // ns
    row = lambda w, off: pl.BlockSpec((1, w), lambda g, i: (0, off + g))
    return pl.pallas_call(
        _ssd_body,
        grid=(SSM_GROUPS, t // tb),
        in_specs=[pl.BlockSpec((tb, gw), lambda g, i: (i, COL_XS // gw + g)),
                  pl.BlockSpec((tb, ns), lambda g, i: (i, COL_B // ns + g)),
                  pl.BlockSpec((tb, ns), lambda g, i: (i, COL_C // ns + g)),
                  pl.BlockSpec((tb, gw), lambda g, i: (i, COL_Z // gw + g)),
                  pl.BlockSpec((tb, SMALL_COLS), lambda g, i: (i, 0)),
                  pl.BlockSpec((SSM_CONV, gw), lambda g, i: (0, xs0 + g)),
                  pl.BlockSpec((SSM_CONV, ns), lambda g, i: (0, b0 + g)),
                  pl.BlockSpec((SSM_CONV, ns), lambda g, i: (0, c0 + g)),
                  row(gw, xs0), row(ns, b0), row(ns, c0),
                  pl.BlockSpec((1, SMALL_COLS), lambda g, i: (0, 0)),
                  pl.BlockSpec((1, SMALL_COLS), lambda g, i: (0, 0)),
                  pl.BlockSpec((1, SMALL_COLS), lambda g, i: (0, 0)),
                  pl.BlockSpec((1, gw), lambda g, i: (0, g))],
        out_specs=pl.BlockSpec((tb, gw), lambda g, i: (i, g)),
        out_shape=jax.ShapeDtypeStruct((t, SSM_INNER), BF16),
        scratch_shapes=[pltpu.VMEM((tb + HALO, gw), F32),
                        pltpu.VMEM((tb + HALO, ns), F32),
                        pltpu.VMEM((tb + HALO, ns), F32),
                        pltpu.VMEM((SMALL_COLS, tb), F32),
                        pltpu.VMEM((ns, gw), F32)],
        compiler_params=_cparams(("parallel", "arbitrary")),
        name="ssd",
    )(p, p, p, p, small, conv_w, conv_w, conv_w, conv_b, conv_b, conv_b,
      dt_bias_s, a_log_s, d_skip_s, norm_g)


def _merge_body(a1_ref, a2_ref, w1_ref, w2_ref, gg_ref, gs_ref, o_ref, w1b_ref, w2b_ref):
    @pl.when(pl.program_id(1) == 0)
    def _():
        w1b_ref[...] = w1_ref[...].astype(BF16)
        w2b_ref[...] = w2_ref[...].astype(BF16)

    y1 = _dot(a1_ref[...], w1b_ref[...])
    y2 = _dot(a2_ref[...], w2b_ref[...])
    o_ref[...] = (_sigmoid(gg_ref[...].astype(F32)) * y1 + _sigmoid(gs_ref[...].astype(F32)) * y2).astype(BF16)


def _merge(o_gla, y_ssm, w1, w2, p):
    t, k1 = o_gla.shape
    k2 = y_ssm.shape[1]
    n = w1.shape[1]
    tm, tn = 512, 512
    return pl.pallas_call(
        _merge_body,
        grid=(n // tn, t // tm),
        in_specs=[pl.BlockSpec((tm, k1), lambda j, m: (m, 0)),
                  pl.BlockSpec((tm, k2), lambda j, m: (m, 0)),
                  pl.BlockSpec((k1, tn), lambda j, m: (0, j)),
                  pl.BlockSpec((k2, tn), lambda j, m: (0, j)),
                  pl.BlockSpec((tm, tn), lambda j, m: (m, COL_GG // tn + j)),
                  pl.BlockSpec((tm, tn), lambda j, m: (m, COL_GS // tn + j))],
        out_specs=pl.BlockSpec((tm, tn), lambda j, m: (m, j)),
        out_shape=jax.ShapeDtypeStruct((t, n), BF16),
        scratch_shapes=[pltpu.VMEM((k1, tn), BF16), pltpu.VMEM((k2, tn), BF16)],
        compiler_params=_cparams(("parallel", "arbitrary")),
        name="merge",
    )(o_gla, y_ssm, w1, w2, p, p)


def _outproj_body(m_ref, w_ref, x_ref, mod_ref, gpost_ref, gpre_ref, wr_ref, x1_ref, hf_ref, lg_ref):
    mix = _dot(m_ref[...], w_ref[...])
    x1 = x_ref[...] + mod_ref[2:3, :] * (_rms(mix) * gpost_ref[...])
    x1_ref[...] = x1
    h = _rms(x1) * gpre_ref[...] * (1.0 + mod_ref[4:5, :]) + mod_ref[3:4, :]
    for s in range(ROW_TILES):
        hf_ref[:, s, :] = h[:, s * LANES:(s + 1) * LANES]
    h_hi = h.astype(BF16)
    h_lo = (h - h_hi.astype(F32)).astype(BF16)
    wr = wr_ref[...]
    w_hi = wr.astype(BF16)
    w_lo = (wr - w_hi.astype(F32)).astype(BF16)
    lg_ref[...] = _dot(h_hi, w_hi) + _dot(h_hi, w_lo) + _dot(h_lo, w_hi)


def _outproj(merged, w_out, x2, mod8, g_post, g_pre, w_router):
    t, d = x2.shape
    tm = 256
    full = lambda r, c: pl.BlockSpec((r, c), lambda m: (0, 0))
    tile = lambda c: pl.BlockSpec((tm, c), lambda m: (m, 0))
    return pl.pallas_call(
        _outproj_body,
        grid=(t // tm,),
        in_specs=[tile(d), full(d, d), tile(d), full(8, d), full(1, d), full(1, d), full(d, 128)],
        out_specs=[tile(d), pl.BlockSpec((tm, ROW_TILES, LANES), lambda m: (m, 0, 0)), tile(128)],
        out_shape=[jax.ShapeDtypeStruct((t, d), F32),
                   jax.ShapeDtypeStruct((t, ROW_TILES, LANES), F32),
                   jax.ShapeDtypeStruct((t, 128), F32)],
        compiler_params=_cparams(("parallel",)),
        name="outproj",
    )(merged, w_out, x2, mod8, g_post, g_pre, w_router)


def _route_body(lg_ref, id_ref, w_ref):
    lg = lg_ref[...]
    lane = lax.broadcasted_iota(I32, lg.shape, 1)
    lane_f = lane.astype(F32)
    neg = jnp.float32(-jnp.inf)

    def first_argmax(vals, mx):
        return jnp.min(jnp.where(vals == mx, lane_f, 1e9), axis=-1, keepdims=True).astype(I32)

    gmask = lane < MOE_GROUPS
    gl = jnp.where(gmask, lg, neg)
    gmax = jnp.max(gl, axis=-1, keepdims=True)
    gsum = jnp.sum(jnp.where(gmask, jnp.exp(gl - gmax), 0.0), axis=-1, keepdims=True)
    g_w = 1.0 / gsum
    g_idx = first_argmax(gl, gmax)
    lo = MOE_GROUPS + g_idx * EXPERTS_PER_GROUP
    emask = (lane >= lo) & (lane < lo + EXPERTS_PER_GROUP)
    el = jnp.where(emask, lg, neg)
    m1 = jnp.max(el, axis=-1, keepdims=True)
    i1 = first_argmax(el, m1)
    el2 = jnp.where(lane == i1, neg, el)
    m2 = jnp.max(el2, axis=-1, keepdims=True)
    i2 = first_argmax(el2, m2)
    r = jnp.exp(m2 - m1)
    w1 = g_w / (1.0 + r)
    w2 = g_w * r / (1.0 + r)
    id_ref[...] = jnp.where(lane == 0, i1 - MOE_GROUPS, jnp.where(lane == 1, i2 - MOE_GROUPS, 0))
    w_ref[...] = jnp.where(lane == 0, w1, jnp.where(lane == 1, w2, 0.0))


def _route(logits):
    t = logits.shape[0]
    tm = 1024
    spec = pl.BlockSpec((tm, 128), lambda m: (m, 0))
    return pl.pallas_call(
        _route_body,
        grid=(t // tm,),
        in_specs=[spec],
        out_specs=[spec, spec],
        out_shape=[jax.ShapeDtypeStruct((t, 128), I32), jax.ShapeDtypeStruct((t, 128), F32)],
        compiler_params=_cparams(("parallel",)),
        name="route",
    )(logits)


GATHER_ROWS = 512


def _issue_rows(idx_ref, src_ref, buf_ref, slot, sem):
    def body(pair, carry):
        for prio in range(2):
            r = 2 * pair + prio
            pltpu.make_async_copy(src_ref.at[idx_ref[0, 0, r]], buf_ref.at[slot, :, r, :],
                                  sem.at[slot]).start(priority=prio)
        return carry

    lax.fori_loop(0, GATHER_ROWS // 2, body, 0, unroll=4)


def _wait_rows(buf_ref, slot, sem):
    pltpu.make_async_copy(buf_ref.at[slot], buf_ref.at[slot], sem.at[slot]).wait()


def _gather_step(idx_ref, idx_next_ref, src_ref, buf_ref, sem):
    i = pl.program_id(0)
    slot = i % 2

    @pl.when(i == 0)
    def _():
        _issue_rows(idx_ref, src_ref, buf_ref, 0, sem)

    @pl.when(i + 1 < pl.num_programs(0))
    def _():
        _issue_rows(idx_next_ref, src_ref, buf_ref, 1 - slot, sem)

    _wait_rows(buf_ref, slot, sem)
    return slot


def _gather_specs(nsteps):
    smem = lambda f: pl.BlockSpec((1, 1, GATHER_ROWS), f, memory_space=pltpu.SMEM)
    return [smem(lambda i: (i, 0, 0)),
            smem(lambda i: (jnp.minimum(i + 1, nsteps - 1), 0, 0)),
            pl.BlockSpec(memory_space=pl.ANY)]


GATHER_SCRATCH = [pltpu.VMEM((2, ROW_TILES, GATHER_ROWS, LANES), F32), pltpu.SemaphoreType.DMA((2,))]


MOE_FC = 512
MOE_J = MOE_FF // MOE_FC
N_ITEMS = (16384 // MOE_BLOCK + N_EXPERTS) // ITEM_BLOCKS + (N_EXPERTS * (ITEM_BLOCKS - 1)) // ITEM_BLOCKS


def _experts_body(ie_ref, io_ref, ins_ref, ifl_ref, nr_ref, wg_ref, wu_ref, wd_ref, h_ref,
                  f0_ref, f1_ref, f2_ref, f3_ref, n0_ref, n1_ref, n2_ref, n3_ref, ys_ref,
                  xst_ref, xb_ref, acc_ref, sem_in, sem_out):
    i = pl.program_id(0)
    j = pl.program_id(1)
    n_items = pl.num_programs(0)
    nsub = ins_ref[i]
    nfill = ifl_ref[i]
    slot = i % 2
    blk = MOE_BLOCK

    def row0(item):
        return pl.multiple_of(io_ref[item] * blk, blk)

    def start_rows(idx_refs, count, sl):
        def block(s):
            def body(r, carry):
                pltpu.make_async_copy(h_ref.at[idx_refs[s][0, 0, r]], xst_ref.at[sl, :, s * blk + r, :],
                                      sem_in.at[sl]).start()
                return carry

            lax.fori_loop(0, blk, body, 0, unroll=8)

        for_blocks(count, block)

    def wait_rows(count, sl):
        @pl.when(count > 0)
        def _():
            part = xst_ref.at[sl, :, pl.ds(0, count * blk), :]
            pltpu.make_async_copy(part, part, sem_in.at[sl]).wait()

    def start_y(item, s):
        for c in range(ROW_TILES):
            pltpu.make_async_copy(acc_ref.at[item % 2, s, :, pl.ds(c * LANES, LANES)],
                                  ys_ref.at[pl.ds(row0(item) + s * blk, blk), c, :], sem_out.at[s]).start()

    def wait_y(s):
        pltpu.make_async_copy(acc_ref.at[0, s], acc_ref.at[0, s], sem_out.at[s]).wait()

    def for_blocks(count, fn):
        for s in range(ITEM_BLOCKS):
            @pl.when(s < count)
            def _():
                fn(s)

    @pl.when(j == 0)
    def _():
        @pl.when(i == 0)
        def _():
            start_rows((f0_ref, f1_ref, f2_ref, f3_ref), nsub, 0)

        wait_rows(nsub, slot)

        def to_bf16(s):
            for c in range(ROW_TILES):
                xb_ref[s, :, c * LANES:(c + 1) * LANES] = xst_ref[slot, c, s * blk:(s + 1) * blk, :].astype(BF16)

        for_blocks(nsub, to_bf16)

    @pl.when((j == MOE_J - 1) & (i + 1 < n_items))
    def _():
        start_rows((n0_ref, n1_ref, n2_ref, n3_ref), ins_ref[jnp.minimum(i + 1, n_items - 1)], 1 - slot)

    for n in range(1, ITEM_BLOCKS + 1):
        @pl.when(nsub == n)
        def _():
            x = xb_ref[0:n].reshape(n * blk, D_MODEL)
            gate = _dot(x, wg_ref[...].astype(BF16))
            up = _dot(x, wu_ref[...].astype(BF16))
            hid = (_silu(gate) * up).astype(BF16)
            y = _dot(hid, wd_ref[...].astype(BF16)).reshape(n, blk, D_MODEL)

            @pl.when(j == 0)
            def _():
                acc_ref[slot, 0:n] = y

            @pl.when(j > 0)
            def _():
                acc_ref[slot, 0:n] = acc_ref[slot, 0:n] + y

    def wait_prev_y():
        @pl.when(i > 0)
        def _():
            for_blocks(ins_ref[jnp.maximum(i - 1, 0)], wait_y)

    @pl.when((nsub > 0) & (j == MOE_J - 1))
    def _():
        wait_prev_y()
        for_blocks(nsub, lambda s: start_y(i, s))

        @pl.when(i == n_items - 1)
        def _():
            for_blocks(nsub, wait_y)

    @pl.when((nsub == 0) & (j == 0))
    def _():
        wait_prev_y()

        @pl.when(nfill > 0)
        def _():
            acc_ref[slot] = jnp.zeros(acc_ref.shape[1:], F32)
            for_blocks(nfill, lambda s: start_y(i, s))
            for_blocks(nfill, wait_y)


def _experts(h_rows, row_tok, w_gate, w_up, w_down, item_e, item_off, item_nsub, item_fill, n_real):
    n_rows = row_tok.shape[0]
    n_blocks = n_rows // MOE_BLOCK
    d = D_MODEL

    def w_in_map(i, j, ie, io, ins, ifl, nr):
        return (ie[i], 0, jnp.where(i < nr[0], j, MOE_J - 1))

    def w_dn_map(i, j, ie, io, ins, ifl, nr):
        return (ie[i], jnp.where(i < nr[0], j, MOE_J - 1), 0)

    def tok_spec(s, next_item):
        def index_map(i, j, ie, io, ins, ifl, nr):
            item = jnp.minimum(i + 1, N_ITEMS - 1) if next_item else 0
            return (jnp.minimum(io[item] + s, n_blocks - 1), 0, 0)

        return pl.BlockSpec((1, 1, MOE_BLOCK), index_map, memory_space=pltpu.SMEM)

    grid_spec = pltpu.PrefetchScalarGridSpec(
        num_scalar_prefetch=5,
        grid=(N_ITEMS, MOE_J),
        in_specs=[pl.BlockSpec((None, d, MOE_FC), w_in_map),
                  pl.BlockSpec((None, d, MOE_FC), w_in_map),
                  pl.BlockSpec((None, MOE_FC, d), w_dn_map),
                  pl.BlockSpec(memory_space=pl.ANY)]
                 + [tok_spec(s, False) for s in range(ITEM_BLOCKS)]
                 + [tok_spec(s, True) for s in range(ITEM_BLOCKS)],
        out_specs=pl.BlockSpec(memory_space=pl.ANY),
        scratch_shapes=[pltpu.VMEM((2, ROW_TILES, ITEM_BLOCKS * MOE_BLOCK, LANES), F32),
                        pltpu.VMEM((ITEM_BLOCKS, MOE_BLOCK, d), BF16),
                        pltpu.VMEM((2, ITEM_BLOCKS, MOE_BLOCK, d), F32),
                        pltpu.SemaphoreType.DMA((2,)),
                        pltpu.SemaphoreType.DMA((ITEM_BLOCKS,))],
    )
    tok3 = row_tok.reshape(n_blocks, 1, MOE_BLOCK)
    return pl.pallas_call(
        _experts_body,
        grid_spec=grid_spec,
        out_shape=jax.ShapeDtypeStruct((n_rows, ROW_TILES, LANES), F32),
        compiler_params=_cparams(("arbitrary", "arbitrary")),
        name="experts",
    )(item_e, item_off, item_nsub, item_fill, n_real, w_gate, w_up, w_down, h_rows, *([tok3] * (2 * ITEM_BLOCKS)))


FINAL_TM = GATHER_ROWS // 2


def _final_body(idx_ref, idx_next_ref, ys_ref, w_ref, x1_ref, mod_ref, g_ref, o_ref, buf_ref, sem):
    slot = _gather_step(idx_ref, idx_next_ref, ys_ref, buf_ref, sem)
    tm = FINAL_TM
    w = w_ref[...]
    w0, w1 = w[:, 0:1], w[:, 1:2]
    ffn = jnp.concatenate([w0 * buf_ref[slot, c, 0:tm, :] + w1 * buf_ref[slot, c, tm:2 * tm, :]
                           for c in range(ROW_TILES)], axis=1)
    o_ref[...] = x1_ref[...] + mod_ref[5:6, :] * (_rms(ffn) * g_ref[...])


def _final(y_sorted, pos, wts, x1, mod8, g_post):
    t, d = x1.shape
    tm = FINAL_TM
    nt = t // tm
    idx3 = pos.reshape(nt, tm, 2).transpose(0, 2, 1).reshape(nt, 1, 2 * tm)
    return pl.pallas_call(
        _final_body,
        grid=(nt,),
        in_specs=_gather_specs(nt) + [pl.BlockSpec((tm, 128), lambda m: (m, 0)),
                                      pl.BlockSpec((tm, d), lambda m: (m, 0)),
                                      pl.BlockSpec((8, d), lambda m: (0, 0)),
                                      pl.BlockSpec((1, d), lambda m: (0, 0))],
        out_specs=pl.BlockSpec((tm, d), lambda m: (m, 0)),
        out_shape=jax.ShapeDtypeStruct((t, d), F32),
        scratch_shapes=GATHER_SCRATCH,
        compiler_params=_cparams(("arbitrary",)),
        name="final",
    )(idx3, idx3, y_sorted, wts, x1, mod8, g_post)


PLAN_TT = 512


def _plan_body(ids_ref, tril_ref, upper_ref, dest_ref, cnt_ref, run_ref):
    p = pl.program_id(0)
    i = pl.program_id(1)

    @pl.when((p == 0) & (i == 0))
    def _():
        run_ref[...] = jnp.zeros_like(run_ref)

    @pl.when((p == 1) & (i == 0))
    def _():
        counts = run_ref[...]
        cnt_ref[...] = counts.astype(I32)
        nblk = jnp.floor((counts + (MOE_BLOCK - 1.0)) * (1.0 / MOE_BLOCK))
        blk_start = _dot(nblk.astype(BF16), upper_ref[...])
        run_ref[...] = blk_start * MOE_BLOCK

    ids = ids_ref[...]
    lane = lax.broadcasted_iota(I32, ids.shape, 1)
    oh0 = lane == ids[:, 0:1]
    oh1 = lane == ids[:, 1:2]
    both = jnp.where(oh0 | oh1, 1.0, 0.0).astype(BF16)

    @pl.when(p == 1)
    def _():
        nxt = _dot(tril_ref[...], both) + run_ref[0:1, :]
        d0 = jnp.sum(jnp.where(oh0, nxt, 0.0), axis=-1, keepdims=True)
        d1 = jnp.sum(jnp.where(oh1, nxt, 0.0), axis=-1, keepdims=True)
        dest_ref[...] = jnp.where(lane == 0, d0, jnp.where(lane == 1, d1, 0.0)).astype(I32)

    run_ref[...] = run_ref[...] + _dot(jnp.ones((8, PLAN_TT), BF16), both)


def _plan(ids):
    t = ids.shape[0]
    r = np.arange(PLAN_TT)
    e = np.arange(128)
    strict_tril = jnp.asarray(r[:, None] > r[None, :], BF16)
    strict_upper = jnp.asarray(e[:, None] < e[None, :], BF16)
    return pl.pallas_call(
        _plan_body,
        grid=(2, t // PLAN_TT),
        in_specs=[pl.BlockSpec((PLAN_TT, 128), lambda p, i: (i, 0)),
                  pl.BlockSpec((PLAN_TT, PLAN_TT), lambda p, i: (0, 0)),
                  pl.BlockSpec((128, 128), lambda p, i: (0, 0))],
        out_specs=[pl.BlockSpec((PLAN_TT, 128), lambda p, i: (i * p, 0)),
                   pl.BlockSpec((8, 128), lambda p, i: (0, 0))],
        out_shape=[jax.ShapeDtypeStruct((t, 128), I32), jax.ShapeDtypeStruct((8, 128), I32)],
        scratch_shapes=[pltpu.VMEM((8, 128), F32)],
        compiler_params=_cparams(("arbitrary", "arbitrary")),
        name="plan",
    )(ids, strict_tril, strict_upper)


def _routing_tables(ids128, n_tok):
    n_assign = n_tok * 2
    n_blocks = n_assign // MOE_BLOCK + N_EXPERTS
    dest128, cnt = _plan(ids128)
    dest = dest128[:, :2].reshape(n_assign)
    counts = cnt[0, :N_EXPERTS]
    nb = (counts + MOE_BLOCK - 1) // MOE_BLOCK
    blk_start = jnp.cumsum(nb) - nb
    n_rows = n_blocks * MOE_BLOCK
    row_tok = (jnp.arange(n_rows, dtype=I32) % n_tok).at[dest].set(jnp.arange(n_assign, dtype=I32) // 2)
    pos = dest.reshape(n_tok, 2)

    n_it = (nb + ITEM_BLOCKS - 1) // ITEM_BLOCKS
    it_end = jnp.cumsum(n_it)
    it_start = it_end - n_it
    n_real = it_end[-1]
    i = jnp.arange(N_ITEMS, dtype=I32)
    e_i = jnp.minimum(jnp.searchsorted(it_end, i, side='right').astype(I32), N_EXPERTS - 1)
    k_i = i - it_start[e_i]
    valid = i < n_real
    last_e = e_i[jnp.maximum(n_real - 1, 0)]
    item_e = jnp.where(valid, e_i, last_e).astype(I32)
    fill_off = jnp.sum(nb) + ITEM_BLOCKS * (i - n_real)
    item_fill = jnp.where(valid, 0, jnp.clip(n_blocks - fill_off, 0, ITEM_BLOCKS)).astype(I32)
    item_off = jnp.where(valid, blk_start[e_i] + ITEM_BLOCKS * k_i, jnp.minimum(fill_off, n_blocks - 1)).astype(I32)
    item_nsub = jnp.where(valid, jnp.clip(nb[e_i] - ITEM_BLOCKS * k_i, 0, ITEM_BLOCKS), 0).astype(I32)
    return row_tok, pos, item_e, item_off, item_nsub, item_fill, n_real.reshape(1).astype(I32)


def _pad_lanes(v, start, total=SMALL_COLS):
    return jnp.zeros((1, total), F32).at[0, start:start + v.shape[0]].set(v)


def _layer(x2, c, w_ada, b_ada, norm_pre_mix, norm_post_mix, norm_pre_ffn, norm_post_ffn,
           w_in, gla_w_gate_up, gla_b_gate, gla_norm, ssm_conv_w, ssm_conv_b, ssm_dt_bias,
           ssm_a_log, ssm_d, ssm_norm, w_branch_gla, w_branch_ssm, w_out,
           router_group, router_expert, moe_w_gate, moe_w_up, moe_w_down):
    t, d = x2.shape
    row = lambda v: v.reshape(1, -1)

    mod = _ada(c, w_ada, b_ada)
    mod8 = jnp.concatenate([mod.reshape(6, d), jnp.zeros((2, d), F32)], axis=0)

    w_big, w_small = _repack(w_in.T)
    p, small = _inproj(x2, mod8, row(norm_pre_mix), w_big, w_small)

    o_gla = _gla(p, small, gla_w_gate_up, row(gla_b_gate), row(gla_norm))
    y_ssm = _ssd(p, small, ssm_conv_w, row(ssm_conv_b),
                 _pad_lanes(ssm_dt_bias, SMALL_DT0), _pad_lanes(ssm_a_log, SMALL_DT0),
                 _pad_lanes(ssm_d, SMALL_DT0), row(ssm_norm))
    merged = _merge(o_gla, y_ssm, w_branch_gla, w_branch_ssm, p)

    w_router = jnp.concatenate([router_group, router_expert,
                                jnp.zeros((d, 128 - MOE_GROUPS - N_EXPERTS), F32)], axis=1)
    x1, h2f, logits = _outproj(merged, w_out.astype(BF16), x2, mod8,
                               row(norm_post_mix), row(norm_pre_ffn), w_router)
    ids, wts = _route(logits)

    row_tok, pos, item_e, item_off, item_nsub, item_fill, n_real = _routing_tables(ids, t)
    y_sorted = _experts(h2f, row_tok, moe_w_gate, moe_w_up, moe_w_down,
                        item_e, item_off, item_nsub, item_fill, n_real)
    return _final(y_sorted, pos, wts, x1, mod8, row(norm_post_ffn))


def kernel(x, c, w_ada, b_ada, norm_pre_mix, norm_post_mix, norm_pre_ffn, norm_post_ffn, w_in, gla_w_gate_up, gla_b_gate, gla_norm, ssm_conv_w, ssm_conv_b, ssm_dt_bias, ssm_a_log, ssm_d, ssm_norm, w_branch_gla, w_branch_ssm, w_out, router_group, router_expert, moe_w_gate, moe_w_up, moe_w_down):
    bsz, seq, d = x.shape
    assert bsz == 1 and d == D_MODEL
    x2 = x.reshape(seq, d)
    params = (w_ada, b_ada, norm_pre_mix, norm_post_mix, norm_pre_ffn, norm_post_ffn, w_in, gla_w_gate_up,
              gla_b_gate, gla_norm, ssm_conv_w, ssm_conv_b, ssm_dt_bias, ssm_a_log, ssm_d, ssm_norm,
              w_branch_gla, w_branch_ssm, w_out, router_group, router_expert, moe_w_gate, moe_w_up, moe_w_down)
    for layer in range(w_ada.shape[0]):
        x2 = _layer(x2, c, *(prm[layer] for prm in params))
    return x2.reshape(bsz, seq, d)
```
